```python
import jax, jax.numpy as jnp
from jax import lax
import numpy as np

D_MODEL = 1024
BATCH = 8
SEQ = 2048
DEPTH = 4
DEC_BATCH = 32
DEC_SEQ = 8
PAST_LEN = 8192
PAGE_SIZE = 128

N_MIXERS = 4
N_POOL_L = (DEPTH + 3) // N_MIXERS
N_RET_L = (DEPTH + 2) // N_MIXERS
N_LRU_L = (DEPTH + 1) // N_MIXERS
N_NSA_L = DEPTH // N_MIXERS
EPS = 1e-6
NEG = -1e30
FORCE = 1e6

E_POOL = 2 * D_MODEL
POOL_WINDOWS = (2, 4, 8, 16)
N_POOL_GROUPS = len(POOL_WINDOWS)
POOL_GW = E_POOL // N_POOL_GROUPS
POOL_BUF = max(POOL_WINDOWS) - 1

R_HEADS = 4
R_DK = D_MODEL // R_HEADS
R_DV = 2 * R_DK
R_CHUNK = 128
ROPE_BASE = 10000.0
RET_IN = 2 * R_HEADS * R_DK + 2 * R_HEADS * R_DV

LRU_BW = 256
D_RNN = -(-(4 * D_MODEL // 3) // LRU_BW) * LRU_BW
LRU_BLOCKS = D_RNN // LRU_BW
CONV_W = 4
LRU_C = 8.0

N_HEADS = 16
N_KV = 4
HPG = N_HEADS // N_KV
HEAD_DIM = 128
L_CMP = 32
S_CMP = 16
L_SEL = 64
N_SEL = 16
WINDOW = 512
WIN_BLK = 128
NSA_SPLITS = (N_HEADS * HEAD_DIM, N_HEADS * HEAD_DIM, 3 * N_HEADS) + (N_KV * HEAD_DIM,) * 6
NSA_IN = sum(NSA_SPLITS)

kernel_name = 'hybrid_pool_retention_rglru_nsa_step'

F32 = jnp.float32


def _rmsnorm(x, g):
    xf = x.astype(F32)
    y = xf * lax.rsqrt(jnp.mean(xf * xf, axis=-1, keepdims=True) + EPS) * g.astype(F32)
    return y.astype(x.dtype)


def _masked_softmax(s, mask):
    p = jax.nn.softmax(jnp.where(mask, s, NEG), axis=-1)
    return jnp.where(mask, p, 0.0)


def _split(x, sizes):
    return jnp.split(x, [int(c) for c in np.cumsum(sizes)[:-1]], axis=-1)


def _pool_mixer(h, buf, pos, w_in, w_grp, b_grp, scale, w_out):
    B, T, _ = h.shape
    u, z = jnp.split(h @ w_in, 2, axis=-1)
    ext = jnp.concatenate([buf.astype(u.dtype), u], axis=1)
    cs = jnp.pad(jnp.cumsum(ext.astype(F32), axis=1), ((0, 0), (1, 0), (0, 0)))
    means = []
    for g, w in enumerate(POOL_WINDOWS):
        c = slice(g * POOL_GW, (g + 1) * POOL_GW)
        win_sum = cs[:, POOL_BUF + 1:POOL_BUF + 1 + T, c] - cs[:, POOL_BUF + 1 - w:POOL_BUF + 1 - w + T, c]
        cnt = jnp.minimum(pos + 1, w).astype(F32)[None, :, None]
        means.append(win_sum / cnt)
    mean = jnp.stack(means, axis=2)
    uf = u.astype(F32).reshape(B, T, N_POOL_GROUPS, POOL_GW)
    mixed = jnp.einsum('btnc,nce->btne', mean - uf, w_grp) + b_grp
    m = mixed.reshape(B, T, E_POOL) * scale
    y = (m * jax.nn.silu(z.astype(F32))).astype(h.dtype) @ w_out
    return y, ext[:, T:]


def _rotate(x, pos):
    half = x.shape[-1] // 2
    inv = 1.0 / (ROPE_BASE ** jnp.linspace(0.0, 1.0, half))
    ang = pos.astype(F32)[:, None] * inv[None, :]
    cos = jnp.cos(ang)[None, :, None, :]
    sin = jnp.sin(ang)[None, :, None, :]
    x1, x2 = x[..., 0::2], x[..., 1::2]
    return jnp.stack([x1 * cos - x2 * sin, x1 * sin + x2 * cos], axis=-1).reshape(x.shape)


def _retention_scan(q, k, v, S0):
    B, T, H, _ = q.shape
    C = R_CHUNK if T % R_CHUNK == 0 else T
    n = T // C
    lg = jnp.log(1.0 - 2.0 ** (-5.0 - jnp.arange(H, dtype=F32)))
    i = jnp.arange(C, dtype=F32)
    diff = i[:, None] - i[None, :]
    intra = jnp.where(diff >= 0, jnp.exp(jnp.maximum(diff, 0.0)[None] * lg[:, None, None]), 0.0)
    q_dec = jnp.exp((i + 1.0)[:, None] * lg[None, :])
    k_dec = jnp.exp((C - 1.0 - i)[:, None] * lg[None, :])
    c_dec = jnp.exp(C * lg)

    def step(S, qkv):
        qc, kc, vc = qkv
        s = jnp.einsum('bihd,bjhd->bhij', qc, kc) * intra
        o = jnp.einsum('bhij,bjhe->bihe', s, vc) + jnp.einsum('bihd,bhde->bihe', qc, S) * q_dec[None, :, :, None]
        S = S * c_dec[None, :, None, None] + jnp.einsum('bjhd,bjhe->bhde', kc * k_dec[None, :, :, None], vc)
        return S, o

    chunks = lambda a: a.reshape(B, n, C, *a.shape[2:]).swapaxes(0, 1)
    S, o = lax.scan(step, S0, (chunks(q), chunks(k), chunks(v)))
    return o.swapaxes(0, 1).reshape(B, T, H, -1), S


def _retention_mixer(h, S0, pos, w_in, w_out):
    B, T, _ = h.shape
    q, k, v, g = _split((h @ w_in).astype(F32), (R_HEADS * R_DK, R_HEADS * R_DK, R_HEADS * R_DV, R_HEADS * R_DV))
    q = _rotate(q.reshape(B, T, R_HEADS, R_DK), pos)
    k = _rotate(k.reshape(B, T, R_HEADS, R_DK), pos) * R_DK ** -0.5
    v = v.reshape(B, T, R_HEADS, R_DV)
    o, S = _retention_scan(q, k, v, S0.astype(F32))
    o = o * lax.rsqrt(jnp.mean(o * o, axis=-1, keepdims=True) + EPS)
    y = (o.reshape(B, T, -1) * jax.nn.silu(g)).astype(h.dtype) @ w_out
    return y, S


def _rglru_mixer(h, conv_buf, h0, w_in, conv_w, conv_b, w_a, b_a, w_x, b_x, lam, w_out):
    B, T, _ = h.shape
    u, z = jnp.split(h @ w_in, 2, axis=-1)
    ext = jnp.concatenate([conv_buf.astype(u.dtype), u], axis=1)
    c = lax.conv_general_dilated(ext.astype(F32), conv_w.astype(F32)[:, None, :], window_strides=(1,),
                                 padding='VALID', dimension_numbers=('NWC', 'WIO', 'NWC'),
                                 feature_group_count=D_RNN) + conv_b
    cb = c.reshape(B, T, LRU_BLOCKS, LRU_BW)
    r = jax.nn.sigmoid(jnp.einsum('btnc,ncd->btnd', cb, w_a).reshape(B, T, D_RNN) + b_a)
    ig = jax.nn.sigmoid(jnp.einsum('btnc,ncd->btnd', cb, w_x).reshape(B, T, D_RNN) + b_x)
    log_a = -LRU_C * r * jax.nn.softplus(-lam)
    a = jnp.exp(log_a)
    bx = jnp.sqrt(-jnp.expm1(2.0 * log_a)) * (ig * c)

    def step(hc, ab):
        hc = ab[0] * hc + ab[1]
        return hc, hc

    hT, hs = lax.scan(step, h0.astype(F32), (a.swapaxes(0, 1), bx.swapaxes(0, 1)))
    y = (hs.swapaxes(0, 1) * jax.nn.silu(z.astype(F32))).astype(h.dtype) @ w_out
    return y, ext[:, T:], hT


def _nsa_project(h, w_in):
    B, T, _ = h.shape
    parts = _split((h @ w_in).astype(F32), NSA_SPLITS)
    q = parts[0].reshape(B, T, N_KV, HPG, HEAD_DIM)
    z = parts[1]
    gates = jax.nn.sigmoid(parts[2]).reshape(B, T, 3, N_KV, HPG)
    kv = [a.reshape(B, T, N_KV, HEAD_DIM) for a in parts[3:]]
    return q, z, gates, kv


def _compress(rows, w_pos, w_lin):
    B, T = rows.shape[:2]
    n_sub = -(-T // S_CMP)
    rows = jnp.pad(rows, ((0, 0), (0, n_sub * S_CMP - T), (0, 0), (0, 0)))
    sub = rows.reshape(B, n_sub, S_CMP, N_KV, HEAD_DIM)
    span = L_CMP // S_CMP
    n_c = n_sub - span + 1
    pooled = sum(jnp.einsum('bnlgd,lg->bngd', sub[:, r:r + n_c], w_pos[r * S_CMP:(r + 1) * S_CMP])
                 for r in range(span))
    return jnp.einsum('bngd,gde->bnge', pooled, w_lin)


def _attend(q, k, v, mask):
    s = jnp.einsum('btghd,bsgd->btghs', q, k) * HEAD_DIM ** -0.5
    p = _masked_softmax(s, mask[None, :, None, None, :])
    return jnp.einsum('btghs,bsgd->btghd', p, v), p


def _cmp_branch(q, pos, ck, cv):
    end = jnp.arange(ck.shape[1]) * S_CMP + (L_CMP - 1)
    return _attend(q, ck, cv, end[None, :] <= pos[:, None])


def _select_blocks(p_cmp, pos, n_blk):
    imp_c = p_cmp.sum(axis=3)
    r, s = L_SEL // S_CMP, L_CMP // S_CMP
    n_c = imp_c.shape[-1]
    pp = jnp.pad(imp_c, ((0, 0), (0, 0), (0, 0), (s - 1, r * n_blk - n_c)))
    imp = sum(pp[..., o:o + r * (n_blk - 1) + 1:r] for o in range(r + s - 1))
    j = jnp.arange(n_blk)[None, :]
    cur = (pos // L_SEL)[:, None]
    valid = (j * L_SEL <= pos[:, None])[None, :, None, :]
    forced = ((j == 0) | (j == cur) | (j == cur - 1))[None, :, None, :]
    score = jnp.where(forced, FORCE, jnp.where(valid, imp, -1.0))
    vals, idx = lax.top_k(score, min(N_SEL, n_blk))
    return idx, vals >= 0.0


def _sel_attend(q, pos, idx, ok, fetch):
    rr = jnp.arange(L_SEL)

    def one(args):
        q_t, t, idx_t, ok_t = args
        kt, vt = fetch(idx_t)
        kpos = idx_t[..., None] * L_SEL + rr
        mask = ok_t[..., None] & (kpos <= t)
        s = jnp.einsum('bghd,bgkld->bghkl', q_t, kt) * HEAD_DIM ** -0.5
        Bq, G, Hg, K, Ls = s.shape
        p = _masked_softmax(s.reshape(Bq, G, Hg, K * Ls), mask.reshape(Bq, G, 1, K * Ls))
        return jnp.einsum('bghn,bgnd->bghd', p, vt.reshape(Bq, G, K * Ls, -1))

    o = lax.map(one, (q.swapaxes(0, 1), pos, idx.swapaxes(0, 1), ok.swapaxes(0, 1)))
    return o.swapaxes(0, 1)


def _win_prompt(q, k, v):
    B, T = q.shape[:2]
    nb = T // WIN_BLK
    span = WINDOW + WIN_BLK
    kp = jnp.pad(k, ((0, 0), (WINDOW, 0), (0, 0), (0, 0)))
    vp = jnp.pad(v, ((0, 0), (WINDOW, 0), (0, 0), (0, 0)))
    qb = q.reshape(B, nb, WIN_BLK, N_KV, HPG, HEAD_DIM).swapaxes(0, 1)
    r = jnp.arange(WIN_BLK)
    c = jnp.arange(span)

    def block(args):
        i, q_i = args
        k_i = lax.dynamic_slice_in_dim(kp, i * WIN_BLK, span, axis=1)
        v_i = lax.dynamic_slice_in_dim(vp, i * WIN_BLK, span, axis=1)
        qpos = i * WIN_BLK + r
        kpos = i * WIN_BLK - WINDOW + c
        mask = (kpos[None] <= qpos[:, None]) & (kpos[None] > qpos[:, None] - WINDOW) & (kpos[None] >= 0)
        return _attend(q_i, k_i, v_i, mask)[0]

    o = lax.map(block, (jnp.arange(nb), qb))
    return o.swapaxes(0, 1).reshape(B, T, N_KV, HPG, HEAD_DIM)


def _nsa_out(gates, o_c, o_s, o_w, z, w_out, dtype):
    o = gates[:, :, 0, ..., None] * o_c + gates[:, :, 1, ..., None] * o_s + gates[:, :, 2, ..., None] * o_w
    B, T = o.shape[:2]
    return (o.reshape(B, T, -1) * jax.nn.silu(z)).astype(dtype) @ w_out


def _nsa_prompt(h, w_in, pk, lk, pv, lv, w_out):
    q, z, gates, (kc, vc, ks, vs, kw, vw) = _nsa_project(h, w_in)
    B, T = h.shape[:2]
    pos = jnp.arange(T, dtype=jnp.int32)
    o_c, p_c = _cmp_branch(q, pos, _compress(kc, pk, lk), _compress(vc, pv, lv))
    n_blk = -(-T // L_SEL)
    idx, ok = _select_blocks(p_c, pos, n_blk)
    pad = n_blk * L_SEL - T
    kb = jnp.pad(ks, ((0, 0), (0, pad), (0, 0), (0, 0))).reshape(B, n_blk, L_SEL, N_KV, HEAD_DIM)
    vb = jnp.pad(vs, ((0, 0), (0, pad), (0, 0), (0, 0))).reshape(B, n_blk, L_SEL, N_KV, HEAD_DIM)
    bi = jnp.arange(B)[:, None, None]
    gi = jnp.arange(N_KV)[None, :, None]
    fetch = lambda ix: (kb[bi, ix, :, gi, :], vb[bi, ix, :, gi, :])
    o_s = _sel_attend(q, pos, idx, ok, fetch)
    o_w = _win_prompt(q, kw, vw)
    y = _nsa_out(gates, o_c, o_s, o_w, z, w_out, h.dtype)
    wb = min(WINDOW, T)
    return y, (kc, vc, ks, vs, kw[:, T - wb:], vw[:, T - wb:])


def _nsa_sample(h, j, pool_kc, pool_vc, pool_ks, pool_vs, win_k, win_v, page_table, w_in, pk, lk, pv, lv, w_out):
    q, z, gates, (kc, vc, ks, vs, kw, vw) = _nsa_project(h, w_in)
    B, L = h.shape[:2]
    past = page_table.shape[1] * PAGE_SIZE
    pos = past + jnp.arange(L, dtype=jnp.int32)

    def summaries(args):
        pt, kn, vn = args
        kp = pool_kc[j, pt].reshape(past, N_KV, HEAD_DIM).astype(F32)
        vp = pool_vc[j, pt].reshape(past, N_KV, HEAD_DIM).astype(F32)
        ck = _compress(jnp.concatenate([kp, kn], axis=0)[None], pk, lk)[0]
        cv = _compress(jnp.concatenate([vp, vn], axis=0)[None], pv, lv)[0]
        return ck, cv

    ck, cv = lax.map(summaries, (page_table, kc, vc))
    o_c, p_c = _cmp_branch(q, pos, ck, cv)
    bpp = PAGE_SIZE // L_SEL
    n_past_blk = past // L_SEL
    n_new = -(-L // L_SEL)
    idx, ok = _select_blocks(p_c, pos, n_past_blk + n_new)
    pad = n_new * L_SEL - L
    kb_new = jnp.pad(ks, ((0, 0), (0, pad), (0, 0), (0, 0))).reshape(B, n_new, L_SEL, N_KV, HEAD_DIM)
    vb_new = jnp.pad(vs, ((0, 0), (0, pad), (0, 0), (0, 0))).reshape(B, n_new, L_SEL, N_KV, HEAD_DIM)
    bi = jnp.arange(B)[:, None, None]
    gi = jnp.arange(N_KV)[None, :, None]
    rr = jnp.arange(L_SEL)

    def fetch(ix):
        jp = jnp.minimum(ix, n_past_blk - 1)
        phys = page_table[bi, jp // bpp][..., None]
        row = (jp % bpp)[..., None] * L_SEL + rr
        gg = gi[..., None]
        kp = pool_ks[j, phys, row, gg].astype(F32)
        vp = pool_vs[j, phys, row, gg].astype(F32)
        jn = jnp.clip(ix - n_past_blk, 0, n_new - 1)
        is_new = (ix >= n_past_blk)[..., None, None]
        return (jnp.where(is_new, kb_new[bi, jn, :, gi, :], kp), jnp.where(is_new, vb_new[bi, jn, :, gi, :], vp))

    o_s = _sel_attend(q, pos, idx, ok, fetch)
    wb = win_k.shape[2]
    keys = jnp.concatenate([win_k[j].astype(F32), kw], axis=1)
    vals = jnp.concatenate([win_v[j].astype(F32), vw], axis=1)
    kpos = past - wb + jnp.arange(wb + L)
    mask = (kpos[None] <= pos[:, None]) & (kpos[None] > pos[:, None] - WINDOW)
    o_w, _ = _attend(q, keys, vals, mask)
    y = _nsa_out(gates, o_c, o_s, o_w, z, w_out, h.dtype)
    return y, (kc, vc, ks, vs, keys[:, L:], vals[:, L:])


def setup_inputs(seed: int = 0) -> dict:
    key = jax.random.key(seed)
    keys = iter(jax.random.split(key, 64))

    def nrm(shape, scale):
        return jax.random.normal(next(keys), shape, F32) * scale

    n_pages = PAST_LEN // PAGE_SIZE
    n_phys = (5 * DEC_BATCH * n_pages + 3) // 4
    wb = min(WINDOW, PAST_LEN)
    page_table = jax.random.permutation(next(keys), n_phys)[:DEC_BATCH * n_pages].reshape(DEC_BATCH, n_pages).astype(jnp.int32)
    a0 = jax.random.uniform(next(keys), (N_LRU_L, D_RNN), F32, 0.9, 0.999)
    sig = a0 ** (1.0 / LRU_C)
    lru_lam = jnp.log(sig) - jnp.log1p(-sig)
    pool_shape = (N_NSA_L, n_phys, PAGE_SIZE, N_KV, HEAD_DIM)
    win_shape = (N_NSA_L, DEC_BATCH, wb, N_KV, HEAD_DIM)
    return {
        'x_prompt': nrm((BATCH, SEQ, D_MODEL), 1.0),
        'x_sample': nrm((DEC_BATCH, DEC_SEQ, D_MODEL), 1.0),
        'state_pool': nrm((N_POOL_L, DEC_BATCH, POOL_BUF, E_POOL), 1.0),
        'state_ret': nrm((N_RET_L, DEC_BATCH, R_HEADS, R_DK, R_DV), 0.5),
        'state_conv': nrm((N_LRU_L, DEC_BATCH, CONV_W - 1, D_RNN), 1.0),
        'state_lru': nrm((N_LRU_L, DEC_BATCH, D_RNN), 0.5),
        'cache_cmp_k': nrm(pool_shape, 1.0),
        'cache_cmp_v': nrm(pool_shape, 1.0),
        'cache_sel_k': nrm(pool_shape, 1.0),
        'cache_sel_v': nrm(pool_shape, 1.0),
        'cache_win_k': nrm(win_shape, 1.0),
        'cache_win_v': nrm(win_shape, 1.0),
        'page_table': page_table,
        'norm_pre': 1.0 + nrm((DEPTH, D_MODEL), 0.05),
        'norm_post': 1.0 + nrm((DEPTH, D_MODEL), 0.05),
        'pool_w_in': nrm((N_POOL_L, D_MODEL, 2 * E_POOL), D_MODEL ** -0.5),
        'pool_w_grp': nrm((N_POOL_L, N_POOL_GROUPS, POOL_GW, POOL_GW), POOL_GW ** -0.5),
        'pool_b_grp': nrm((N_POOL_L, N_POOL_GROUPS, POOL_GW), 0.02),
        'pool_scale': 1.0 + nrm((N_POOL_L, E_POOL), 0.1),
        'pool_w_out': nrm((N_POOL_L, E_POOL, D_MODEL), E_POOL ** -0.5),
        'ret_w_in': nrm((N_RET_L, D_MODEL, RET_IN), D_MODEL ** -0.5),
        'ret_w_out': nrm((N_RET_L, R_HEADS * R_DV, D_MODEL), (R_HEADS * R_DV) ** -0.5),
        'lru_w_in': nrm((N_LRU_L, D_MODEL, 2 * D_RNN), D_MODEL ** -0.5),
        'lru_conv_w': nrm((N_LRU_L, CONV_W, D_RNN), CONV_W ** -0.5),
        'lru_conv_b': nrm((N_LRU_L, D_RNN), 0.02),
        'lru_w_a': nrm((N_LRU_L, LRU_BLOCKS, LRU_BW, LRU_BW), LRU_BW ** -0.5),
        'lru_b_a': nrm((N_LRU_L, D_RNN), 0.02),
        'lru_w_x': nrm((N_LRU_L, LRU_BLOCKS, LRU_BW, LRU_BW), LRU_BW ** -0.5),
        'lru_b_x': nrm((N_LRU_L, D_RNN), 0.02),
        'lru_lam': lru_lam,
        'lru_w_out': nrm((N_LRU_L, D_RNN, D_MODEL), D_RNN ** -0.5),
        'nsa_w_in': nrm((N_NSA_L, D_MODEL, NSA_IN), D_MODEL ** -0.5),
        'nsa_cmp_pos_k': (1.0 + nrm((N_NSA_L, L_CMP, N_KV), 0.2)) * L_CMP ** -0.5,
        'nsa_cmp_lin_k': nrm((N_NSA_L, N_KV, HEAD_DIM, HEAD_DIM), HEAD_DIM ** -0.5),
        'nsa_cmp_pos_v': (1.0 + nrm((N_NSA_L, L_CMP, N_KV), 0.2)) * L_CMP ** -0.5,
        'nsa_cmp_lin_v': nrm((N_NSA_L, N_KV, HEAD_DIM, HEAD_DIM), HEAD_DIM ** -0.5),
        'nsa_w_out': nrm((N_NSA_L, N_HEADS * HEAD_DIM, D_MODEL), (N_HEADS * HEAD_DIM) ** -0.5),
    }


def reference(x_prompt, x_sample, state_pool, state_ret, state_conv, state_lru,
              cache_cmp_k, cache_cmp_v, cache_sel_k, cache_sel_v, cache_win_k, cache_win_v, page_table,
              norm_pre, norm_post, pool_w_in, pool_w_grp, pool_b_grp, pool_scale, pool_w_out,
              ret_w_in, ret_w_out, lru_w_in, lru_conv_w, lru_conv_b, lru_w_a, lru_b_a, lru_w_x, lru_b_x,
              lru_lam, lru_w_out, nsa_w_in, nsa_cmp_pos_k, nsa_cmp_lin_k, nsa_cmp_pos_v, nsa_cmp_lin_v, nsa_w_out):
    B, T, _ = x_prompt.shape
    past = page_table.shape[1] * PAGE_SIZE
    pos_p = jnp.arange(T, dtype=jnp.int32)
    pos_s = past + jnp.arange(x_sample.shape[1], dtype=jnp.int32)
    xp, xs = x_prompt, x_sample
    pool_p, pool_s, ret_p, ret_s = [], [], [], []
    conv_p, conv_s, lru_p, lru_s = [], [], [], []
    nsa_p, nsa_s = [], []
    for i in range(DEPTH):
        kind, j = i % N_MIXERS, i // N_MIXERS
        hp = _rmsnorm(xp, norm_pre[i])
        hs = _rmsnorm(xs, norm_pre[i])
        if kind == 0:
            w = (pool_w_in[j], pool_w_grp[j], pool_b_grp[j], pool_scale[j], pool_w_out[j])
            yp, sp = _pool_mixer(hp, jnp.zeros((B, POOL_BUF, E_POOL), hp.dtype), pos_p, *w)
            ys, ss = _pool_mixer(hs, state_pool[j], pos_s, *w)
            pool_p.append(sp)
            pool_s.append(ss)
        elif kind == 1:
            yp, sp = _retention_mixer(hp, jnp.zeros((B, R_HEADS, R_DK, R_DV), F32), pos_p, ret_w_in[j], ret_w_out[j])
            ys, ss = _retention_mixer(hs, state_ret[j], pos_s, ret_w_in[j], ret_w_out[j])
            ret_p.append(sp)
            ret_s.append(ss)
        elif kind == 2:
            w = (lru_w_in[j], lru_conv_w[j], lru_conv_b[j], lru_w_a[j], lru_b_a[j], lru_w_x[j], lru_b_x[j],
                 lru_lam[j], lru_w_out[j])
            yp, cp, lp = _rglru_mixer(hp, jnp.zeros((B, CONV_W - 1, D_RNN), hp.dtype), jnp.zeros((B, D_RNN), F32), *w)
            ys, cs_, ls = _rglru_mixer(hs, state_conv[j], state_lru[j], *w)
            conv_p.append(cp)
            conv_s.append(cs_)
            lru_p.append(lp)
            lru_s.append(ls)
        else:
            w = (nsa_w_in[j], nsa_cmp_pos_k[j], nsa_cmp_lin_k[j], nsa_cmp_pos_v[j], nsa_cmp_lin_v[j], nsa_w_out[j])
            yp, sp = _nsa_prompt(hp, *w)
            ys, ss = _nsa_sample(hs, j, cache_cmp_k, cache_cmp_v, cache_sel_k, cache_sel_v,
                                 cache_win_k, cache_win_v, page_table, *w)
            nsa_p.append(sp)
            nsa_s.append(ss)
        xp = xp + _rmsnorm(yp, norm_post[i]).astype(xp.dtype)
        xs = xs + _rmsnorm(ys, norm_post[i]).astype(xs.dtype)
    np_ = [jnp.stack([e[m] for e in nsa_p]) for m in range(6)]
    ns_ = [jnp.stack([e[m] for e in nsa_s]) for m in range(6)]
    return (xp, xs, jnp.stack(pool_p), jnp.stack(pool_s), jnp.stack(ret_p), jnp.stack(ret_s),
            jnp.stack(conv_p), jnp.stack(conv_s), jnp.stack(lru_p), jnp.stack(lru_s),
            np_[0], ns_[0], np_[1], ns_[1], np_[2], ns_[2], np_[3], ns_[3], np_[4], ns_[4], np_[5], ns_[5])
```

```python
import functools
import math

import jax
import jax.numpy as jnp
from jax import lax
from jax.experimental import pallas as pl
from jax.experimental.pallas import tpu as pltpu

F32 = jnp.float32
BF16 = jnp.bfloat16

EPS = 1e-6
NEG = -1e30
FORCE = 1e6

PAGE_SIZE = 128
POOL_WINDOWS = (2, 4, 8, 16)
R_HEADS = 4
R_CHUNK = 128
ROPE_BASE = 10000.0
LRU_BW = 256
CONV_W = 4
LRU_C = 8.0
N_HEADS = 16
N_KV = 4
HPG = N_HEADS // N_KV
HEAD_DIM = 128
L_CMP = 32
S_CMP = 16
L_SEL = 64
N_SEL = 16
WINDOW = 512
KV_W = N_KV * HEAD_DIM
SUB_PER_PAGE = PAGE_SIZE // S_CMP
PAGES_PER_STEP = 8

LANES = 128
SUBLANES = 8
VMEM_LIMIT = 56 * 1024 * 1024


def _cparams(sem):
    return pltpu.CompilerParams(dimension_semantics=sem, vmem_limit_bytes=VMEM_LIMIT)


def _act_dtype(rows):
    return BF16 if rows % (2 * SUBLANES) == 0 else F32


def _dot(a, b):
    return jnp.dot(a, b, preferred_element_type=F32)


def _dot_nt(a, b):
    return lax.dot_general(a, b, (((1,), (1,)), ((), ())), preferred_element_type=F32)


def _dot_tn(a, b):
    return lax.dot_general(a, b, (((0,), (0,)), ((), ())), preferred_element_type=F32)


def _silu(x):
    return x * jax.nn.sigmoid(x)


def _iota(shape, dim):
    return lax.broadcasted_iota(jnp.int32, shape, dim)


def _norm_mm_body(x_ref, g_ref, w_ref, o_ref, h_ref):
    @pl.when(pl.program_id(1) == 0)
    def _():
        x = x_ref[...]
        ms = jnp.mean(x * x, axis=-1, keepdims=True)
        h_ref[...] = (x * lax.rsqrt(ms + EPS) * g_ref[...]).astype(BF16)

    o_ref[...] = _dot(h_ref[...], w_ref[...]).reshape(o_ref.shape)


def _norm_matmul(x, x_map, g, w, *, m, tm, tn, out_shape, out_block, out_map):
    d, n = w.shape
    return pl.pallas_call(
        _norm_mm_body,
        grid=(m // tm, n // tn),
        in_specs=[
            pl.BlockSpec((tm, d), x_map),
            pl.BlockSpec((1, d), lambda i, j: (0, 0)),
            pl.BlockSpec((d, tn), lambda i, j: (0, j)),
        ],
        out_specs=pl.BlockSpec(out_block, out_map),
        out_shape=jax.ShapeDtypeStruct(out_shape, F32),
        scratch_shapes=[pltpu.VMEM((tm, d), BF16)],
        compiler_params=_cparams(("parallel", "arbitrary")),
        name="norm_matmul",
    )(x, g, w)


def _out_proj_body(a_ref, w_ref, x_ref, g_ref, o_ref):
    y = _dot(a_ref[...].astype(BF16), w_ref[...])
    ms = jnp.mean(y * y, axis=-1, keepdims=True)
    o_ref[...] = x_ref[...] + y * lax.rsqrt(ms + EPS) * g_ref[...]


def _out_proj(a, w, x, x_map, g, *, m, tm, out_shape, out_map):
    e, d = w.shape
    return pl.pallas_call(
        _out_proj_body,
        grid=(m // tm,),
        in_specs=[
            pl.BlockSpec((tm, e), lambda i: (i, 0)),
            pl.BlockSpec((e, d), lambda i: (0, 0)),
            pl.BlockSpec((tm, d), x_map),
            pl.BlockSpec((1, d), lambda i: (0, 0)),
        ],
        out_specs=pl.BlockSpec((tm, d), out_map),
        out_shape=jax.ShapeDtypeStruct(out_shape, F32),
        compiler_params=_cparams(("parallel",)),
        name="out_proj",
    )(a, w, x, g)


POOL_HALO = 16
POOL_PAD = 8


def _pool_body(u_ref, z_ref, buf_ref, wg_ref, bg_ref, sc_ref, a_ref, st_ref, ext, sa, sb,
               *, tt, pos0, gw):
    ti = pl.program_id(1)
    lo = POOL_PAD
    top = POOL_PAD + POOL_HALO
    rows = top + tt

    @pl.when(ti == 0)
    def _():
        ext[0:lo, :] = jnp.zeros((lo, ext.shape[1]), F32)
        ext[lo:top, :] = buf_ref[0]

    @pl.when(ti > 0)
    def _():
        ext[lo:top, :] = ext[lo + tt:top + tt, :]

    u = u_ref[...]
    ext[top:rows, :] = u
    sa[0:lo, :] = jnp.zeros((lo, gw), F32)
    sb[0:lo, :] = jnp.zeros((lo, gw), F32)
    pos = _iota((tt, 1), 0) + (pos0 + ti * tt)
    for g, w in enumerate(POOL_WINDOWS):
        cs = slice(g * gw, (g + 1) * gw)
        sa[lo:rows, :] = ext[lo:rows, cs] + ext[lo - 1:rows - 1, cs]
        cur, oth, sh = sa, sb, 2
        while sh < w:
            oth[lo:rows, :] = cur[lo:rows, :] + cur[lo - sh:rows - sh, :]
            cur, oth, sh = oth, cur, sh * 2
        cnt = jnp.minimum(pos + 1, w).astype(F32)
        mean = cur[top:rows, :] / cnt
        mixed = _dot((mean - u[:, cs]).astype(BF16), wg_ref[g]) + bg_ref[:, cs]
        a_ref[:, cs] = (mixed * sc_ref[:, cs] * _silu(z_ref[:, cs])).astype(a_ref.dtype)
    st_ref[0] = ext[lo + tt:top + tt, :]


def _pool_mixer(uz, buf16, wg, bg, sc, *, b, t, tt, pos0):
    e = buf16.shape[-1]
    gw = e // len(POOL_WINDOWS)
    nt = t // tt
    rows = POOL_PAD + POOL_HALO + tt
    return pl.pallas_call(
        functools.partial(_pool_body, tt=tt, pos0=pos0, gw=gw),
        grid=(b, nt),
        in_specs=[
            pl.BlockSpec((tt, e), lambda i, j: (i * nt + j, 0)),
            pl.BlockSpec((tt, e), lambda i, j: (i * nt + j, 1)),
            pl.BlockSpec((1, POOL_HALO, e), lambda i, j: (i, 0, 0)),
            pl.BlockSpec((len(POOL_WINDOWS), gw, gw), lambda i, j: (0, 0, 0)),
            pl.BlockSpec((1, e), lambda i, j: (0, 0)),
            pl.BlockSpec((1, e), lambda i, j: (0, 0)),
        ],
        out_specs=[
            pl.BlockSpec((tt, e), lambda i, j: (i * nt + j, 0)),
            pl.BlockSpec((1, POOL_HALO, e), lambda i, j: (i, 0, 0)),
        ],
        out_shape=[
            jax.ShapeDtypeStruct((b * t, e), _act_dtype(tt)),
            jax.ShapeDtypeStruct((b, POOL_HALO, e), F32),
        ],
        scratch_shapes=[
            pltpu.VMEM((rows, e), F32),
            pltpu.VMEM((rows, gw), F32),
            pltpu.VMEM((rows, gw), F32),
        ],
        compiler_params=_cparams(("parallel", "arbitrary")),
        name="pool_mixer",
    )(uz, uz, buf16, wg, bg, sc)


def _ret_body(lg_ref, q_ref, k_ref, v_ref, g_ref, cos_ref, sin_ref, s0_ref, a_ref, so_ref, s_scr,
              *, c_len, c_pad, dk):
    h = pl.program_id(1)
    c = pl.program_id(2)

    @pl.when(c == 0)
    def _():
        s_scr[...] = s0_ref[0, 0]

    lg = lg_ref[h]
    cosv = cos_ref[...]
    sinv = sin_ref[...]
    even = (_iota((c_len, dk), 1) & 1) == 0

    def rot(x):
        nxt = pltpu.roll(x, dk - 1, axis=1)
        prv = pltpu.roll(x, 1, axis=1)
        return x * cosv + jnp.where(even, nxt, prv) * sinv

    def pad(x):
        if c_pad == c_len:
            return x
        return jnp.concatenate([x, jnp.zeros((c_pad - c_len, x.shape[1]), x.dtype)], axis=0)

    q = pad(rot(q_ref[...]))
    k = pad(rot(k_ref[...]) * (dk ** -0.5))
    v = pad(v_ref[...])
    i_col = _iota((c_pad, 1), 0).astype(F32)
    j_row = _iota((1, c_pad), 1).astype(F32)
    diff = i_col - j_row
    intra = jnp.where(diff >= 0, jnp.exp(jnp.maximum(diff, 0.0) * lg), 0.0)
    q_dec = jnp.exp((i_col + 1.0) * lg)
    k_dec = jnp.exp((c_len - 1.0 - i_col) * lg)
    c_dec = jnp.exp(jnp.full((1, 1), float(c_len), F32) * lg)

    qb = q.astype(BF16)
    vb = v.astype(BF16)
    s = _dot_nt(qb, k.astype(BF16)) * intra
    s_old = s_scr[...]
    o = _dot(s.astype(BF16), vb) + _dot(qb, s_old.astype(BF16)) * q_dec
    s_scr[...] = s_old * c_dec + _dot_tn((k * k_dec).astype(BF16), vb)
    o = o[0:c_len]
    o = o * lax.rsqrt(jnp.mean(o * o, axis=-1, keepdims=True) + EPS)
    a_ref[...] = (o * _silu(g_ref[...])).astype(a_ref.dtype)

    @pl.when(c == pl.num_programs(2) - 1)
    def _():
        so_ref[0, 0] = s_scr[...]


def _retention_mixer(proj, cos_t, sin_t, lg, s0, *, b, t, dk, dv):
    c_len = R_CHUNK if t % R_CHUNK == 0 else t
    c_pad = max(c_len, R_CHUNK)
    nc = t // c_len
    nh = R_HEADS
    k_off = nh
    v_off = 2 * nh * dk // dv
    g_off = v_off + nh
    row = lambda i, h, c: i * nc + c
    return pl.pallas_call(
        functools.partial(_ret_body, c_len=c_len, c_pad=c_pad, dk=dk),
        grid=(b, nh, nc),
        in_specs=[
            pl.BlockSpec(memory_space=pltpu.SMEM),
            pl.BlockSpec((c_len, dk), lambda i, h, c: (row(i, h, c), h)),
            pl.BlockSpec((c_len, dk), lambda i, h, c: (row(i, h, c), k_off + h)),
            pl.BlockSpec((c_len, dv), lambda i, h, c: (row(i, h, c), v_off + h)),
            pl.BlockSpec((c_len, dv), lambda i, h, c: (row(i, h, c), g_off + h)),
            pl.BlockSpec((c_len, dk), lambda i, h, c: (c, 0)),
            pl.BlockSpec((c_len, dk), lambda i, h, c: (c, 0)),
            pl.BlockSpec((1, 1, dk, dv), lambda i, h, c: (i, h, 0, 0)),
        ],
        out_specs=[
            pl.BlockSpec((c_len, dv), lambda i, h, c: (row(i, h, c), h)),
            pl.BlockSpec((1, 1, dk, dv), lambda i, h, c: (i, h, 0, 0)),
        ],
        out_shape=[
            jax.ShapeDtypeStruct((b * t, nh * dv), _act_dtype(c_len)),
            jax.ShapeDtypeStruct((b, nh, dk, dv), F32),
        ],
        scratch_shapes=[pltpu.VMEM((dk, dv), F32)],
        compiler_params=_cparams(("parallel", "parallel", "arbitrary")),
        name="retention_mixer",
    )(lg, proj, proj, proj, proj, cos_t, sin_t, s0)


def _lru_body(u_ref, z_ref, buf_ref, cw_ref, cb_ref, wa_ref, ba_ref, wx_ref, bx_ref, lam_ref, h0_ref,
              a_ref, cso_ref, ho_ref, ext, a_s, b_s, h_s, hc, *, tt, nb):
    ti = pl.program_id(1)
    halo = (CONV_W - 1) * nb
    rows = tt * nb

    @pl.when(ti == 0)
    def _():
        ext[0:halo, :] = buf_ref[...]
        hc[...] = h0_ref[...]

    @pl.when(ti > 0)
    def _():
        ext[0:halo, :] = ext[rows:rows + halo, :]

    ext[halo:halo + rows, :] = u_ref[...]
    cw = cw_ref[...]
    c = cb_ref[...] + ext[0:rows, :] * cw[0:1, :]
    for kk in range(1, CONV_W):
        c = c + ext[kk * nb:kk * nb + rows, :] * cw[kk:kk + 1, :]
    cb16 = c.astype(BF16)
    r = jax.nn.sigmoid(_dot(cb16, wa_ref[0]) + ba_ref[...])
    ig = jax.nn.sigmoid(_dot(cb16, wx_ref[0]) + bx_ref[...])
    nl = -lam_ref[...]
    softplus = jnp.maximum(nl, 0.0) + jnp.log1p(jnp.exp(-jnp.abs(nl)))
    log_a = (-LRU_C) * r * softplus
    a_s[...] = jnp.exp(log_a)
    b_s[...] = jnp.sqrt(1.0 - jnp.exp(2.0 * log_a)) * (ig * c)

    def step(t, h):
        rs = pl.ds(pl.multiple_of(t * nb, SUBLANES), nb)
        h = a_s[rs, :] * h + b_s[rs, :]
        h_s[rs, :] = h
        return h

    hc[...] = lax.fori_loop(0, tt, step, hc[...], unroll=8)
    a_ref[...] = (h_s[...] * _silu(z_ref[...])).astype(BF16)
    cso_ref[...] = ext[rows:rows + halo, :]
    ho_ref[...] = hc[...]


def _lru_mixer(uz, buf, cw, cb, wa, ba, wx, bx, lam, h0, *, nb, t, tt):
    d_rnn = h0.shape[-1]
    nblk = d_rnn // LRU_BW
    nt = t // tt
    rows = tt * nb
    halo = (CONV_W - 1) * nb
    vec = lambda: pl.BlockSpec((1, LRU_BW), lambda j, i: (0, j))
    return pl.pallas_call(
        functools.partial(_lru_body, tt=tt, nb=nb),
        grid=(nblk, nt),
        in_specs=[
            pl.BlockSpec((rows, LRU_BW), lambda j, i: (i, j)),
            pl.BlockSpec((rows, LRU_BW), lambda j, i: (i, nblk + j)),
            pl.BlockSpec((halo, LRU_BW), lambda j, i: (0, j)),
            pl.BlockSpec((CONV_W, LRU_BW), lambda j, i: (0, j)),
            vec(),
            pl.BlockSpec((1, LRU_BW, LRU_BW), lambda j, i: (j, 0, 0)),
            vec(),
            pl.BlockSpec((1, LRU_BW, LRU_BW), lambda j, i: (j, 0, 0)),
            vec(),
            vec(),
            pl.BlockSpec((nb, LRU_BW), lambda j, i: (0, j)),
        ],
        out_specs=[
            pl.BlockSpec((rows, LRU_BW), lambda j, i: (i, j)),
            pl.BlockSpec((halo, LRU_BW), lambda j, i: (0, j)),
            pl.BlockSpec((nb, LRU_BW), lambda j, i: (0, j)),
        ],
        out_shape=[
            jax.ShapeDtypeStruct((t * nb, d_rnn), BF16),
            jax.ShapeDtypeStruct((halo, d_rnn), F32),
            jax.ShapeDtypeStruct((nb, d_rnn), F32),
        ],
        scratch_shapes=[
            pltpu.VMEM((halo + rows, LRU_BW), F32),
            pltpu.VMEM((rows, LRU_BW), F32),
            pltpu.VMEM((rows, LRU_BW), F32),
            pltpu.VMEM((rows, LRU_BW), F32),
            pltpu.VMEM((nb, LRU_BW), F32),
        ],
        compiler_params=_cparams(("parallel", "arbitrary")),
        name="lru_mixer",
    )(uz, uz, buf, cw, cb, wa, ba, wx, bx, lam, h0)


def _softmax_masked(s, mask):
    sm = jnp.where(mask, s, NEG)
    m = jnp.max(sm, axis=-1, keepdims=True)
    e = jnp.where(mask, jnp.exp(sm - m), 0.0)
    l = jnp.sum(e, axis=-1, keepdims=True)
    return e / jnp.where(l > 0.0, l, 1.0)


def _dot_f32_by_01(x, m01):
    hi = x.astype(BF16)
    r1 = x - hi.astype(F32)
    mid = r1.astype(BF16)
    lo = (r1 - mid.astype(F32)).astype(BF16)
    return _dot(hi, m01) + _dot(mid, m01) + _dot(lo, m01)


def _block_importance_matrix(n_cmp_lanes, n_blk_lanes, n_blk):
    r, s = L_SEL // S_CMP, L_CMP // S_CMP
    i = _iota((n_cmp_lanes, n_blk_lanes), 0)
    j = _iota((n_cmp_lanes, n_blk_lanes), 1)
    hit = (i >= r * j - (s - 1)) & (i <= r * j + (r - 1)) & (j < n_blk)
    return jnp.where(hit, 1.0, 0.0).astype(BF16)


def _select_blocks(imp, pos, n_blk):
    j = _iota(imp.shape, 1)
    cur = lax.shift_right_logical(pos, int(math.log2(L_SEL)))
    valid = j * L_SEL <= pos
    forced = (j == 0) | (j == cur) | (j == cur - 1)
    score = jnp.where(forced, FORCE, jnp.where(valid, imp, -1.0))
    cnt = jnp.zeros(imp.shape, F32)
    for i in range(n_blk):
        si = score[:, i:i + 1]
        ahead = (si > score) | ((si == score) & (j > i))
        cnt = cnt + jnp.where(ahead, 1.0, 0.0)
    keep = (cnt < float(min(N_SEL, n_blk))) & (score >= 0.0) & (j < n_blk)
    return jnp.where(keep, 1.0, 0.0)


def _online_update(m_ref, l_ref, acc_ref, s, mask, pv):
    sm = jnp.where(mask, s, NEG)
    m_old = m_ref[...]
    m_new = jnp.maximum(m_old, jnp.max(sm, axis=-1, keepdims=True))
    p = jnp.where(mask, jnp.exp(sm - m_new), 0.0)
    alpha = jnp.exp(m_old - m_new)
    l_ref[...] = alpha * l_ref[...] + jnp.sum(p, axis=-1, keepdims=True)
    acc_ref[...] = alpha * acc_ref[...] + pv(p)
    m_ref[...] = m_new


def _online_finish(l_ref, acc_ref):
    l = l_ref[...]
    return acc_ref[...] / jnp.where(l > 0.0, l, 1.0)


def _cmp_accumulate(page, w2_ref, acc_a, acc_b, row0):
    pb = page.astype(BF16)
    for g in range(N_KV):
        cs = slice(g * HEAD_DIM, (g + 1) * HEAD_DIM)
        ab = _dot(w2_ref[g], pb[:, cs])
        acc_a[pl.ds(row0, SUB_PER_PAGE), cs] = ab[0:SUB_PER_PAGE]
        acc_b[pl.ds(row0, SUB_PER_PAGE), cs] = ab[SUB_PER_PAGE:2 * SUB_PER_PAGE]


def _cmp_finish(acc_a, acc_b, lin_ref, out_ref, ncp):
    pooled = acc_a[0:ncp, :] + acc_b[1:ncp + 1, :]
    for g in range(N_KV):
        cs = slice(g * HEAD_DIM, (g + 1) * HEAD_DIM)
        out_ref[0, :, cs] = _dot(pooled[:, cs].astype(BF16), lin_ref[g])


def _compress_rows_body(k_ref, v_ref, w2k_ref, w2v_ref, lk_ref, lv_ref, ok_ref, ov_ref,
                        aak, abk, aav, abv, *, pages, ncp):
    s = pl.program_id(1)

    @pl.when(s == 0)
    def _():
        tail = jnp.zeros((SUBLANES, KV_W), F32)
        abk[ncp:ncp + SUBLANES, :] = tail
        abv[ncp:ncp + SUBLANES, :] = tail

    for i in range(pages):
        row0 = pl.multiple_of((s * pages + i) * SUB_PER_PAGE, SUBLANES)
        rs = slice(i * PAGE_SIZE, (i + 1) * PAGE_SIZE)
        _cmp_accumulate(k_ref[0, rs, :], w2k_ref, aak, abk, row0)
        _cmp_accumulate(v_ref[0, rs, :], w2v_ref, aav, abv, row0)

    @pl.when(s == pl.num_programs(1) - 1)
    def _():
        _cmp_finish(aak, abk, lk_ref, ok_ref, ncp)
        _cmp_finish(aav, abv, lv_ref, ov_ref, ncp)


def _compress_rows(kv, w2k, w2v, lk, lv, *, b, t):
    pages = min(PAGES_PER_STEP, t // PAGE_SIZE)
    rows = pages * PAGE_SIZE
    ns = t // rows
    ncp = t // S_CMP
    wspec = lambda shp: pl.BlockSpec(shp, lambda i, s: (0, 0, 0))
    acc = lambda: pltpu.VMEM((ncp + SUBLANES, KV_W), F32)
    return pl.pallas_call(
        functools.partial(_compress_rows_body, pages=pages, ncp=ncp),
        grid=(b, ns),
        in_specs=[
            pl.BlockSpec((1, rows, KV_W), lambda i, s: (0, i * ns + s, 0)),
            pl.BlockSpec((1, rows, KV_W), lambda i, s: (1, i * ns + s, 0)),
            wspec(w2k.shape), wspec(w2v.shape), wspec(lk.shape), wspec(lv.shape),
        ],
        out_specs=[
            pl.BlockSpec((1, ncp, KV_W), lambda i, s: (i, 0, 0)),
            pl.BlockSpec((1, ncp, KV_W), lambda i, s: (i, 0, 0)),
        ],
        out_shape=[jax.ShapeDtypeStruct((b, ncp, KV_W), F32)] * 2,
        scratch_shapes=[acc(), acc(), acc(), acc()],
        compiler_params=_cparams(("parallel", "arbitrary")),
        name="compress_rows",
    )(kv, kv, w2k, w2v, lk, lv)


def _compress_paged_body(pt_ref, *refs, pages, n_pool_steps, ncp):
    pk = refs[0:pages]
    pv = refs[pages:2 * pages]
    nk_ref, nv_ref, w2k_ref, w2v_ref, lk_ref, lv_ref, ok_ref, ov_ref, aak, abk, aav, abv = refs[2 * pages:]
    s = pl.program_id(1)

    @pl.when(s < n_pool_steps)
    def _():
        for i in range(pages):
            row0 = pl.multiple_of((s * pages + i) * SUB_PER_PAGE, SUBLANES)
            _cmp_accumulate(pk[i][0], w2k_ref, aak, abk, row0)
            _cmp_accumulate(pv[i][0], w2v_ref, aav, abv, row0)

    @pl.when(s == n_pool_steps)
    def _():
        row0 = n_pool_steps * pages * SUB_PER_PAGE
        _cmp_accumulate(nk_ref[0], w2k_ref, aak, abk, row0)
        _cmp_accumulate(nv_ref[0], w2v_ref, aav, abv, row0)
        _cmp_finish(aak, abk, lk_ref, ok_ref, ncp)
        _cmp_finish(aav, abv, lv_ref, ov_ref, ncp)


def _page_spec(i, pages, n_pool_steps):
    def index(bi, s, pt):
        return (pt[bi, jnp.minimum(s, n_pool_steps - 1) * pages + i], 0, 0)
    return pl.BlockSpec((1, PAGE_SIZE, KV_W), index)


def _compress_paged(page_table, pool_k, pool_v, new_k, new_v, w2k, w2v, lk, lv):
    b, n_pages = page_table.shape
    pages = PAGES_PER_STEP
    n_pool_steps = n_pages // pages
    ncp = n_pages * SUB_PER_PAGE
    wspec = lambda shp: pl.BlockSpec(shp, lambda bi, s, pt: (0, 0, 0))
    newspec = lambda: pl.BlockSpec((1, PAGE_SIZE, KV_W), lambda bi, s, pt: (bi, 0, 0))
    acc = lambda: pltpu.VMEM((ncp + SUB_PER_PAGE, KV_W), F32)
    grid_spec = pltpu.PrefetchScalarGridSpec(
        num_scalar_prefetch=1,
        grid=(b, n_pool_steps + 1),
        in_specs=(
            [_page_spec(i, pages, n_pool_steps) for i in range(pages)] * 2
            + [newspec(), newspec(), wspec(w2k.shape), wspec(w2v.shape), wspec(lk.shape), wspec(lv.shape)]
        ),
        out_specs=[
            pl.BlockSpec((1, ncp, KV_W), lambda bi, s, pt: (bi, 0, 0)),
            pl.BlockSpec((1, ncp, KV_W), lambda bi, s, pt: (bi, 0, 0)),
        ],
        scratch_shapes=[acc(), acc(), acc(), acc()],
    )
    return pl.pallas_call(
        functools.partial(_compress_paged_body, pages=pages, n_pool_steps=n_pool_steps, ncp=ncp),
        grid_spec=grid_spec,
        out_shape=[jax.ShapeDtypeStruct((b, ncp, KV_W), F32)] * 2,
        compiler_params=_cparams(("parallel", "arbitrary")),
        name="compress_paged",
    )(page_table, *([pool_k] * pages), *([pool_v] * pages), new_k, new_v, w2k, w2v, lk, lv)


def _nsa_prompt_body(q_ref, z_ref, gt_ref, ck_ref, cv_ref, ks_ref, vs_ref, kw_ref, vw_ref, a_ref,
                     ksb, vsb, kwb, vwb, m_s, l_s, acc_s, *, tq, t, kc, n_cmp):
    qi = pl.program_id(2)

    @pl.when(qi == 0)
    def _():
        ksb[...] = ks_ref[0].astype(BF16)
        vsb[...] = vs_ref[0].astype(BF16)
        kwb[...] = kw_ref[0].astype(BF16)
        vwb[...] = vw_ref[0].astype(BF16)

    n_blk = -(-t // L_SEL)
    ncl = ck_ref.shape[1]
    q0 = qi * tq
    pos = q0 + _iota((tq, 1), 0)
    scale = HEAD_DIM ** -0.5
    hs = [slice(h * HEAD_DIM, (h + 1) * HEAD_DIM) for h in range(HPG)]
    qb = [q_ref[:, hs[h]].astype(BF16) for h in range(HPG)]

    ckb = ck_ref[0].astype(BF16)
    cvb = cv_ref[0].astype(BF16)
    ci = _iota((1, ncl), 1)
    cmask = (ci * S_CMP + (L_CMP - 1) <= pos) & (ci < n_cmp)
    imp_c = jnp.zeros((tq, ncl), F32)
    o_c = []
    for h in range(HPG):
        p = _softmax_masked(_dot_nt(qb[h], ckb) * scale, cmask)
        imp_c = imp_c + p
        o_c.append(_dot(p.astype(BF16), cvb))

    imp = _dot_f32_by_01(imp_c, _block_importance_matrix(ncl, LANES, n_blk))
    selb = _select_blocks(imp, pos, n_blk).astype(BF16)

    for h in range(HPG):
        m_s[h] = jnp.full((tq, 1), NEG, F32)
        l_s[h] = jnp.zeros((tq, 1), F32)
        acc_s[h] = jnp.zeros((tq, HEAD_DIM), F32)

    def chunk(ki, carry):
        k0 = pl.multiple_of(ki * kc, kc)
        kb = ksb[pl.ds(k0, kc), :]
        vb = vsb[pl.ds(k0, kc), :]
        kpos = k0 + _iota((1, kc), 1)
        blk = lax.shift_right_logical(k0 + _iota((LANES, kc), 1), int(math.log2(L_SEL)))
        expand = jnp.where(blk == _iota((LANES, kc), 0), 1.0, 0.0).astype(BF16)
        mask = (_dot(selb, expand) > 0.5) & (kpos <= pos)
        for h in range(HPG):
            _online_update(m_s.at[h], l_s.at[h], acc_s.at[h], _dot_nt(qb[h], kb) * scale, mask,
                           lambda p: _dot(p.astype(BF16), vb))
        return carry

    n_chunks = (q0 + tq + kc - 1) // kc
    lax.fori_loop(0, n_chunks, chunk, 0)

    ws = min(WINDOW + tq, t)
    w0 = pl.multiple_of(jnp.minimum(jnp.maximum(q0 - WINDOW, 0), t - ws), tq)
    kwv = kwb[pl.ds(w0, ws), :]
    vwv = vwb[pl.ds(w0, ws), :]
    wpos = w0 + _iota((1, ws), 1)
    wmask = (wpos <= pos) & (wpos > pos - WINDOW)

    gates = jax.nn.sigmoid(gt_ref[...])
    for h in range(HPG):
        o_s = _online_finish(l_s.at[h], acc_s.at[h])
        pw = _softmax_masked(_dot_nt(qb[h], kwv) * scale, wmask)
        o_w = _dot(pw.astype(BF16), vwv)
        o = (gates[:, h:h + 1] * o_c[h] + gates[:, HPG + h:HPG + h + 1] * o_s
             + gates[:, 2 * HPG + h:2 * HPG + h + 1] * o_w)
        a_ref[:, hs[h]] = (o * _silu(z_ref[:, hs[h]])).astype(BF16)


def _nsa_prompt_attend(qz, gates, ck, cv, kv, *, b, t, tq, kc):
    gq = HPG * HEAD_DIM
    nq = t // tq
    n_cmp = t // S_CMP - (L_CMP // S_CMP) + 1
    ncl = ck.shape[1]
    rowq = lambda i, g, q: i * nq + q
    kvspec = lambda idx: pl.BlockSpec((1, t, HEAD_DIM), lambda i, g, q: (idx, i, g))
    kvb = lambda: pltpu.VMEM((t, HEAD_DIM), BF16)
    return pl.pallas_call(
        functools.partial(_nsa_prompt_body, tq=tq, t=t, kc=kc, n_cmp=n_cmp),
        grid=(b, N_KV, nq),
        in_specs=[
            pl.BlockSpec((tq, gq), lambda i, g, q: (rowq(i, g, q), g)),
            pl.BlockSpec((tq, gq), lambda i, g, q: (rowq(i, g, q), N_KV + g)),
            pl.BlockSpec((tq, LANES), lambda i, g, q: (rowq(i, g, q), g)),
            pl.BlockSpec((1, ncl, HEAD_DIM), lambda i, g, q: (i, 0, g)),
            pl.BlockSpec((1, ncl, HEAD_DIM), lambda i, g, q: (i, 0, g)),
            kvspec(2), kvspec(3), kvspec(4), kvspec(5),
        ],
        out_specs=pl.BlockSpec((tq, gq), lambda i, g, q: (rowq(i, g, q), g)),
        out_shape=jax.ShapeDtypeStruct((b * t, N_KV * gq), BF16),
        scratch_shapes=[
            kvb(), kvb(), kvb(), kvb(),
            pltpu.VMEM((HPG, tq, 1), F32),
            pltpu.VMEM((HPG, tq, 1), F32),
            pltpu.VMEM((HPG, tq, HEAD_DIM), F32),
        ],
        compiler_params=_cparams(("parallel", "parallel", "arbitrary")),
        name="nsa_prompt_attend",
    )(qz, qz, gates, ck, cv, kv, kv, kv, kv)


def _diag_blocks(x, rows_per_group):
    return jnp.concatenate(
        [x[g * rows_per_group:(g + 1) * rows_per_group, g * HEAD_DIM:(g + 1) * HEAD_DIM]
         for g in range(N_KV)], axis=0)


def _nsa_sample_body(pt_ref, *refs, pages, n_pool_steps, past, n_new, wb):
    sk = refs[0:pages]
    sv = refs[pages:2 * pages]
    (nks4, nvs4, nkw4, nvw4, ck_ref, cv_ref, wk_ref, wv_ref, q_ref, z_ref, gt_ref,
     a_ref, wko_ref, wvo_ref, q2, selr, m_s, l_s, acc, oc, ow) = refs[2 * pages:]
    nks_ref, nvs_ref, nkw_ref, nvw_ref = nks4.at[0], nvs4.at[0], nkw4.at[0], nvw4.at[0]
    s = pl.program_id(1)
    rows = N_HEADS * n_new
    rpg = HPG * n_new
    scale = HEAD_DIM ** -0.5
    n_blk = past // L_SEL + -(-n_new // L_SEL)
    n_cmp = ck_ref.shape[1]
    sel_lanes = selr.shape[1]
    pos = past + (_iota((rows, 1), 0) % n_new)

    @pl.when(s == 0)
    def _():
        q2[...] = jnp.zeros(q2.shape, F32)
        for hh in range(N_HEADS):
            g = hh // HPG
            q2[hh * n_new:(hh + 1) * n_new, g * HEAD_DIM:(g + 1) * HEAD_DIM] = (
                q_ref[:, hh * HEAD_DIM:(hh + 1) * HEAD_DIM])
        qb = q2[...].astype(BF16)

        ci = _iota((1, n_cmp), 1)
        cmask = ci * S_CMP + (L_CMP - 1) <= pos
        p_c = _softmax_masked(_dot_nt(qb, ck_ref[0].astype(BF16)) * scale, cmask)
        oc[...] = _diag_blocks(_dot(p_c.astype(BF16), cv_ref[0].astype(BF16)), rpg)

        imp_c = jnp.concatenate(
            [sum(p_c[(g * HPG + h) * n_new:(g * HPG + h + 1) * n_new, :] for h in range(HPG))
             for g in range(N_KV)], axis=0)
        imp = _dot_f32_by_01(imp_c, _block_importance_matrix(n_cmp, sel_lanes, n_blk))
        pos_gt = past + (_iota((N_KV * n_new, 1), 0) % n_new)
        sel = _select_blocks(imp, pos_gt, n_blk)
        selr[...] = jnp.concatenate(
            [sel[(hh // HPG) * n_new:(hh // HPG + 1) * n_new, :] for hh in range(N_HEADS)], axis=0)

        wpos = (past - wb) + _iota((1, wb), 1)
        npos = past + _iota((1, PAGE_SIZE), 1)
        mask = jnp.concatenate(
            [(wpos <= pos) & (wpos > pos - WINDOW),
             (npos <= pos) & (_iota((1, PAGE_SIZE), 1) < n_new)], axis=1)
        sc = jnp.concatenate(
            [_dot_nt(qb, wk_ref[0].astype(BF16)), _dot_nt(qb, nkw_ref[0].astype(BF16))], axis=1) * scale
        p_w = _softmax_masked(sc, mask)
        ow[...] = _diag_blocks(
            _dot(p_w[:, 0:wb].astype(BF16), wv_ref[0].astype(BF16))
            + _dot(p_w[:, wb:].astype(BF16), nvw_ref[0].astype(BF16)), rpg)

        wko_ref[0, 0:wb - n_new, :] = wk_ref[0, n_new:wb, :]
        wko_ref[0, wb - n_new:wb, :] = nkw_ref[0, 0:n_new, :]
        wvo_ref[0, 0:wb - n_new, :] = wv_ref[0, n_new:wb, :]
        wvo_ref[0, wb - n_new:wb, :] = nvw_ref[0, 0:n_new, :]

        m_s[...] = jnp.full(m_s.shape, NEG, F32)
        l_s[...] = jnp.zeros(l_s.shape, F32)
        acc[...] = jnp.zeros(acc.shape, F32)

    @pl.when(s < n_pool_steps)
    def _():
        qb = q2[...].astype(BF16)
        nk = pages * PAGE_SIZE
        sc = jnp.concatenate([_dot_nt(qb, sk[i][0].astype(BF16)) for i in range(pages)], axis=1) * scale
        blk = s * (nk // L_SEL) + lax.shift_right_logical(_iota((sel_lanes, nk), 1), int(math.log2(L_SEL)))
        expand = jnp.where(blk == _iota((sel_lanes, nk), 0), 1.0, 0.0).astype(BF16)
        mask = _dot(selr[...].astype(BF16), expand) > 0.5

        def pv(p):
            return sum(_dot(p[:, i * PAGE_SIZE:(i + 1) * PAGE_SIZE].astype(BF16), sv[i][0].astype(BF16))
                       for i in range(pages))

        _online_update(m_s, l_s, acc, sc, mask, pv)

    @pl.when(s == n_pool_steps)
    def _():
        qb = q2[...].astype(BF16)
        new_blk = past // L_SEL
        npos = past + _iota((1, PAGE_SIZE), 1)
        mask = (selr[:, new_blk:new_blk + 1] > 0.5) & (npos <= pos) & (_iota((1, PAGE_SIZE), 1) < n_new)
        _online_update(m_s, l_s, acc, _dot_nt(qb, nks_ref[0].astype(BF16)) * scale, mask,
                       lambda p: _dot(p.astype(BF16), nvs_ref[0].astype(BF16)))
        o_s = _diag_blocks(_online_finish(l_s, acc), rpg)

        gates = jax.nn.sigmoid(gt_ref[...])

        def gate_col(br):
            return jnp.concatenate(
                [gates[:, (hh // HPG) * LANES + br * HPG + hh % HPG:(hh // HPG) * LANES + br * HPG + hh % HPG + 1]
                 for hh in range(N_HEADS)], axis=0)

        zr = jnp.concatenate([z_ref[:, hh * HEAD_DIM:(hh + 1) * HEAD_DIM] for hh in range(N_HEADS)], axis=0)
        o = gate_col(0) * oc[...] + gate_col(1) * o_s + gate_col(2) * ow[...]
        res = o * _silu(zr)
        for hh in range(N_HEADS):
            a_ref[:, hh * HEAD_DIM:(hh + 1) * HEAD_DIM] = res[hh * n_new:(hh + 1) * n_new, :]


def _nsa_sample_attend(page_table, pool_k, pool_v, new_pages, ck, cv, win_k, win_v, qz, gates, *, n_new):
    b, n_pages = page_table.shape
    pages = PAGES_PER_STEP
    n_pool_steps = n_pages // pages
    past = n_pages * PAGE_SIZE
    wb = win_k.shape[1]
    n_cmp = ck.shape[1]
    rows = N_HEADS * n_new
    qw = N_HEADS * HEAD_DIM
    n_blk = past // L_SEL + -(-n_new // L_SEL)
    sel_lanes = -(-n_blk // LANES) * LANES
    newspec = lambda idx: pl.BlockSpec((1, 1, PAGE_SIZE, KV_W), lambda bi, s, pt: (idx, bi, 0, 0))
    full = lambda n: pl.BlockSpec((1, n, KV_W), lambda bi, s, pt: (bi, 0, 0))
    grid_spec = pltpu.PrefetchScalarGridSpec(
        num_scalar_prefetch=1,
        grid=(b, n_pool_steps + 1),
        in_specs=(
            [_page_spec(i, pages, n_pool_steps) for i in range(pages)] * 2
            + [newspec(2), newspec(3), newspec(4), newspec(5), full(n_cmp), full(n_cmp), full(wb), full(wb),
               pl.BlockSpec((n_new, qw), lambda bi, s, pt: (bi, 0)),
               pl.BlockSpec((n_new, qw), lambda bi, s, pt: (bi, 1)),
               pl.BlockSpec((n_new, N_KV * LANES), lambda bi, s, pt: (bi, 0))]
        ),
        out_specs=[
            pl.BlockSpec((n_new, qw), lambda bi, s, pt: (bi, 0)),
            full(wb), full(wb),
        ],
        scratch_shapes=[
            pltpu.VMEM((rows, KV_W), F32),
            pltpu.VMEM((rows, sel_lanes), F32),
            pltpu.VMEM((rows, 1), F32),
            pltpu.VMEM((rows, 1), F32),
            pltpu.VMEM((rows, KV_W), F32),
            pltpu.VMEM((rows, HEAD_DIM), F32),
            pltpu.VMEM((rows, HEAD_DIM), F32),
        ],
    )
    body = functools.partial(_nsa_sample_body, pages=pages, n_pool_steps=n_pool_steps, past=past,
                             n_new=n_new, wb=wb)

    return pl.pallas_call(
        body,
        grid_spec=grid_spec,
        out_shape=[
            jax.ShapeDtypeStruct((b * n_new, qw), F32),
            jax.ShapeDtypeStruct((b, wb, KV_W), F32),
            jax.ShapeDtypeStruct((b, wb, KV_W), F32),
        ],
        compiler_params=_cparams(("parallel", "arbitrary")),
        name="nsa_sample_attend",
    )(page_table, *([pool_k] * pages), *([pool_v] * pages), new_pages, new_pages, new_pages, new_pages,
      ck, cv, win_k, win_v, qz, qz, gates)


def _rope_tables(pos, dk):
    half = dk // 2
    inv = 1.0 / (ROPE_BASE ** jnp.linspace(0.0, 1.0, half))
    ang = pos.astype(F32)[:, None] * inv[None, :]
    cos = jnp.repeat(jnp.cos(ang), 2, axis=1)
    sin = jnp.stack([-jnp.sin(ang), jnp.sin(ang)], axis=-1).reshape(pos.shape[0], dk)
    return cos, sin


def _pooling_matrix(w_pos):
    eye = jnp.eye(SUB_PER_PAGE, dtype=F32)
    first = jnp.einsum("mn,lg->gmnl", eye, w_pos[:S_CMP]).reshape(N_KV, SUB_PER_PAGE, PAGE_SIZE)
    second = jnp.einsum("mn,lg->gmnl", eye, w_pos[S_CMP:]).reshape(N_KV, SUB_PER_PAGE, PAGE_SIZE)
    return jnp.concatenate([first, second], axis=1).astype(BF16)


def _split_nsa_w_in(w):
    qw = N_HEADS * HEAD_DIM
    ng = 3 * N_HEADS
    d = w.shape[0]
    w_qz = w[:, :2 * qw]
    w_g = w[:, 2 * qw:2 * qw + ng].reshape(d, 3, N_KV, HPG).transpose(0, 2, 1, 3).reshape(d, N_KV, 3 * HPG)
    w_g = jnp.pad(w_g, ((0, 0), (0, 0), (0, LANES - 3 * HPG))).reshape(d, N_KV * LANES)
    w_kv = w[:, 2 * qw + ng:]
    return w_qz.astype(BF16), w_g.astype(BF16), w_kv.astype(BF16)


def _tile_rows(m, pref):
    return pref if m % pref == 0 else m


def _run_group(x, *, pos0, pool_buf, ret_s0, conv_buf, lru_h0, nsa_cache, params):
    nb, t, d = x.shape
    m = nb * t
    tm = _tile_rows(m, 512)
    tn = 512
    fused_layout = t % tm == 0
    nt = t // tm if fused_layout else 1
    bm_in = lambda i, j: (i, 0)
    bm_io = lambda i: (i, 0)
    tb_in = lambda i, j: (i % nt, i // nt)
    tb_io = lambda i: (i % nt, i // nt)
    pos = pos0 + jnp.arange(t, dtype=jnp.int32)
    vec = lambda v: v.reshape(1, -1)

    depth = params["norm_pre"].shape[0]
    cur = x.reshape(m, d)
    cur_is_tm = False
    states = {k: [] for k in ("pool", "ret", "conv", "lru", "nsa")}

    def to_bm_view(a):
        return (a.reshape(t, nb * d), tb_in, tb_io) if fused_layout else (a, bm_in, bm_io)

    for i in range(depth):
        kind, j = i % 4, i // 4
        g_pre = vec(params["norm_pre"][i])
        g_post = vec(params["norm_post"][i])
        if kind == 0:
            w_in = params["pool_w_in"][j].astype(BF16)
            e = w_in.shape[1] // 2
            uz = _norm_matmul(cur, bm_in, g_pre, w_in, m=m, tm=tm, tn=tn,
                              out_shape=(m, 2 * e), out_block=(tm, tn), out_map=lambda a, b_: (a, b_))
            buf16 = jnp.pad(pool_buf[j], ((0, 0), (POOL_HALO - pool_buf[j].shape[1], 0), (0, 0)))
            a, st = _pool_mixer(uz, buf16, params["pool_w_grp"][j].astype(BF16), vec(params["pool_b_grp"][j]),
                                vec(params["pool_scale"][j]), b=nb, t=t, tt=_tile_rows(t, 256), pos0=pos0)
            states["pool"].append(st[:, 1:, :])
            cur = _out_proj(a, params["pool_w_out"][j].astype(BF16), cur, bm_io, g_post, m=m, tm=tm,
                            out_shape=(m, d), out_map=bm_io)
        elif kind == 1:
            w_in = params["ret_w_in"][j].astype(BF16)
            dk = d // R_HEADS
            dv = 2 * dk
            proj = _norm_matmul(cur, bm_in, g_pre, w_in, m=m, tm=tm, tn=tn,
                                out_shape=(m, w_in.shape[1]), out_block=(tm, tn), out_map=lambda a, b_: (a, b_))
            cos_t, sin_t = _rope_tables(pos, dk)
            lg = jnp.log(1.0 - 2.0 ** (-5.0 - jnp.arange(R_HEADS, dtype=F32)))
            a, s_new = _retention_mixer(proj, cos_t, sin_t, lg, ret_s0[j], b=nb, t=t, dk=dk, dv=dv)
            states["ret"].append(s_new)
            w_out = params["ret_w_out"][j].astype(BF16)
            if fused_layout:
                cur = _out_proj(a, w_out, cur, bm_io, g_post, m=m, tm=tm,
                                out_shape=(t, nb * d), out_map=tb_io).reshape(m, d)
            else:
                cur = _out_proj(a, w_out, cur, bm_io, g_post, m=m, tm=tm, out_shape=(m, d), out_map=bm_io)
                cur = cur.reshape(nb, t, d).swapaxes(0, 1).reshape(m, d)
            cur_is_tm = True
        elif kind == 2:
            assert cur_is_tm
            w_in = params["lru_w_in"][j].astype(BF16)
            d_rnn = w_in.shape[1] // 2
            uz = _norm_matmul(cur, bm_in, g_pre, w_in, m=m, tm=tm, tn=tn,
                              out_shape=(m, 2 * d_rnn), out_block=(tm, tn), out_map=lambda a, b_: (a, b_))
            buf = conv_buf[j].swapaxes(0, 1).reshape((CONV_W - 1) * nb, d_rnn)
            tt = max(1, min(t, 512 // nb))
            a, cs_new, h_new = _lru_mixer(
                uz, buf, params["lru_conv_w"][j], vec(params["lru_conv_b"][j]),
                params["lru_w_a"][j].astype(BF16), vec(params["lru_b_a"][j]),
                params["lru_w_x"][j].astype(BF16), vec(params["lru_b_x"][j]),
                vec(params["lru_lam"][j]), lru_h0[j], nb=nb, t=t, tt=tt)
            states["conv"].append(cs_new.reshape(CONV_W - 1, nb, d_rnn).swapaxes(0, 1))
            states["lru"].append(h_new)
            cur = _out_proj(a, params["lru_w_out"][j].astype(BF16), cur, bm_io, g_post, m=m, tm=tm,
                            out_shape=(m, d), out_map=bm_io)
        else:
            if cur_is_tm and not fused_layout:
                cur = cur.reshape(t, nb, d).swapaxes(0, 1).reshape(m, d)
                cur_is_tm = False
            xv, x_in, x_io = to_bm_view(cur) if cur_is_tm else (cur, bm_in, bm_io)
            w_qz, w_g, w_kv = _split_nsa_w_in(params["nsa_w_in"][j])
            qz = _norm_matmul(xv, x_in, g_pre, w_qz, m=m, tm=tm, tn=tn,
                              out_shape=(m, w_qz.shape[1]), out_block=(tm, tn), out_map=lambda a, b_: (a, b_))
            gates = _norm_matmul(xv, x_in, g_pre, w_g, m=m, tm=tm, tn=w_g.shape[1],
                                 out_shape=(m, w_g.shape[1]), out_block=(tm, w_g.shape[1]),
                                 out_map=lambda a, b_: (a, b_))
            n_kv_t = w_kv.shape[1] // KV_W
            kv = _norm_matmul(xv, x_in, g_pre, w_kv, m=m, tm=tm, tn=KV_W,
                              out_shape=(n_kv_t, m, KV_W), out_block=(1, tm, KV_W), out_map=lambda a, b_: (b_, a, 0))
            w2k = _pooling_matrix(params["nsa_cmp_pos_k"][j])
            w2v = _pooling_matrix(params["nsa_cmp_pos_v"][j])
            lk = params["nsa_cmp_lin_k"][j].astype(BF16)
            lv = params["nsa_cmp_lin_v"][j].astype(BF16)
            kv5 = kv.reshape(n_kv_t, nb, t, N_KV, HEAD_DIM)
            if nsa_cache is None:
                ck, cv = _compress_rows(kv, w2k, w2v, lk, lv, b=nb, t=t)
                a = _nsa_prompt_attend(qz, gates, ck, cv, kv, b=nb, t=t, tq=256, kc=512)
                wb = min(WINDOW, t)
                new_state = (kv5[0], kv5[1], kv5[2], kv5[3], kv5[4][:, t - wb:], kv5[5][:, t - wb:])
            else:
                page_table, cmp_k, cmp_v, sel_k, sel_v, win_k, win_v = nsa_cache
                flat = lambda c: c[j].reshape(c.shape[1], c.shape[2], KV_W)
                new_pages = jnp.pad(kv.reshape(n_kv_t, nb, t, KV_W), ((0, 0), (0, 0), (0, PAGE_SIZE - t), (0, 0)))
                ck, cv = _compress_paged(page_table, flat(cmp_k), flat(cmp_v), new_pages[0], new_pages[1],
                                         w2k, w2v, lk, lv)
                a, wk_new, wv_new = _nsa_sample_attend(page_table, flat(sel_k), flat(sel_v), new_pages, ck, cv,
                                                       flat(win_k), flat(win_v), qz, gates, n_new=t)
                wshape = (nb, win_k.shape[2], N_KV, HEAD_DIM)
                new_state = (kv5[0], kv5[1], kv5[2], kv5[3], wk_new.reshape(wshape), wv_new.reshape(wshape))
            states["nsa"].append(new_state)
            cur = _out_proj(a, params["nsa_w_out"][j].astype(BF16), xv, x_io, g_post, m=m, tm=tm,
                            out_shape=(m, d), out_map=bm_io)
            cur_is_tm = False
    if cur_is_tm:
        cur = cur.reshape(t, nb, d).swapaxes(0, 1).reshape(m, d)
    return cur.reshape(nb, t, d), states


def kernel(x_prompt, x_sample, state_pool, state_ret, state_conv, state_lru, cache_cmp_k, cache_cmp_v, cache_sel_k, cache_sel_v, cache_win_k, cache_win_v, page_table, norm_pre, norm_post, pool_w_in, pool_w_grp, pool_b_grp, pool_scale, pool_w_out, ret_w_in, ret_w_out, lru_w_in, lru_conv_w, lru_conv_b, lru_w_a, lru_b_a, lru_w_x, lru_b_x, lru_lam, lru_w_out, nsa_w_in, nsa_cmp_pos_k, nsa_cmp_lin_k, nsa_cmp_pos_v, nsa_cmp_lin_v, nsa_w_out):
    params = dict(
        norm_pre=norm_pre, norm_post=norm_post, pool_w_in=pool_w_in, pool_w_grp=pool_w_grp, pool_b_grp=pool_b_grp,
        pool_scale=pool_scale, pool_w_out=pool_w_out, ret_w_in=ret_w_in, ret_w_out=ret_w_out, lru_w_in=lru_w_in,
        lru_conv_w=lru_conv_w, lru_conv_b=lru_conv_b, lru_w_a=lru_w_a, lru_b_a=lru_b_a, lru_w_x=lru_w_x,
        lru_b_x=lru_b_x, lru_lam=lru_lam, lru_w_out=lru_w_out, nsa_w_in=nsa_w_in, nsa_cmp_pos_k=nsa_cmp_pos_k,
        nsa_cmp_lin_k=nsa_cmp_lin_k, nsa_cmp_pos_v=nsa_cmp_pos_v, nsa_cmp_lin_v=nsa_cmp_lin_v, nsa_w_out=nsa_w_out)
    b = x_prompt.shape[0]
    past = page_table.shape[1] * PAGE_SIZE
    zeros_like_state = lambda s: jnp.zeros((s.shape[0], b) + s.shape[2:], F32)
    yp, sp = _run_group(
        x_prompt, pos0=0, pool_buf=zeros_like_state(state_pool), ret_s0=zeros_like_state(state_ret),
        conv_buf=zeros_like_state(state_conv), lru_h0=zeros_like_state(state_lru), nsa_cache=None, params=params)
    ys, ss = _run_group(
        x_sample, pos0=past, pool_buf=state_pool, ret_s0=state_ret, conv_buf=state_conv, lru_h0=state_lru,
        nsa_cache=(page_table, cache_cmp_k, cache_cmp_v, cache_sel_k, cache_sel_v, cache_win_k, cache_win_v),
        params=params)
    out = [yp, ys]
    for key in ("pool", "ret", "conv", "lru"):
        out += [jnp.stack(sp[key]), jnp.stack(ss[key])]
    for idx in range(6):
        out += [jnp.stack([e[idx] for e in sp["nsa"]]), jnp.stack([e[idx] for e in ss["nsa"]])]
    return tuple(out)
```

```python
import functools
import math

import jax
import jax.numpy as jnp
from jax import lax
from jax.experimental import pallas as pl
from jax.experimental.pallas import tpu as pltpu

F32 = jnp.float32
BF16 = jnp.bfloat16

EPS = 1e-6
NEG = -1e30
FORCE = 1e6

PAGE_SIZE = 128
POOL_WINDOWS = (2, 4, 8, 16)
R_HEADS = 4
R_CHUNK = 128
ROPE_BASE = 10000.0
LRU_BW = 256
CONV_W = 4
LRU_C = 8.0
N_HEADS = 16
N_KV = 4
HPG = N_HEADS // N_KV
HEAD_DIM = 128
L_CMP = 32
S_CMP = 16
L_SEL = 64
N_SEL = 16
WINDOW = 512
KV_W = N_KV * HEAD_DIM
SUB_PER_PAGE = PAGE_SIZE // S_CMP
PAGES_PER_STEP = 8

LANES = 128
SUBLANES = 8
VMEM_LIMIT = 56 * 1024 * 1024
IN_ROWS, IN_COLS = 1024, 1024
OUT_ROWS = 512


def _cparams(sem):
    return pltpu.CompilerParams(dimension_semantics=sem, vmem_limit_bytes=VMEM_LIMIT)


def _act_dtype(rows):
    return BF16 if rows % (2 * SUBLANES) == 0 else F32


def _dot(a, b):
    return jnp.dot(a, b, preferred_element_type=F32)


def _dot_nt(a, b):
    return lax.dot_general(a, b, (((1,), (1,)), ((), ())), preferred_element_type=F32)


def _dot_tn(a, b):
    return lax.dot_general(a, b, (((0,), (0,)), ((), ())), preferred_element_type=F32)


def _silu(x):
    return x * jax.nn.sigmoid(x)


def _iota(shape, dim):
    return lax.broadcasted_iota(jnp.int32, shape, dim)


def _norm_mm_body(x_ref, g_ref, w_ref, o_ref, h_ref):
    @pl.when(pl.program_id(1) == 0)
    def _():
        x = x_ref[...]
        ms = jnp.mean(x * x, axis=-1, keepdims=True)
        h_ref[...] = (x * lax.rsqrt(ms + EPS) * g_ref[...]).astype(BF16)

    o_ref[...] = _dot(h_ref[...], w_ref[...]).reshape(o_ref.shape)


def _norm_matmul(x, x_map, g, w, *, m, tm, tn, out_shape, out_block, out_map):
    d, n = w.shape
    return pl.pallas_call(
        _norm_mm_body,
        grid=(m // tm, n // tn),
        in_specs=[
            pl.BlockSpec((tm, d), x_map),
            pl.BlockSpec((1, d), lambda i, j: (0, 0)),
            pl.BlockSpec((d, tn), lambda i, j: (0, j)),
        ],
        out_specs=pl.BlockSpec(out_block, out_map),
        out_shape=jax.ShapeDtypeStruct(out_shape, F32),
        scratch_shapes=[pltpu.VMEM((tm, d), BF16)],
        compiler_params=_cparams(("parallel", "arbitrary")),
        name="norm_matmul",
    )(x, g, w)


def _norm_mm_kv_body(x_ref, g_ref, w_ref, o_ref, oi_ref, h_ref):
    @pl.when(pl.program_id(1) == 0)
    def _():
        x = x_ref[...]
        ms = jnp.mean(x * x, axis=-1, keepdims=True)
        h_ref[...] = (x * lax.rsqrt(ms + EPS) * g_ref[...]).astype(BF16)

    r = _dot(h_ref[...], w_ref[...])
    o_ref[0] = r
    for g in range(N_KV):
        oi_ref[0, pl.ds(g, r.shape[0], stride=N_KV), :] = r[:, g * HEAD_DIM:(g + 1) * HEAD_DIM]


def _norm_matmul_kv(x, x_map, g, w, *, m, tm):
    d, n = w.shape
    n_t = n // KV_W
    return pl.pallas_call(
        _norm_mm_kv_body,
        grid=(m // tm, n_t),
        in_specs=[
            pl.BlockSpec((tm, d), x_map),
            pl.BlockSpec((1, d), lambda i, j: (0, 0)),
            pl.BlockSpec((d, KV_W), lambda i, j: (0, j)),
        ],
        out_specs=[
            pl.BlockSpec((1, tm, KV_W), lambda i, j: (j, i, 0)),
            pl.BlockSpec((1, tm * N_KV, HEAD_DIM), lambda i, j: (j, i, 0)),
        ],
        out_shape=[
            jax.ShapeDtypeStruct((n_t, m, KV_W), F32),
            jax.ShapeDtypeStruct((n_t, m * N_KV, HEAD_DIM), F32),
        ],
        scratch_shapes=[pltpu.VMEM((tm, d), BF16)],
        compiler_params=_cparams(("parallel", "arbitrary")),
        name="norm_matmul_kv",
    )(x, g, w)


def _out_proj_body(a_ref, w_ref, x_ref, g_ref, o_ref):
    y = _dot(a_ref[...].astype(BF16), w_ref[...])
    ms = jnp.mean(y * y, axis=-1, keepdims=True)
    o_ref[...] = x_ref[...] + y * lax.rsqrt(ms + EPS) * g_ref[...]


def _out_proj(a, w, x, x_map, g, *, m, tm, out_shape, out_map):
    e, d = w.shape
    return pl.pallas_call(
        _out_proj_body,
        grid=(m // tm,),
        in_specs=[
            pl.BlockSpec((tm, e), lambda i: (i, 0)),
            pl.BlockSpec((e, d), lambda i: (0, 0)),
            pl.BlockSpec((tm, d), x_map),
            pl.BlockSpec((1, d), lambda i: (0, 0)),
        ],
        out_specs=pl.BlockSpec((tm, d), out_map),
        out_shape=jax.ShapeDtypeStruct(out_shape, F32),
        compiler_params=_cparams(("parallel",)),
        name="out_proj",
    )(a, w, x, g)


POOL_HALO = 16
POOL_PAD = 8


def _pool_body(u_ref, z_ref, buf_ref, wg_ref, bg_ref, sc_ref, a_ref, st_ref, ext, sa, sb,
               *, tt, pos0, gw):
    ti = pl.program_id(1)
    lo = POOL_PAD
    top = POOL_PAD + POOL_HALO
    rows = top + tt

    @pl.when(ti == 0)
    def _():
        ext[0:lo, :] = jnp.zeros((lo, ext.shape[1]), F32)
        ext[lo:top, :] = buf_ref[0]

    @pl.when(ti > 0)
    def _():
        ext[lo:top, :] = ext[lo + tt:top + tt, :]

    u = u_ref[...]
    ext[top:rows, :] = u
    sa[0:lo, :] = jnp.zeros((lo, gw), F32)
    sb[0:lo, :] = jnp.zeros((lo, gw), F32)
    pos = _iota((tt, 1), 0) + (pos0 + ti * tt)
    for g, w in enumerate(POOL_WINDOWS):
        cs = slice(g * gw, (g + 1) * gw)
        sa[lo:rows, :] = ext[lo:rows, cs] + ext[lo - 1:rows - 1, cs]
        cur, oth, sh = sa, sb, 2
        while sh < w:
            oth[lo:rows, :] = cur[lo:rows, :] + cur[lo - sh:rows - sh, :]
            cur, oth, sh = oth, cur, sh * 2
        cnt = jnp.minimum(pos + 1, w).astype(F32)
        mean = cur[top:rows, :] / cnt
        mixed = _dot((mean - u[:, cs]).astype(BF16), wg_ref[g]) + bg_ref[:, cs]
        a_ref[:, cs] = (mixed * sc_ref[:, cs] * _silu(z_ref[:, cs])).astype(a_ref.dtype)
    st_ref[0] = ext[lo + tt:top + tt, :]


def _pool_mixer(uz, buf16, wg, bg, sc, *, b, t, tt, pos0):
    e = buf16.shape[-1]
    gw = e // len(POOL_WINDOWS)
    nt = t // tt
    rows = POOL_PAD + POOL_HALO + tt
    return pl.pallas_call(
        functools.partial(_pool_body, tt=tt, pos0=pos0, gw=gw),
        grid=(b, nt),
        in_specs=[
            pl.BlockSpec((tt, e), lambda i, j: (i * nt + j, 0)),
            pl.BlockSpec((tt, e), lambda i, j: (i * nt + j, 1)),
            pl.BlockSpec((1, POOL_HALO, e), lambda i, j: (i, 0, 0)),
            pl.BlockSpec((len(POOL_WINDOWS), gw, gw), lambda i, j: (0, 0, 0)),
            pl.BlockSpec((1, e), lambda i, j: (0, 0)),
            pl.BlockSpec((1, e), lambda i, j: (0, 0)),
        ],
        out_specs=[
            pl.BlockSpec((tt, e), lambda i, j: (i * nt + j, 0)),
            pl.BlockSpec((1, POOL_HALO, e), lambda i, j: (i, 0, 0)),
        ],
        out_shape=[
            jax.ShapeDtypeStruct((b * t, e), _act_dtype(tt)),
            jax.ShapeDtypeStruct((b, POOL_HALO, e), F32),
        ],
        scratch_shapes=[
            pltpu.VMEM((rows, e), F32),
            pltpu.VMEM((rows, gw), F32),
            pltpu.VMEM((rows, gw), F32),
        ],
        compiler_params=_cparams(("parallel", "arbitrary")),
        name="pool_mixer",
    )(uz, uz, buf16, wg, bg, sc)


def _ret_body(lg_ref, q_ref, k_ref, v_ref, g_ref, cos_ref, sin_ref, s0_ref, a_ref, so_ref, s_scr,
              *, c_len, c_pad, dk):
    h = pl.program_id(1)
    c = pl.program_id(2)

    @pl.when(c == 0)
    def _():
        s_scr[...] = s0_ref[0, 0]

    lg = lg_ref[h]
    cosv = cos_ref[...]
    sinv = sin_ref[...]
    even = (_iota((c_len, dk), 1) & 1) == 0

    def rot(x):
        nxt = pltpu.roll(x, dk - 1, axis=1)
        prv = pltpu.roll(x, 1, axis=1)
        return x * cosv + jnp.where(even, nxt, prv) * sinv

    def pad(x):
        if c_pad == c_len:
            return x
        return jnp.concatenate([x, jnp.zeros((c_pad - c_len, x.shape[1]), x.dtype)], axis=0)

    q = pad(rot(q_ref[...]))
    k = pad(rot(k_ref[...]) * (dk ** -0.5))
    v = pad(v_ref[...])
    i_col = _iota((c_pad, 1), 0).astype(F32)
    j_row = _iota((1, c_pad), 1).astype(F32)
    diff = i_col - j_row
    intra = jnp.where(diff >= 0, jnp.exp(jnp.maximum(diff, 0.0) * lg), 0.0)
    q_dec = jnp.exp((i_col + 1.0) * lg)
    k_dec = jnp.exp((c_len - 1.0 - i_col) * lg)
    c_dec = jnp.exp(jnp.full((1, 1), float(c_len), F32) * lg)

    qb = q.astype(BF16)
    vb = v.astype(BF16)
    s = _dot_nt(qb, k.astype(BF16)) * intra
    s_old = s_scr[...]
    o = _dot(s.astype(BF16), vb) + _dot(qb, s_old.astype(BF16)) * q_dec
    s_scr[...] = s_old * c_dec + _dot_tn((k * k_dec).astype(BF16), vb)
    o = o[0:c_len]
    o = o * lax.rsqrt(jnp.mean(o * o, axis=-1, keepdims=True) + EPS)
    a_ref[...] = (o * _silu(g_ref[...])).astype(a_ref.dtype)

    @pl.when(c == pl.num_programs(2) - 1)
    def _():
        so_ref[0, 0] = s_scr[...]


def _retention_mixer(proj, cos_t, sin_t, lg, s0, *, b, t, dk, dv):
    c_len = R_CHUNK if t % R_CHUNK == 0 else t
    c_pad = max(c_len, R_CHUNK)
    nc = t // c_len
    nh = R_HEADS
    k_off = nh
    v_off = 2 * nh * dk // dv
    g_off = v_off + nh
    row = lambda i, h, c: i * nc + c
    return pl.pallas_call(
        functools.partial(_ret_body, c_len=c_len, c_pad=c_pad, dk=dk),
        grid=(b, nh, nc),
        in_specs=[
            pl.BlockSpec(memory_space=pltpu.SMEM),
            pl.BlockSpec((c_len, dk), lambda i, h, c: (row(i, h, c), h)),
            pl.BlockSpec((c_len, dk), lambda i, h, c: (row(i, h, c), k_off + h)),
            pl.BlockSpec((c_len, dv), lambda i, h, c: (row(i, h, c), v_off + h)),
            pl.BlockSpec((c_len, dv), lambda i, h, c: (row(i, h, c), g_off + h)),
            pl.BlockSpec((c_len, dk), lambda i, h, c: (c, 0)),
            pl.BlockSpec((c_len, dk), lambda i, h, c: (c, 0)),
            pl.BlockSpec((1, 1, dk, dv), lambda i, h, c: (i, h, 0, 0)),
        ],
        out_specs=[
            pl.BlockSpec((c_len, dv), lambda i, h, c: (row(i, h, c), h)),
            pl.BlockSpec((1, 1, dk, dv), lambda i, h, c: (i, h, 0, 0)),
        ],
        out_shape=[
            jax.ShapeDtypeStruct((b * t, nh * dv), _act_dtype(c_len)),
            jax.ShapeDtypeStruct((b, nh, dk, dv), F32),
        ],
        scratch_shapes=[pltpu.VMEM((dk, dv), F32)],
        compiler_params=_cparams(("parallel", "parallel", "arbitrary")),
        name="retention_mixer",
    )(lg, proj, proj, proj, proj, cos_t, sin_t, s0)


def _lru_body(u_ref, z_ref, buf_ref, cw_ref, cb_ref, wa_ref, ba_ref, wx_ref, bx_ref, lam_ref, h0_ref,
              a_ref, cso_ref, ho_ref, ext, a_s, b_s, h_s, hc, *, tt, nb):
    ti = pl.program_id(1)
    halo = (CONV_W - 1) * nb
    rows = tt * nb

    @pl.when(ti == 0)
    def _():
        ext[0:halo, :] = buf_ref[...]
        hc[...] = h0_ref[...]

    @pl.when(ti > 0)
    def _():
        ext[0:halo, :] = ext[rows:rows + halo, :]

    ext[halo:halo + rows, :] = u_ref[...]
    cw = cw_ref[...]
    c = cb_ref[...] + ext[0:rows, :] * cw[0:1, :]
    for kk in range(1, CONV_W):
        c = c + ext[kk * nb:kk * nb + rows, :] * cw[kk:kk + 1, :]
    cb16 = c.astype(BF16)
    r = jax.nn.sigmoid(_dot(cb16, wa_ref[0]) + ba_ref[...])
    ig = jax.nn.sigmoid(_dot(cb16, wx_ref[0]) + bx_ref[...])
    nl = -lam_ref[...]
    softplus = jnp.maximum(nl, 0.0) + jnp.log1p(jnp.exp(-jnp.abs(nl)))
    log_a = (-LRU_C) * r * softplus
    a_s[...] = jnp.exp(log_a)
    b_s[...] = jnp.sqrt(1.0 - jnp.exp(2.0 * log_a)) * (ig * c)

    def step(t, h):
        rs = pl.ds(pl.multiple_of(t * nb, SUBLANES), nb)
        h = a_s[rs, :] * h + b_s[rs, :]
        h_s[rs, :] = h
        return h

    hc[...] = lax.fori_loop(0, tt, step, hc[...], unroll=8)
    a_ref[...] = (h_s[...] * _silu(z_ref[...])).astype(BF16)
    cso_ref[...] = ext[rows:rows + halo, :]
    ho_ref[...] = hc[...]


def _lru_mixer(uz, buf, cw, cb, wa, ba, wx, bx, lam, h0, *, nb, t, tt):
    d_rnn = h0.shape[-1]
    nblk = d_rnn // LRU_BW
    nt = t // tt
    rows = tt * nb
    halo = (CONV_W - 1) * nb
    vec = lambda: pl.BlockSpec((1, LRU_BW), lambda j, i: (0, j))
    return pl.pallas_call(
        functools.partial(_lru_body, tt=tt, nb=nb),
        grid=(nblk, nt),
        in_specs=[
            pl.BlockSpec((rows, LRU_BW), lambda j, i: (i, j)),
            pl.BlockSpec((rows, LRU_BW), lambda j, i: (i, nblk + j)),
            pl.BlockSpec((halo, LRU_BW), lambda j, i: (0, j)),
            pl.BlockSpec((CONV_W, LRU_BW), lambda j, i: (0, j)),
            vec(),
            pl.BlockSpec((1, LRU_BW, LRU_BW), lambda j, i: (j, 0, 0)),
            vec(),
            pl.BlockSpec((1, LRU_BW, LRU_BW), lambda j, i: (j, 0, 0)),
            vec(),
            vec(),
            pl.BlockSpec((nb, LRU_BW), lambda j, i: (0, j)),
        ],
        out_specs=[
            pl.BlockSpec((rows, LRU_BW), lambda j, i: (i, j)),
            pl.BlockSpec((halo, LRU_BW), lambda j, i: (0, j)),
            pl.BlockSpec((nb, LRU_BW), lambda j, i: (0, j)),
        ],
        out_shape=[
            jax.ShapeDtypeStruct((t * nb, d_rnn), BF16),
            jax.ShapeDtypeStruct((halo, d_rnn), F32),
            jax.ShapeDtypeStruct((nb, d_rnn), F32),
        ],
        scratch_shapes=[
            pltpu.VMEM((halo + rows, LRU_BW), F32),
            pltpu.VMEM((rows, LRU_BW), F32),
            pltpu.VMEM((rows, LRU_BW), F32),
            pltpu.VMEM((rows, LRU_BW), F32),
            pltpu.VMEM((nb, LRU_BW), F32),
        ],
        compiler_params=_cparams(("parallel", "arbitrary")),
        name="lru_mixer",
    )(uz, uz, buf, cw, cb, wa, ba, wx, bx, lam, h0)


def _softmax_masked(s, mask):
    sm = jnp.where(mask, s, NEG)
    m = jnp.max(sm, axis=-1, keepdims=True)
    e = jnp.where(mask, jnp.exp(sm - m), 0.0)
    l = jnp.sum(e, axis=-1, keepdims=True)
    return e / jnp.where(l > 0.0, l, 1.0)


def _dot_f32_by_01(x, m01):
    hi = x.astype(BF16)
    r1 = x - hi.astype(F32)
    mid = r1.astype(BF16)
    lo = (r1 - mid.astype(F32)).astype(BF16)
    return _dot(hi, m01) + _dot(mid, m01) + _dot(lo, m01)


def _block_importance_matrix(n_rows, n_blk_lanes, n_blk, rows_per_cmp=1):
    r, s = L_SEL // S_CMP, L_CMP // S_CMP
    i = lax.shift_right_logical(_iota((n_rows, n_blk_lanes), 0), int(math.log2(rows_per_cmp)))
    j = _iota((n_rows, n_blk_lanes), 1)
    hit = (i >= r * j - (s - 1)) & (i <= r * j + (r - 1)) & (j < n_blk)
    return jnp.where(hit, 1.0, 0.0).astype(BF16)


def _select_blocks(imp, pos, n_blk):
    j = _iota(imp.shape, 1)
    cur = lax.shift_right_logical(pos, int(math.log2(L_SEL)))
    valid = j * L_SEL <= pos
    forced = (j == 0) | (j == cur) | (j == cur - 1)
    score = jnp.where(forced, FORCE, jnp.where(valid, imp, -1.0))
    cnt = jnp.zeros(imp.shape, F32)
    for i in range(n_blk):
        si = score[:, i:i + 1]
        ahead = (si > score) | ((si == score) & (j > i))
        cnt = cnt + jnp.where(ahead, 1.0, 0.0)
    keep = (cnt < float(min(N_SEL, n_blk))) & (score >= 0.0) & (j < n_blk)
    return jnp.where(keep, 1.0, 0.0)


def _online_update(m_ref, l_ref, acc_ref, s, mask, pv):
    sm = jnp.where(mask, s, NEG)
    m_old = m_ref[...]
    m_new = jnp.maximum(m_old, jnp.max(sm, axis=-1, keepdims=True))
    p = jnp.where(mask, jnp.exp(sm - m_new), 0.0)
    alpha = jnp.exp(m_old - m_new)
    l_ref[...] = alpha * l_ref[...] + jnp.sum(p, axis=-1, keepdims=True)
    acc_ref[...] = alpha * acc_ref[...] + pv(p)
    m_ref[...] = m_new


def _online_finish(l_ref, acc_ref):
    l = l_ref[...]
    return acc_ref[...] / jnp.where(l > 0.0, l, 1.0)


def _cmp_accumulate(page, w2_ref, acc_a, acc_b, row0):
    pb = page.astype(BF16)
    for g in range(N_KV):
        cs = slice(g * HEAD_DIM, (g + 1) * HEAD_DIM)
        ab = _dot(w2_ref[g], pb[:, cs])
        acc_a[pl.ds(row0, SUB_PER_PAGE), cs] = ab[0:SUB_PER_PAGE]
        acc_b[pl.ds(row0, SUB_PER_PAGE), cs] = ab[SUB_PER_PAGE:2 * SUB_PER_PAGE]


def _cmp_finish(acc_a, acc_b, lin_ref, out_ref, ncp):
    pooled = acc_a[0:ncp, :] + acc_b[1:ncp + 1, :]
    for g in range(N_KV):
        cs = slice(g * HEAD_DIM, (g + 1) * HEAD_DIM)
        out_ref[0, :, cs] = _dot(pooled[:, cs].astype(BF16), lin_ref[g])


def _compress_rows_body(k_ref, v_ref, w2k_ref, w2v_ref, lk_ref, lv_ref, ok_ref, ov_ref,
                        aak, abk, aav, abv, *, pages, ncp):
    s = pl.program_id(1)

    @pl.when(s == 0)
    def _():
        tail = jnp.zeros((SUBLANES, KV_W), F32)
        abk[ncp:ncp + SUBLANES, :] = tail
        abv[ncp:ncp + SUBLANES, :] = tail

    for i in range(pages):
        row0 = pl.multiple_of((s * pages + i) * SUB_PER_PAGE, SUBLANES)
        rs = slice(i * PAGE_SIZE, (i + 1) * PAGE_SIZE)
        _cmp_accumulate(k_ref[0, rs, :], w2k_ref, aak, abk, row0)
        _cmp_accumulate(v_ref[0, rs, :], w2v_ref, aav, abv, row0)

    @pl.when(s == pl.num_programs(1) - 1)
    def _():
        _cmp_finish(aak, abk, lk_ref, ok_ref, ncp)
        _cmp_finish(aav, abv, lv_ref, ov_ref, ncp)


def _compress_rows(kv, w2k, w2v, lk, lv, *, b, t):
    pages = min(PAGES_PER_STEP, t // PAGE_SIZE)
    rows = pages * PAGE_SIZE
    ns = t // rows
    ncp = t // S_CMP
    wspec = lambda shp: pl.BlockSpec(shp, lambda i, s: (0, 0, 0))
    acc = lambda: pltpu.VMEM((ncp + SUBLANES, KV_W), F32)
    return pl.pallas_call(
        functools.partial(_compress_rows_body, pages=pages, ncp=ncp),
        grid=(b, ns),
        in_specs=[
            pl.BlockSpec((1, rows, KV_W), lambda i, s: (0, i * ns + s, 0)),
            pl.BlockSpec((1, rows, KV_W), lambda i, s: (1, i * ns + s, 0)),
            wspec(w2k.shape), wspec(w2v.shape), wspec(lk.shape), wspec(lv.shape),
        ],
        out_specs=[
            pl.BlockSpec((1, ncp, KV_W), lambda i, s: (i, 0, 0)),
            pl.BlockSpec((1, ncp, KV_W), lambda i, s: (i, 0, 0)),
        ],
        out_shape=[jax.ShapeDtypeStruct((b, ncp, KV_W), F32)] * 2,
        scratch_shapes=[acc(), acc(), acc(), acc()],
        compiler_params=_cparams(("parallel", "arbitrary")),
        name="compress_rows",
    )(kv, kv, w2k, w2v, lk, lv)


PAGE_ROWS = PAGE_SIZE * N_KV
SUB_ROWS = SUB_PER_PAGE * N_KV


def _cmp_accumulate_rg(page, w2_ref, acc_a, acc_b, row0):
    ab = _dot(w2_ref[...], page.astype(BF16))
    acc_a[pl.ds(row0, SUB_ROWS), :] = ab[0:SUB_ROWS]
    acc_b[pl.ds(row0, SUB_ROWS), :] = ab[SUB_ROWS:2 * SUB_ROWS]


def _cmp_finish_rg(acc_a, acc_b, lin_ref, out_ref, nrows):
    pooled = acc_a[0:nrows, :] + acc_b[N_KV:nrows + N_KV, :]
    res = _dot(pooled.astype(BF16), lin_ref[...])
    grp = _iota((nrows, 1), 0) & (N_KV - 1)
    out = jnp.zeros((nrows, HEAD_DIM), F32)
    for g in range(N_KV):
        out = out + jnp.where(grp == g, res[:, g * HEAD_DIM:(g + 1) * HEAD_DIM], 0.0)
    out_ref[0] = out


def _compress_paged_body(pt_ref, *refs, pages, n_pool_steps, nrows):
    pk = refs[0:pages]
    pv = refs[pages:2 * pages]
    nk_ref, nv_ref, w2k_ref, w2v_ref, lk_ref, lv_ref, ok_ref, ov_ref, aak, abk, aav, abv = refs[2 * pages:]
    s = pl.program_id(1)

    @pl.when(s < n_pool_steps)
    def _():
        for i in range(pages):
            row0 = pl.multiple_of((s * pages + i) * SUB_ROWS, SUB_ROWS)
            _cmp_accumulate_rg(pk[i][...], w2k_ref, aak, abk, row0)
            _cmp_accumulate_rg(pv[i][...], w2v_ref, aav, abv, row0)

    @pl.when(s == n_pool_steps)
    def _():
        row0 = n_pool_steps * pages * SUB_ROWS
        _cmp_accumulate_rg(nk_ref[0, 0], w2k_ref, aak, abk, row0)
        _cmp_accumulate_rg(nv_ref[0, 0], w2v_ref, aav, abv, row0)
        _cmp_finish_rg(aak, abk, lk_ref, ok_ref, nrows)
        _cmp_finish_rg(aav, abv, lv_ref, ov_ref, nrows)


def _page_spec(i, pages, n_pool_steps):
    def index(bi, s, pt):
        return (pt[bi, jnp.minimum(s, n_pool_steps - 1) * pages + i], 0)
    return pl.BlockSpec((PAGE_ROWS, HEAD_DIM), index)


def _compress_paged(page_table, pool_k, pool_v, new_pages, w2k, w2v, lk, lv):
    b, n_pages = page_table.shape
    pages = PAGES_PER_STEP
    n_pool_steps = n_pages // pages
    nrows = n_pages * SUB_ROWS
    wspec = lambda shp: pl.BlockSpec(shp, lambda bi, s, pt: (0, 0))
    newspec = lambda idx: pl.BlockSpec((1, 1, PAGE_ROWS, HEAD_DIM), lambda bi, s, pt: (idx, bi, 0, 0))
    acc = lambda: pltpu.VMEM((nrows + SUB_ROWS, HEAD_DIM), F32)
    grid_spec = pltpu.PrefetchScalarGridSpec(
        num_scalar_prefetch=1,
        grid=(b, n_pool_steps + 1),
        in_specs=(
            [_page_spec(i, pages, n_pool_steps) for i in range(pages)] * 2
            + [newspec(0), newspec(1), wspec(w2k.shape), wspec(w2v.shape), wspec(lk.shape), wspec(lv.shape)]
        ),
        out_specs=[
            pl.BlockSpec((1, nrows, HEAD_DIM), lambda bi, s, pt: (bi, 0, 0)),
            pl.BlockSpec((1, nrows, HEAD_DIM), lambda bi, s, pt: (bi, 0, 0)),
        ],
        scratch_shapes=[acc(), acc(), acc(), acc()],
    )
    return pl.pallas_call(
        functools.partial(_compress_paged_body, pages=pages, n_pool_steps=n_pool_steps, nrows=nrows),
        grid_spec=grid_spec,
        out_shape=[jax.ShapeDtypeStruct((b, nrows, HEAD_DIM), F32)] * 2,
        compiler_params=_cparams(("parallel", "arbitrary")),
        name="compress_paged",
    )(page_table, *([pool_k] * pages), *([pool_v] * pages), new_pages, new_pages, w2k, w2v, lk, lv)


def _nsa_prompt_body(q_ref, z_ref, gt_ref, ck_ref, cv_ref, ks_ref, vs_ref, kw_ref, vw_ref, a_ref,
                     ksb, vsb, kwb, vwb, m_s, l_s, acc_s, *, tq, t, kc, n_cmp):
    qi = pl.program_id(2)

    @pl.when(qi == 0)
    def _():
        ksb[...] = ks_ref[0].astype(BF16)
        vsb[...] = vs_ref[0].astype(BF16)
        kwb[...] = kw_ref[0].astype(BF16)
        vwb[...] = vw_ref[0].astype(BF16)

    n_blk = -(-t // L_SEL)
    ncl = ck_ref.shape[1]
    q0 = qi * tq
    pos = q0 + _iota((tq, 1), 0)
    scale = HEAD_DIM ** -0.5
    hs = [slice(h * HEAD_DIM, (h + 1) * HEAD_DIM) for h in range(HPG)]
    qb = [q_ref[:, hs[h]].astype(BF16) for h in range(HPG)]

    ckb = ck_ref[0].astype(BF16)
    cvb = cv_ref[0].astype(BF16)
    ci = _iota((1, ncl), 1)
    cmask = (ci * S_CMP + (L_CMP - 1) <= pos) & (ci < n_cmp)
    imp_c = jnp.zeros((tq, ncl), F32)
    o_c = []
    for h in range(HPG):
        p = _softmax_masked(_dot_nt(qb[h], ckb) * scale, cmask)
        imp_c = imp_c + p
        o_c.append(_dot(p.astype(BF16), cvb))

    imp = _dot_f32_by_01(imp_c, _block_importance_matrix(ncl, LANES, n_blk))
    selb = _select_blocks(imp, pos, n_blk).astype(BF16)

    for h in range(HPG):
        m_s[h] = jnp.full((tq, 1), NEG, F32)
        l_s[h] = jnp.zeros((tq, 1), F32)
        acc_s[h] = jnp.zeros((tq, HEAD_DIM), F32)

    def chunk(ki, carry):
        k0 = pl.multiple_of(ki * kc, kc)
        kb = ksb[pl.ds(k0, kc), :]
        vb = vsb[pl.ds(k0, kc), :]
        kpos = k0 + _iota((1, kc), 1)
        blk = lax.shift_right_logical(k0 + _iota((LANES, kc), 1), int(math.log2(L_SEL)))
        expand = jnp.where(blk == _iota((LANES, kc), 0), 1.0, 0.0).astype(BF16)
        mask = (_dot(selb, expand) > 0.5) & (kpos <= pos)
        for h in range(HPG):
            _online_update(m_s.at[h], l_s.at[h], acc_s.at[h], _dot_nt(qb[h], kb) * scale, mask,
                           lambda p: _dot(p.astype(BF16), vb))
        return carry

    n_chunks = (q0 + tq + kc - 1) // kc
    lax.fori_loop(0, n_chunks, chunk, 0)

    ws = min(WINDOW + tq, t)
    w0 = pl.multiple_of(jnp.minimum(jnp.maximum(q0 - WINDOW, 0), t - ws), tq)
    kwv = kwb[pl.ds(w0, ws), :]
    vwv = vwb[pl.ds(w0, ws), :]
    wpos = w0 + _iota((1, ws), 1)
    wmask = (wpos <= pos) & (wpos > pos - WINDOW)

    gates = jax.nn.sigmoid(gt_ref[...])
    for h in range(HPG):
        o_s = _online_finish(l_s.at[h], acc_s.at[h])
        pw = _softmax_masked(_dot_nt(qb[h], kwv) * scale, wmask)
        o_w = _dot(pw.astype(BF16), vwv)
        o = (gates[:, h:h + 1] * o_c[h] + gates[:, HPG + h:HPG + h + 1] * o_s
             + gates[:, 2 * HPG + h:2 * HPG + h + 1] * o_w)
        a_ref[:, hs[h]] = (o * _silu(z_ref[:, hs[h]])).astype(BF16)


def _nsa_prompt_attend(qz, gates, ck, cv, kv, *, b, t, tq, kc):
    gq = HPG * HEAD_DIM
    nq = t // tq
    n_cmp = t // S_CMP - (L_CMP // S_CMP) + 1
    ncl = ck.shape[1]
    rowq = lambda i, g, q: i * nq + q
    kvspec = lambda idx: pl.BlockSpec((1, t, HEAD_DIM), lambda i, g, q: (idx, i, g))
    kvb = lambda: pltpu.VMEM((t, HEAD_DIM), BF16)
    return pl.pallas_call(
        functools.partial(_nsa_prompt_body, tq=tq, t=t, kc=kc, n_cmp=n_cmp),
        grid=(b, N_KV, nq),
        in_specs=[
            pl.BlockSpec((tq, gq), lambda i, g, q: (rowq(i, g, q), g)),
            pl.BlockSpec((tq, gq), lambda i, g, q: (rowq(i, g, q), N_KV + g)),
            pl.BlockSpec((tq, LANES), lambda i, g, q: (rowq(i, g, q), g)),
            pl.BlockSpec((1, ncl, HEAD_DIM), lambda i, g, q: (i, 0, g)),
            pl.BlockSpec((1, ncl, HEAD_DIM), lambda i, g, q: (i, 0, g)),
            kvspec(2), kvspec(3), kvspec(4), kvspec(5),
        ],
        out_specs=pl.BlockSpec((tq, gq), lambda i, g, q: (rowq(i, g, q), g)),
        out_shape=jax.ShapeDtypeStruct((b * t, N_KV * gq), BF16),
        scratch_shapes=[
            kvb(), kvb(), kvb(), kvb(),
            pltpu.VMEM((HPG, tq, 1), F32),
            pltpu.VMEM((HPG, tq, 1), F32),
            pltpu.VMEM((HPG, tq, HEAD_DIM), F32),
        ],
        compiler_params=_cparams(("parallel", "parallel", "arbitrary")),
        name="nsa_prompt_attend",
    )(qz, qz, gates, ck, cv, kv, kv, kv, kv)


def _nsa_sample_body(pt_ref, *refs, pages, n_pool_steps, past, n_new, wb):
    sk = refs[0:pages]
    sv = refs[pages:2 * pages]
    (nks4, nvs4, nkw4, nvw4, ck_ref, cv_ref, wk_ref, wv_ref, q_ref, z_ref, gt_ref,
     a_ref, wko_ref, wvo_ref, qh, selr, m_s, l_s, acc, oc, ow) = refs[2 * pages:]
    nks_ref, nvs_ref, nkw_ref, nvw_ref = nks4.at[0, 0], nvs4.at[0, 0], nkw4.at[0, 0], nvw4.at[0, 0]
    s = pl.program_id(1)
    rows = N_HEADS * n_new
    scale = HEAD_DIM ** -0.5
    n_blk = past // L_SEL + -(-n_new // L_SEL)
    sel_lanes = selr.shape[1]
    g_shift = int(math.log2(N_KV))
    rid = _iota((rows, 1), 0)
    pos = past + rid % n_new
    q_grp = rid // (HPG * n_new)
    new_cols = nks4.shape[2]

    def col_group(n):
        return _iota((1, n), 1) & (N_KV - 1)

    def col_row(n):
        return lax.shift_right_logical(_iota((1, n), 1), g_shift)

    @pl.when(s == 0)
    def _():
        qh[...] = jnp.concatenate(
            [q_ref[:, hh * HEAD_DIM:(hh + 1) * HEAD_DIM] for hh in range(N_HEADS)], axis=0)
        qb = qh[...].astype(BF16)

        nc = ck_ref.shape[1]
        cmask = (col_group(nc) == q_grp) & (col_row(nc) * S_CMP + (L_CMP - 1) <= pos)
        p_c = _softmax_masked(_dot_nt(qb, ck_ref[0].astype(BF16)) * scale, cmask)
        oc[...] = _dot(p_c.astype(BF16), cv_ref[0].astype(BF16))

        imp_c = jnp.concatenate(
            [sum(p_c[(g * HPG + h) * n_new:(g * HPG + h + 1) * n_new, :] for h in range(HPG))
             for g in range(N_KV)], axis=0)
        imp = _dot_f32_by_01(imp_c, _block_importance_matrix(nc, sel_lanes, n_blk, N_KV))
        pos_gt = past + (_iota((N_KV * n_new, 1), 0) % n_new)
        sel = _select_blocks(imp, pos_gt, n_blk)
        selr[...] = jnp.concatenate(
            [sel[(hh // HPG) * n_new:(hh // HPG + 1) * n_new, :] for hh in range(N_HEADS)], axis=0)

        nw = wk_ref.shape[1]
        wpos = (past - wb) + col_row(nw)
        npos = past + col_row(new_cols)
        mask = jnp.concatenate(
            [(col_group(nw) == q_grp) & (wpos <= pos) & (wpos > pos - WINDOW),
             (col_group(new_cols) == q_grp) & (npos <= pos) & (col_row(new_cols) < n_new)], axis=1)
        sc = jnp.concatenate(
            [_dot_nt(qb, wk_ref[0].astype(BF16)), _dot_nt(qb, nkw_ref[...].astype(BF16))], axis=1) * scale
        p_w = _softmax_masked(sc, mask)
        ow[...] = (_dot(p_w[:, 0:nw].astype(BF16), wv_ref[0].astype(BF16))
                   + _dot(p_w[:, nw:].astype(BF16), nvw_ref[...].astype(BF16)))

        keep = (wb - n_new) * N_KV
        wko_ref[0, 0:keep, :] = wk_ref[0, n_new * N_KV:nw, :]
        wko_ref[0, keep:nw, :] = nkw_ref[0:n_new * N_KV, :]
        wvo_ref[0, 0:keep, :] = wv_ref[0, n_new * N_KV:nw, :]
        wvo_ref[0, keep:nw, :] = nvw_ref[0:n_new * N_KV, :]

        m_s[...] = jnp.full(m_s.shape, NEG, F32)
        l_s[...] = jnp.zeros(l_s.shape, F32)
        acc[...] = jnp.zeros(acc.shape, F32)

    @pl.when(s < n_pool_steps)
    def _():
        qb = qh[...].astype(BF16)
        nk = pages * PAGE_ROWS
        sc = jnp.concatenate([_dot_nt(qb, sk[i][...].astype(BF16)) for i in range(pages)], axis=1) * scale
        blk_shift = int(math.log2(L_SEL * N_KV))
        blk = s * (nk >> blk_shift) + lax.shift_right_logical(_iota((sel_lanes, nk), 1), blk_shift)
        expand = jnp.where(blk == _iota((sel_lanes, nk), 0), 1.0, 0.0).astype(BF16)
        mask = (_dot(selr[...].astype(BF16), expand) > 0.5) & (col_group(nk) == q_grp)

        def pv(p):
            return sum(_dot(p[:, i * PAGE_ROWS:(i + 1) * PAGE_ROWS].astype(BF16), sv[i][...].astype(BF16))
                       for i in range(pages))

        _online_update(m_s, l_s, acc, sc, mask, pv)

    @pl.when(s == n_pool_steps)
    def _():
        qb = qh[...].astype(BF16)
        new_blk = past // L_SEL
        npos = past + col_row(new_cols)
        mask = ((selr[:, new_blk:new_blk + 1] > 0.5) & (col_group(new_cols) == q_grp) & (npos <= pos)
                & (col_row(new_cols) < n_new))
        _online_update(m_s, l_s, acc, _dot_nt(qb, nks_ref[...].astype(BF16)) * scale, mask,
                       lambda p: _dot(p.astype(BF16), nvs_ref[...].astype(BF16)))
        o_s = _online_finish(l_s, acc)

        gates = jax.nn.sigmoid(gt_ref[...])

        def gate_col(br):
            return jnp.concatenate(
                [gates[:, (hh // HPG) * LANES + br * HPG + hh % HPG:(hh // HPG) * LANES + br * HPG + hh % HPG + 1]
                 for hh in range(N_HEADS)], axis=0)

        zr = jnp.concatenate([z_ref[:, hh * HEAD_DIM:(hh + 1) * HEAD_DIM] for hh in range(N_HEADS)], axis=0)
        o = gate_col(0) * oc[...] + gate_col(1) * o_s + gate_col(2) * ow[...]
        res = o * _silu(zr)
        for hh in range(N_HEADS):
            a_ref[:, hh * HEAD_DIM:(hh + 1) * HEAD_DIM] = res[hh * n_new:(hh + 1) * n_new, :]


def _nsa_sample_attend(page_table, pool_k, pool_v, new_pages, ck, cv, win_k, win_v, qz, gates, *, n_new):
    b, n_pages = page_table.shape
    pages = PAGES_PER_STEP
    n_pool_steps = n_pages // pages
    past = n_pages * PAGE_SIZE
    wb = win_k.shape[1] // N_KV
    rows = N_HEADS * n_new
    qw = N_HEADS * HEAD_DIM
    n_blk = past // L_SEL + -(-n_new // L_SEL)
    sel_lanes = -(-n_blk // LANES) * LANES
    new_rows = -(-n_new * N_KV // LANES) * LANES
    newspec = lambda idx: pl.BlockSpec((1, 1, new_rows, HEAD_DIM), lambda bi, s, pt: (idx, bi, 0, 0))
    full = lambda n: pl.BlockSpec((1, n, HEAD_DIM), lambda bi, s, pt: (bi, 0, 0))
    grid_spec = pltpu.PrefetchScalarGridSpec(
        num_scalar_prefetch=1,
        grid=(b, n_pool_steps + 1),
        in_specs=(
            [_page_spec(i, pages, n_pool_steps) for i in range(pages)] * 2
            + [newspec(2), newspec(3), newspec(4), newspec(5),
               full(ck.shape[1]), full(cv.shape[1]), full(win_k.shape[1]), full(win_v.shape[1]),
               pl.BlockSpec((n_new, qw), lambda bi, s, pt: (bi, 0)),
               pl.BlockSpec((n_new, qw), lambda bi, s, pt: (bi, 1)),
               pl.BlockSpec((n_new, N_KV * LANES), lambda bi, s, pt: (bi, 0))]
        ),
        out_specs=[
            pl.BlockSpec((n_new, qw), lambda bi, s, pt: (bi, 0)),
            full(win_k.shape[1]), full(win_v.shape[1]),
        ],
        scratch_shapes=[
            pltpu.VMEM((rows, HEAD_DIM), F32),
            pltpu.VMEM((rows, sel_lanes), F32),
            pltpu.VMEM((rows, 1), F32),
            pltpu.VMEM((rows, 1), F32),
            pltpu.VMEM((rows, HEAD_DIM), F32),
            pltpu.VMEM((rows, HEAD_DIM), F32),
            pltpu.VMEM((rows, HEAD_DIM), F32),
        ],
    )
    body = functools.partial(_nsa_sample_body, pages=pages, n_pool_steps=n_pool_steps, past=past,
                             n_new=n_new, wb=wb)

    return pl.pallas_call(
        body,
        grid_spec=grid_spec,
        out_shape=[
            jax.ShapeDtypeStruct((b * n_new, qw), F32),
            jax.ShapeDtypeStruct(win_k.shape, F32),
            jax.ShapeDtypeStruct(win_v.shape, F32),
        ],
        compiler_params=_cparams(("parallel", "arbitrary")),
        name="nsa_sample_attend",
    )(page_table, *([pool_k] * pages), *([pool_v] * pages), new_pages, new_pages, new_pages, new_pages,
      ck, cv, win_k, win_v, qz, qz, gates)


def _rope_tables(pos, dk):
    half = dk // 2
    inv = 1.0 / (ROPE_BASE ** jnp.linspace(0.0, 1.0, half))
    ang = pos.astype(F32)[:, None] * inv[None, :]
    cos = jnp.repeat(jnp.cos(ang), 2, axis=1)
    sin = jnp.stack([-jnp.sin(ang), jnp.sin(ang)], axis=-1).reshape(pos.shape[0], dk)
    return cos, sin


def _pooling_matrix(w_pos):
    eye = jnp.eye(SUB_PER_PAGE, dtype=F32)
    first = jnp.einsum("mn,lg->gmnl", eye, w_pos[:S_CMP]).reshape(N_KV, SUB_PER_PAGE, PAGE_SIZE)
    second = jnp.einsum("mn,lg->gmnl", eye, w_pos[S_CMP:]).reshape(N_KV, SUB_PER_PAGE, PAGE_SIZE)
    return jnp.concatenate([first, second], axis=1).astype(BF16)


def _pooling_matrix_rg(w_pos):
    band = _pooling_matrix(w_pos).astype(F32).reshape(N_KV, 2, SUB_PER_PAGE, PAGE_SIZE)
    full = jnp.einsum("ghmr,gk->hmgrk", band, jnp.eye(N_KV, dtype=F32))
    return full.reshape(2 * SUB_ROWS, PAGE_ROWS).astype(BF16)


def _split_nsa_w_in(w):
    qw = N_HEADS * HEAD_DIM
    ng = 3 * N_HEADS
    d = w.shape[0]
    w_qz = w[:, :2 * qw]
    w_g = w[:, 2 * qw:2 * qw + ng].reshape(d, 3, N_KV, HPG).transpose(0, 2, 1, 3).reshape(d, N_KV, 3 * HPG)
    w_g = jnp.pad(w_g, ((0, 0), (0, 0), (0, LANES - 3 * HPG))).reshape(d, N_KV * LANES)
    w_kv = w[:, 2 * qw + ng:]
    return w_qz.astype(BF16), w_g.astype(BF16), w_kv.astype(BF16)


def _tile_rows(m, pref):
    return pref if m % pref == 0 else m


def _run_group(x, *, pos0, pool_buf, ret_s0, conv_buf, lru_h0, nsa_cache, params):
    nb, t, d = x.shape
    m = nb * t
    tm = _tile_rows(m, OUT_ROWS)
    tm_in = _tile_rows(m, IN_ROWS)
    fused_layout = t % tm == 0 and t % tm_in == 0
    nt = t // tm if fused_layout else 1
    nt_in = t // tm_in if fused_layout else 1
    bm_in = lambda i, j: (i, 0)
    bm_io = lambda i: (i, 0)
    tb_in = lambda i, j: (i % nt_in, i // nt_in)
    tb_io = lambda i: (i % nt, i // nt)
    pos = pos0 + jnp.arange(t, dtype=jnp.int32)
    vec = lambda v: v.reshape(1, -1)

    depth = params["norm_pre"].shape[0]
    cur = x.reshape(m, d)
    cur_is_tm = False
    states = {k: [] for k in ("pool", "ret", "conv", "lru", "nsa")}

    def project(xv, x_in, g_pre, w):
        n = w.shape[1]
        tn = IN_COLS if n % IN_COLS == 0 else n
        return _norm_matmul(xv, x_in, g_pre, w, m=m, tm=tm_in, tn=tn, out_shape=(m, n),
                            out_block=(tm_in, tn), out_map=lambda a, b_: (a, b_))

    def to_bm_view(a):
        return (a.reshape(t, nb * d), tb_in, tb_io) if fused_layout else (a, bm_in, bm_io)

    for i in range(depth):
        kind, j = i % 4, i // 4
        g_pre = vec(params["norm_pre"][i])
        g_post = vec(params["norm_post"][i])
        if kind == 0:
            w_in = params["pool_w_in"][j].astype(BF16)
            e = w_in.shape[1] // 2
            uz = project(cur, bm_in, g_pre, w_in)
            buf16 = jnp.pad(pool_buf[j], ((0, 0), (POOL_HALO - pool_buf[j].shape[1], 0), (0, 0)))
            a, st = _pool_mixer(uz, buf16, params["pool_w_grp"][j].astype(BF16), vec(params["pool_b_grp"][j]),
                                vec(params["pool_scale"][j]), b=nb, t=t, tt=_tile_rows(t, 256), pos0=pos0)
            states["pool"].append(st[:, 1:, :])
            cur = _out_proj(a, params["pool_w_out"][j].astype(BF16), cur, bm_io, g_post, m=m, tm=tm,
                            out_shape=(m, d), out_map=bm_io)
        elif kind == 1:
            w_in = params["ret_w_in"][j].astype(BF16)
            dk = d // R_HEADS
            dv = 2 * dk
            proj = project(cur, bm_in, g_pre, w_in)
            cos_t, sin_t = _rope_tables(pos, dk)
            lg = jnp.log(1.0 - 2.0 ** (-5.0 - jnp.arange(R_HEADS, dtype=F32)))
            a, s_new = _retention_mixer(proj, cos_t, sin_t, lg, ret_s0[j], b=nb, t=t, dk=dk, dv=dv)
            states["ret"].append(s_new)
            w_out = params["ret_w_out"][j].astype(BF16)
            if fused_layout:
                cur = _out_proj(a, w_out, cur, bm_io, g_post, m=m, tm=tm,
                                out_shape=(t, nb * d), out_map=tb_io).reshape(m, d)
            else:
                cur = _out_proj(a, w_out, cur, bm_io, g_post, m=m, tm=tm, out_shape=(m, d), out_map=bm_io)
                cur = cur.reshape(nb, t, d).swapaxes(0, 1).reshape(m, d)
            cur_is_tm = True
        elif kind == 2:
            assert cur_is_tm
            w_in = params["lru_w_in"][j].astype(BF16)
            d_rnn = w_in.shape[1] // 2
            uz = project(cur, bm_in, g_pre, w_in)
            buf = conv_buf[j].swapaxes(0, 1).reshape((CONV_W - 1) * nb, d_rnn)
            tt = max(1, min(t, 512 // nb))
            a, cs_new, h_new = _lru_mixer(
                uz, buf, params["lru_conv_w"][j], vec(params["lru_conv_b"][j]),
                params["lru_w_a"][j].astype(BF16), vec(params["lru_b_a"][j]),
                params["lru_w_x"][j].astype(BF16), vec(params["lru_b_x"][j]),
                vec(params["lru_lam"][j]), lru_h0[j], nb=nb, t=t, tt=tt)
            states["conv"].append(cs_new.reshape(CONV_W - 1, nb, d_rnn).swapaxes(0, 1))
            states["lru"].append(h_new)
            cur = _out_proj(a, params["lru_w_out"][j].astype(BF16), cur, bm_io, g_post, m=m, tm=tm,
                            out_shape=(m, d), out_map=bm_io)
        else:
            if cur_is_tm and not fused_layout:
                cur = cur.reshape(t, nb, d).swapaxes(0, 1).reshape(m, d)
                cur_is_tm = False
            xv, x_in, x_io = to_bm_view(cur) if cur_is_tm else (cur, bm_in, bm_io)
            w_qz, w_g, w_kv = _split_nsa_w_in(params["nsa_w_in"][j])
            qz = project(xv, x_in, g_pre, w_qz)
            gates = project(xv, x_in, g_pre, w_g)
            n_kv_t = w_kv.shape[1] // KV_W
            kv, kv_rg = _norm_matmul_kv(xv, x_in, g_pre, w_kv, m=m, tm=tm_in)
            pos_k, pos_v = params["nsa_cmp_pos_k"][j], params["nsa_cmp_pos_v"][j]
            lin_k, lin_v = params["nsa_cmp_lin_k"][j], params["nsa_cmp_lin_v"][j]
            kv5 = kv_rg.reshape(n_kv_t, nb, t, N_KV, HEAD_DIM)
            if nsa_cache is None:
                ck, cv = _compress_rows(kv, _pooling_matrix(pos_k), _pooling_matrix(pos_v),
                                        lin_k.astype(BF16), lin_v.astype(BF16), b=nb, t=t)
                a = _nsa_prompt_attend(qz, gates, ck, cv, kv, b=nb, t=t, tq=256, kc=512)
                wb = min(WINDOW, t)
                new_state = (kv5[0], kv5[1], kv5[2], kv5[3], kv5[4][:, t - wb:], kv5[5][:, t - wb:])
            else:
                page_table, cmp_k, cmp_v, sel_k, sel_v, win_k, win_v = nsa_cache
                pool2d = lambda c: c[j].reshape(c.shape[1] * PAGE_ROWS, HEAD_DIM)
                win2d = lambda c: c[j].reshape(c.shape[1], c.shape[2] * N_KV, HEAD_DIM)
                all_groups = lambda w: w.transpose(1, 0, 2).reshape(HEAD_DIM, KV_W).astype(BF16)
                new_pages = jnp.pad(kv_rg.reshape(n_kv_t, nb, t * N_KV, HEAD_DIM),
                                    ((0, 0), (0, 0), (0, PAGE_ROWS - t * N_KV), (0, 0)))
                ck, cv = _compress_paged(page_table, pool2d(cmp_k), pool2d(cmp_v), new_pages,
                                         _pooling_matrix_rg(pos_k), _pooling_matrix_rg(pos_v),
                                         all_groups(lin_k), all_groups(lin_v))
                a, wk_new, wv_new = _nsa_sample_attend(page_table, pool2d(sel_k), pool2d(sel_v), new_pages, ck, cv,
                                                       win2d(win_k), win2d(win_v), qz, gates, n_new=t)
                wshape = (nb, win_k.shape[2], N_KV, HEAD_DIM)
                new_state = (kv5[0], kv5[1], kv5[2], kv5[3], wk_new.reshape(wshape), wv_new.reshape(wshape))
            states["nsa"].append(new_state)
            cur = _out_proj(a, params["nsa_w_out"][j].astype(BF16), xv, x_io, g_post, m=m, tm=tm,
                            out_shape=(m, d), out_map=bm_io)
            cur_is_tm = False
    if cur_is_tm:
        cur = cur.reshape(t, nb, d).swapaxes(0, 1).reshape(m, d)
    return cur.reshape(nb, t, d), states


def kernel(x_prompt, x_sample, state_pool, state_ret, state_conv, state_lru, cache_cmp_k, cache_cmp_v, cache_sel_k, cache_sel_v, cache_win_k, cache_win_v, page_table, norm_pre, norm_post, pool_w_in, pool_w_grp, pool_b_grp, pool_scale, pool_w_out, ret_w_in, ret_w_out, lru_w_in, lru_conv_w, lru_conv_b, lru_w_a, lru_b_a, lru_w_x, lru_b_x, lru_lam, lru_w_out, nsa_w_in, nsa_cmp_pos_k, nsa_cmp_lin_k, nsa_cmp_pos_v, nsa_cmp_lin_v, nsa_w_out):
    params = dict(
        norm_pre=norm_pre, norm_post=norm_post, pool_w_in=pool_w_in, pool_w_grp=pool_w_grp, pool_b_grp=pool_b_grp,
        pool_scale=pool_scale, pool_w_out=pool_w_out, ret_w_in=ret_w_in, ret_w_out=ret_w_out, lru_w_in=lru_w_in,
        lru_conv_w=lru_conv_w, lru_conv_b=lru_conv_b, lru_w_a=lru_w_a, lru_b_a=lru_b_a, lru_w_x=lru_w_x,
        lru_b_x=lru_b_x, lru_lam=lru_lam, lru_w_out=lru_w_out, nsa_w_in=nsa_w_in, nsa_cmp_pos_k=nsa_cmp_pos_k,
        nsa_cmp_lin_k=nsa_cmp_lin_k, nsa_cmp_pos_v=nsa_cmp_pos_v, nsa_cmp_lin_v=nsa_cmp_lin_v, nsa_w_out=nsa_w_out)
    b = x_prompt.shape[0]
    past = page_table.shape[1] * PAGE_SIZE
    zeros_like_state = lambda s: jnp.zeros((s.shape[0], b) + s.shape[2:], F32)
    yp, sp = _run_group(
        x_prompt, pos0=0, pool_buf=zeros_like_state(state_pool), ret_s0=zeros_like_state(state_ret),
        conv_buf=zeros_like_state(state_conv), lru_h0=zeros_like_state(state_lru), nsa_cache=None, params=params)
    ys, ss = _run_group(
        x_sample, pos0=past, pool_buf=state_pool, ret_s0=state_ret, conv_buf=state_conv, lru_h0=state_lru,
        nsa_cache=(page_table, cache_cmp_k, cache_cmp_v, cache_sel_k, cache_sel_v, cache_win_k, cache_win_v),
        params=params)
    out = [yp, ys]
    for key in ("pool", "ret", "conv", "lru"):
        out += [jnp.stack(sp[key]), jnp.stack(ss[key])]
    for idx in range(6):
        out += [jnp.stack([e[idx] for e in sp["nsa"]]), jnp.stack([e[idx] for e in ss["nsa"]])]
    return tuple(out)
```

```python
import functools
import math

import jax
import jax.numpy as jnp
from jax import lax
from jax.experimental import pallas as pl
from jax.experimental.pallas import tpu as pltpu

F32 = jnp.float32
BF16 = jnp.bfloat16

EPS = 1e-6
NEG = -1e30
FORCE = 1e6

PAGE_SIZE = 128
POOL_WINDOWS = (2, 4, 8, 16)
R_HEADS = 4
R_CHUNK = 128
ROPE_BASE = 10000.0
LRU_BW = 256
CONV_W = 4
LRU_C = 8.0
N_HEADS = 16
N_KV = 4
HPG = N_HEADS // N_KV
HEAD_DIM = 128
L_CMP = 32
S_CMP = 16
L_SEL = 64
N_SEL = 16
WINDOW = 512
KV_W = N_KV * HEAD_DIM
SUB_PER_PAGE = PAGE_SIZE // S_CMP
PAGES_PER_STEP = 8

LANES = 128
SUBLANES = 8
VMEM_LIMIT = 56 * 1024 * 1024
IN_ROWS, IN_COLS = 1024, 1024
OUT_ROWS = 512
NSA_TQ, NSA_KC = 256, 256


def _cparams(sem):
    return pltpu.CompilerParams(dimension_semantics=sem, vmem_limit_bytes=VMEM_LIMIT)


def _act_dtype(rows):
    return BF16 if rows % (2 * SUBLANES) == 0 else F32


def _dot(a, b):
    return jnp.dot(a, b, preferred_element_type=F32)


def _dot_nt(a, b):
    return lax.dot_general(a, b, (((1,), (1,)), ((), ())), preferred_element_type=F32)


def _dot_tn(a, b):
    return lax.dot_general(a, b, (((0,), (0,)), ((), ())), preferred_element_type=F32)


def _silu(x):
    return x * jax.nn.sigmoid(x)


def _iota(shape, dim):
    return lax.broadcasted_iota(jnp.int32, shape, dim)


def _norm_mm_body(x_ref, g_ref, w_ref, o_ref, h_ref):
    @pl.when(pl.program_id(1) == 0)
    def _():
        x = x_ref[...]
        ms = jnp.mean(x * x, axis=-1, keepdims=True)
        h_ref[...] = (x * lax.rsqrt(ms + EPS) * g_ref[...]).astype(BF16)

    o_ref[...] = _dot(h_ref[...], w_ref[...]).reshape(o_ref.shape)


def _norm_matmul(x, x_map, g, w, *, m, tm, tn, out_shape, out_block, out_map):
    d, n = w.shape
    return pl.pallas_call(
        _norm_mm_body,
        grid=(m // tm, n // tn),
        in_specs=[
            pl.BlockSpec((tm, d), x_map),
            pl.BlockSpec((1, d), lambda i, j: (0, 0)),
            pl.BlockSpec((d, tn), lambda i, j: (0, j)),
        ],
        out_specs=pl.BlockSpec(out_block, out_map),
        out_shape=jax.ShapeDtypeStruct(out_shape, F32),
        scratch_shapes=[pltpu.VMEM((tm, d), BF16)],
        compiler_params=_cparams(("parallel", "arbitrary")),
        name="norm_matmul",
    )(x, g, w)


def _norm_mm_kv_body(x_ref, g_ref, w_ref, o_ref, *rest):
    oi_refs, h_ref = rest[:-1], rest[-1]
    j = pl.program_id(1)

    @pl.when(j == 0)
    def _():
        x = x_ref[...]
        ms = jnp.mean(x * x, axis=-1, keepdims=True)
        h_ref[...] = (x * lax.rsqrt(ms + EPS) * g_ref[...]).astype(BF16)

    r = _dot(h_ref[...], w_ref[...])
    o_ref[0] = r
    for jj, oi_ref in enumerate(oi_refs):
        @pl.when(j == jj)
        def _():
            for g in range(N_KV):
                oi_ref[pl.ds(g, r.shape[0], stride=N_KV), :] = r[:, g * HEAD_DIM:(g + 1) * HEAD_DIM]


def _norm_matmul_kv(x, x_map, g, w, *, m, tm):
    d, n = w.shape
    n_t = n // KV_W
    outs = pl.pallas_call(
        _norm_mm_kv_body,
        grid=(m // tm, n_t),
        in_specs=[
            pl.BlockSpec((tm, d), x_map),
            pl.BlockSpec((1, d), lambda i, j: (0, 0)),
            pl.BlockSpec((d, KV_W), lambda i, j: (0, j)),
        ],
        out_specs=[pl.BlockSpec((1, tm, KV_W), lambda i, j: (j, i, 0))]
        + [pl.BlockSpec((tm * N_KV, HEAD_DIM), lambda i, j: (i, 0)) for _ in range(n_t)],
        out_shape=[jax.ShapeDtypeStruct((n_t, m, KV_W), F32)]
        + [jax.ShapeDtypeStruct((m * N_KV, HEAD_DIM), F32) for _ in range(n_t)],
        scratch_shapes=[pltpu.VMEM((tm, d), BF16)],
        compiler_params=_cparams(("parallel", "arbitrary")),
        name="norm_matmul_kv",
    )(x, g, w)
    return outs[0], outs[1:]


def _out_proj_body(a_ref, w_ref, x_ref, g_ref, o_ref):
    y = _dot(a_ref[...].astype(BF16), w_ref[...])
    ms = jnp.mean(y * y, axis=-1, keepdims=True)
    o_ref[...] = x_ref[...] + y * lax.rsqrt(ms + EPS) * g_ref[...]


def _out_proj(a, w, x, x_map, g, *, m, tm, out_shape, out_map):
    e, d = w.shape
    return pl.pallas_call(
        _out_proj_body,
        grid=(m // tm,),
        in_specs=[
            pl.BlockSpec((tm, e), lambda i: (i, 0)),
            pl.BlockSpec((e, d), lambda i: (0, 0)),
            pl.BlockSpec((tm, d), x_map),
            pl.BlockSpec((1, d), lambda i: (0, 0)),
        ],
        out_specs=pl.BlockSpec((tm, d), out_map),
        out_shape=jax.ShapeDtypeStruct(out_shape, F32),
        compiler_params=_cparams(("parallel",)),
        name="out_proj",
    )(a, w, x, g)


POOL_HALO = 16
POOL_PAD = 8


def _pool_body(u_ref, z_ref, buf_ref, wg_ref, bg_ref, sc_ref, a_ref, st_ref, ext, sa, sb,
               *, tt, pos0, gw):
    ti = pl.program_id(1)
    lo = POOL_PAD
    top = POOL_PAD + POOL_HALO
    rows = top + tt

    @pl.when(ti == 0)
    def _():
        ext[0:lo, :] = jnp.zeros((lo, ext.shape[1]), F32)
        ext[lo:top, :] = buf_ref[0]

    @pl.when(ti > 0)
    def _():
        ext[lo:top, :] = ext[lo + tt:top + tt, :]

    u = u_ref[...]
    ext[top:rows, :] = u
    sa[0:lo, :] = jnp.zeros((lo, gw), F32)
    sb[0:lo, :] = jnp.zeros((lo, gw), F32)
    pos = _iota((tt, 1), 0) + (pos0 + ti * tt)
    for g, w in enumerate(POOL_WINDOWS):
        cs = slice(g * gw, (g + 1) * gw)
        sa[lo:rows, :] = ext[lo:rows, cs] + ext[lo - 1:rows - 1, cs]
        cur, oth, sh = sa, sb, 2
        while sh < w:
            oth[lo:rows, :] = cur[lo:rows, :] + cur[lo - sh:rows - sh, :]
            cur, oth, sh = oth, cur, sh * 2
        cnt = jnp.minimum(pos + 1, w).astype(F32)
        mean = cur[top:rows, :] / cnt
        mixed = _dot((mean - u[:, cs]).astype(BF16), wg_ref[g]) + bg_ref[:, cs]
        a_ref[:, cs] = (mixed * sc_ref[:, cs] * _silu(z_ref[:, cs])).astype(a_ref.dtype)
    st_ref[0] = ext[lo + tt:top + tt, :]


def _pool_mixer(uz, buf16, wg, bg, sc, *, b, t, tt, pos0):
    e = buf16.shape[-1]
    gw = e // len(POOL_WINDOWS)
    nt = t // tt
    rows = POOL_PAD + POOL_HALO + tt
    return pl.pallas_call(
        functools.partial(_pool_body, tt=tt, pos0=pos0, gw=gw),
        grid=(b, nt),
        in_specs=[
            pl.BlockSpec((tt, e), lambda i, j: (i * nt + j, 0)),
            pl.BlockSpec((tt, e), lambda i, j: (i * nt + j, 1)),
            pl.BlockSpec((1, POOL_HALO, e), lambda i, j: (i, 0, 0)),
            pl.BlockSpec((len(POOL_WINDOWS), gw, gw), lambda i, j: (0, 0, 0)),
            pl.BlockSpec((1, e), lambda i, j: (0, 0)),
            pl.BlockSpec((1, e), lambda i, j: (0, 0)),
        ],
        out_specs=[
            pl.BlockSpec((tt, e), lambda i, j: (i * nt + j, 0)),
            pl.BlockSpec((1, POOL_HALO, e), lambda i, j: (i, 0, 0)),
        ],
        out_shape=[
            jax.ShapeDtypeStruct((b * t, e), _act_dtype(tt)),
            jax.ShapeDtypeStruct((b, POOL_HALO, e), F32),
        ],
        scratch_shapes=[
            pltpu.VMEM((rows, e), F32),
            pltpu.VMEM((rows, gw), F32),
            pltpu.VMEM((rows, gw), F32),
        ],
        compiler_params=_cparams(("parallel", "arbitrary")),
        name="pool_mixer",
    )(uz, uz, buf16, wg, bg, sc)


def _ret_body(lg_ref, q_ref, k_ref, v_ref, g_ref, cos_ref, sin_ref, s0_ref, a_ref, so_ref, s_scr,
              *, c_len, c_pad, dk, dv):
    c = pl.program_id(1)

    @pl.when(c == 0)
    def _():
        s_scr[...] = s0_ref[0]

    cosv = cos_ref[...]
    sinv = sin_ref[...]
    even = (_iota((c_len, dk), 1) & 1) == 0

    def rot(x):
        nxt = pltpu.roll(x, dk - 1, axis=1)
        prv = pltpu.roll(x, 1, axis=1)
        return x * cosv + jnp.where(even, nxt, prv) * sinv

    def pad(x):
        if c_pad == c_len:
            return x
        return jnp.concatenate([x, jnp.zeros((c_pad - c_len, x.shape[1]), x.dtype)], axis=0)

    i_col = _iota((c_pad, 1), 0).astype(F32)
    j_row = _iota((1, c_pad), 1).astype(F32)
    diff = i_col - j_row
    for h in range(R_HEADS):
        ks = slice(h * dk, (h + 1) * dk)
        vs = slice(h * dv, (h + 1) * dv)
        lg = lg_ref[h]
        q = pad(rot(q_ref[:, ks]))
        k = pad(rot(k_ref[:, ks]) * (dk ** -0.5))
        v = pad(v_ref[:, vs])
        intra = jnp.where(diff >= 0, jnp.exp(jnp.maximum(diff, 0.0) * lg), 0.0)
        q_dec = jnp.exp((i_col + 1.0) * lg)
        k_dec = jnp.exp((c_len - 1.0 - i_col) * lg)
        c_dec = jnp.exp(jnp.full((1, 1), float(c_len), F32) * lg)

        qb = q.astype(BF16)
        vb = v.astype(BF16)
        s = _dot_nt(qb, k.astype(BF16)) * intra
        s_old = s_scr[h]
        o = _dot(s.astype(BF16), vb) + _dot(qb, s_old.astype(BF16)) * q_dec
        s_scr[h] = s_old * c_dec + _dot_tn((k * k_dec).astype(BF16), vb)
        o = o[0:c_len]
        o = o * lax.rsqrt(jnp.mean(o * o, axis=-1, keepdims=True) + EPS)
        a_ref[:, vs] = (o * _silu(g_ref[:, vs])).astype(a_ref.dtype)

    @pl.when(c == pl.num_programs(1) - 1)
    def _():
        so_ref[0] = s_scr[...]


def _retention_mixer(proj, cos_t, sin_t, lg, s0, *, b, t, dk, dv):
    c_len = R_CHUNK if t % R_CHUNK == 0 else t
    c_pad = max(c_len, R_CHUNK)
    nc = t // c_len
    nh = R_HEADS
    qk_w, vg_w = nh * dk, nh * dv
    row = lambda i, c: i * nc + c
    return pl.pallas_call(
        functools.partial(_ret_body, c_len=c_len, c_pad=c_pad, dk=dk, dv=dv),
        grid=(b, nc),
        in_specs=[
            pl.BlockSpec(memory_space=pltpu.SMEM),
            pl.BlockSpec((c_len, qk_w), lambda i, c: (row(i, c), 0)),
            pl.BlockSpec((c_len, qk_w), lambda i, c: (row(i, c), 1)),
            pl.BlockSpec((c_len, vg_w), lambda i, c: (row(i, c), 2 * qk_w // vg_w)),
            pl.BlockSpec((c_len, vg_w), lambda i, c: (row(i, c), 2 * qk_w // vg_w + 1)),
            pl.BlockSpec((c_len, dk), lambda i, c: (c, 0)),
            pl.BlockSpec((c_len, dk), lambda i, c: (c, 0)),
            pl.BlockSpec((1, nh, dk, dv), lambda i, c: (i, 0, 0, 0)),
        ],
        out_specs=[
            pl.BlockSpec((c_len, vg_w), lambda i, c: (row(i, c), 0)),
            pl.BlockSpec((1, nh, dk, dv), lambda i, c: (i, 0, 0, 0)),
        ],
        out_shape=[
            jax.ShapeDtypeStruct((b * t, vg_w), _act_dtype(c_len)),
            jax.ShapeDtypeStruct((b, nh, dk, dv), F32),
        ],
        scratch_shapes=[pltpu.VMEM((nh, dk, dv), F32)],
        compiler_params=_cparams(("parallel", "arbitrary")),
        name="retention_mixer",
    )(lg, proj, proj, proj, proj, cos_t, sin_t, s0)


def _lru_body(u_ref, z_ref, buf_ref, cw_ref, cb_ref, wa_ref, ba_ref, wx_ref, bx_ref, lam_ref, h0_ref,
              a_ref, cso_ref, ho_ref, ext, a_s, b_s, h_s, hc, *, tt, nb):
    ti = pl.program_id(1)
    halo = (CONV_W - 1) * nb
    rows = tt * nb

    @pl.when(ti == 0)
    def _():
        ext[0:halo, :] = buf_ref[...]
        hc[...] = h0_ref[...]

    @pl.when(ti > 0)
    def _():
        ext[0:halo, :] = ext[rows:rows + halo, :]

    ext[halo:halo + rows, :] = u_ref[...]
    cw = cw_ref[...]
    c = cb_ref[...] + ext[0:rows, :] * cw[0:1, :]
    for kk in range(1, CONV_W):
        c = c + ext[kk * nb:kk * nb + rows, :] * cw[kk:kk + 1, :]
    cb16 = c.astype(BF16)
    r = jax.nn.sigmoid(_dot(cb16, wa_ref[0]) + ba_ref[...])
    ig = jax.nn.sigmoid(_dot(cb16, wx_ref[0]) + bx_ref[...])
    nl = -lam_ref[...]
    softplus = jnp.maximum(nl, 0.0) + jnp.log1p(jnp.exp(-jnp.abs(nl)))
    log_a = (-LRU_C) * r * softplus
    a_s[...] = jnp.exp(log_a)
    b_s[...] = jnp.sqrt(1.0 - jnp.exp(2.0 * log_a)) * (ig * c)

    def step(t, h):
        rs = pl.ds(pl.multiple_of(t * nb, SUBLANES), nb)
        h = a_s[rs, :] * h + b_s[rs, :]
        h_s[rs, :] = h
        return h

    hc[...] = lax.fori_loop(0, tt, step, hc[...], unroll=8)
    a_ref[...] = (h_s[...] * _silu(z_ref[...])).astype(BF16)
    cso_ref[...] = ext[rows:rows + halo, :]
    ho_ref[...] = hc[...]


def _lru_mixer(uz, buf, cw, cb, wa, ba, wx, bx, lam, h0, *, nb, t, tt):
    d_rnn = h0.shape[-1]
    nblk = d_rnn // LRU_BW
    nt = t // tt
    rows = tt * nb
    halo = (CONV_W - 1) * nb
    vec = lambda: pl.BlockSpec((1, LRU_BW), lambda j, i: (0, j))
    return pl.pallas_call(
        functools.partial(_lru_body, tt=tt, nb=nb),
        grid=(nblk, nt),
        in_specs=[
            pl.BlockSpec((rows, LRU_BW), lambda j, i: (i, j)),
            pl.BlockSpec((rows, LRU_BW), lambda j, i: (i, nblk + j)),
            pl.BlockSpec((halo, LRU_BW), lambda j, i: (0, j)),
            pl.BlockSpec((CONV_W, LRU_BW), lambda j, i: (0, j)),
            vec(),
            pl.BlockSpec((1, LRU_BW, LRU_BW), lambda j, i: (j, 0, 0)),
            vec(),
            pl.BlockSpec((1, LRU_BW, LRU_BW), lambda j, i: (j, 0, 0)),
            vec(),
            vec(),
            pl.BlockSpec((nb, LRU_BW), lambda j, i: (0, j)),
        ],
        out_specs=[
            pl.BlockSpec((rows, LRU_BW), lambda j, i: (i, j)),
            pl.BlockSpec((halo, LRU_BW), lambda j, i: (0, j)),
            pl.BlockSpec((nb, LRU_BW), lambda j, i: (0, j)),
        ],
        out_shape=[
            jax.ShapeDtypeStruct((t * nb, d_rnn), BF16),
            jax.ShapeDtypeStruct((halo, d_rnn), F32),
            jax.ShapeDtypeStruct((nb, d_rnn), F32),
        ],
        scratch_shapes=[
            pltpu.VMEM((halo + rows, LRU_BW), F32),
            pltpu.VMEM((rows, LRU_BW), F32),
            pltpu.VMEM((rows, LRU_BW), F32),
            pltpu.VMEM((rows, LRU_BW), F32),
            pltpu.VMEM((nb, LRU_BW), F32),
        ],
        compiler_params=_cparams(("parallel", "arbitrary")),
        name="lru_mixer",
    )(uz, uz, buf, cw, cb, wa, ba, wx, bx, lam, h0)


def _softmax_masked(s, mask):
    sm = jnp.where(mask, s, NEG)
    m = jnp.max(sm, axis=-1, keepdims=True)
    e = jnp.where(mask, jnp.exp(sm - m), 0.0)
    l = jnp.sum(e, axis=-1, keepdims=True)
    return e / jnp.where(l > 0.0, l, 1.0)


def _dot_f32_by_01(x, m01):
    hi = x.astype(BF16)
    r1 = x - hi.astype(F32)
    mid = r1.astype(BF16)
    lo = (r1 - mid.astype(F32)).astype(BF16)
    return _dot(hi, m01) + _dot(mid, m01) + _dot(lo, m01)


def _block_importance_matrix(n_rows, n_blk_lanes, n_blk, rows_per_cmp=1):
    r, s = L_SEL // S_CMP, L_CMP // S_CMP
    i = lax.shift_right_logical(_iota((n_rows, n_blk_lanes), 0), int(math.log2(rows_per_cmp)))
    j = _iota((n_rows, n_blk_lanes), 1)
    hit = (i >= r * j - (s - 1)) & (i <= r * j + (r - 1)) & (j < n_blk)
    return jnp.where(hit, 1.0, 0.0).astype(BF16)


def _select_blocks(imp, pos, n_blk):
    j = _iota(imp.shape, 1)
    cur = lax.shift_right_logical(pos, int(math.log2(L_SEL)))
    valid = j * L_SEL <= pos
    forced = (j == 0) | (j == cur) | (j == cur - 1)
    score = jnp.where(forced, FORCE, jnp.where(valid, imp, -1.0))
    cnt = jnp.zeros(imp.shape, F32)
    for i in range(n_blk):
        si = score[:, i:i + 1]
        ahead = (si > score) | ((si == score) & (j > i))
        cnt = cnt + jnp.where(ahead, 1.0, 0.0)
    keep = (cnt < float(min(N_SEL, n_blk))) & (score >= 0.0) & (j < n_blk)
    return jnp.where(keep, 1.0, 0.0)


def _select_blocks_t(imp, pos_row, n_blk):
    nb8 = -(-n_blk // SUBLANES) * SUBLANES
    imp_t = imp.T[0:nb8, :]
    j = _iota(imp_t.shape, 0)
    cur = lax.shift_right_logical(pos_row, int(math.log2(L_SEL)))
    valid = j * L_SEL <= pos_row
    forced = (j == 0) | (j == cur) | (j == cur - 1)
    score = jnp.where(forced, FORCE, jnp.where(valid, imp_t, -1.0))
    cnt = jnp.zeros(imp_t.shape, F32)
    for i in range(n_blk):
        si = score[i:i + 1, :]
        ahead = (si > score) | ((si == score) & (j > i))
        cnt = cnt + jnp.where(ahead, 1.0, 0.0)
    keep = (cnt < float(min(N_SEL, n_blk))) & (score >= 0.0) & (j < n_blk)
    sel_t = jnp.where(keep, 1.0, 0.0)
    if nb8 < LANES:
        sel_t = jnp.concatenate([sel_t, jnp.zeros((LANES - nb8, imp.shape[0]), F32)], axis=0)
    return sel_t.T


def _online_update(m_ref, l_ref, acc_ref, s, mask, pv):
    sm = jnp.where(mask, s, NEG)
    m_old = m_ref[...]
    m_new = jnp.maximum(m_old, jnp.max(sm, axis=-1, keepdims=True))
    p = jnp.where(mask, jnp.exp(sm - m_new), 0.0)
    alpha = jnp.exp(m_old - m_new)
    l_ref[...] = alpha * l_ref[...] + jnp.sum(p, axis=-1, keepdims=True)
    acc_ref[...] = alpha * acc_ref[...] + pv(p)
    m_ref[...] = m_new


def _online_finish(l_ref, acc_ref):
    l = l_ref[...]
    return acc_ref[...] / jnp.where(l > 0.0, l, 1.0)


def _cmp_accumulate(page, w2_ref, acc_a, acc_b, row0):
    pb = page.astype(BF16)
    for g in range(N_KV):
        cs = slice(g * HEAD_DIM, (g + 1) * HEAD_DIM)
        ab = _dot(w2_ref[g], pb[:, cs])
        acc_a[pl.ds(row0, SUB_PER_PAGE), cs] = ab[0:SUB_PER_PAGE]
        acc_b[pl.ds(row0, SUB_PER_PAGE), cs] = ab[SUB_PER_PAGE:2 * SUB_PER_PAGE]


def _cmp_finish(acc_a, acc_b, lin_ref, out_ref, ncp):
    pooled = acc_a[0:ncp, :] + acc_b[1:ncp + 1, :]
    for g in range(N_KV):
        cs = slice(g * HEAD_DIM, (g + 1) * HEAD_DIM)
        out_ref[0, :, cs] = _dot(pooled[:, cs].astype(BF16), lin_ref[g])


def _compress_rows_body(k_ref, v_ref, w2k_ref, w2v_ref, lk_ref, lv_ref, ok_ref, ov_ref,
                        aak, abk, aav, abv, *, pages, ncp):
    s = pl.program_id(1)

    @pl.when(s == 0)
    def _():
        tail = jnp.zeros((SUBLANES, KV_W), F32)
        abk[ncp:ncp + SUBLANES, :] = tail
        abv[ncp:ncp + SUBLANES, :] = tail

    for i in range(pages):
        row0 = pl.multiple_of((s * pages + i) * SUB_PER_PAGE, SUBLANES)
        rs = slice(i * PAGE_SIZE, (i + 1) * PAGE_SIZE)
        _cmp_accumulate(k_ref[0, rs, :], w2k_ref, aak, abk, row0)
        _cmp_accumulate(v_ref[0, rs, :], w2v_ref, aav, abv, row0)

    @pl.when(s == pl.num_programs(1) - 1)
    def _():
        _cmp_finish(aak, abk, lk_ref, ok_ref, ncp)
        _cmp_finish(aav, abv, lv_ref, ov_ref, ncp)


def _compress_rows(kv, w2k, w2v, lk, lv, *, b, t):
    pages = min(PAGES_PER_STEP, t // PAGE_SIZE)
    rows = pages * PAGE_SIZE
    ns = t // rows
    ncp = t // S_CMP
    wspec = lambda shp: pl.BlockSpec(shp, lambda i, s: (0, 0, 0))
    acc = lambda: pltpu.VMEM((ncp + SUBLANES, KV_W), F32)
    return pl.pallas_call(
        functools.partial(_compress_rows_body, pages=pages, ncp=ncp),
        grid=(b, ns),
        in_specs=[
            pl.BlockSpec((1, rows, KV_W), lambda i, s: (0, i * ns + s, 0)),
            pl.BlockSpec((1, rows, KV_W), lambda i, s: (1, i * ns + s, 0)),
            wspec(w2k.shape), wspec(w2v.shape), wspec(lk.shape), wspec(lv.shape),
        ],
        out_specs=[
            pl.BlockSpec((1, ncp, KV_W), lambda i, s: (i, 0, 0)),
            pl.BlockSpec((1, ncp, KV_W), lambda i, s: (i, 0, 0)),
        ],
        out_shape=[jax.ShapeDtypeStruct((b, ncp, KV_W), F32)] * 2,
        scratch_shapes=[acc(), acc(), acc(), acc()],
        compiler_params=_cparams(("parallel", "arbitrary")),
        name="compress_rows",
    )(kv, kv, w2k, w2v, lk, lv)


PAGE_ROWS = PAGE_SIZE * N_KV
SUB_ROWS = SUB_PER_PAGE * N_KV


def _cmp_accumulate_rg(page, w2_ref, acc_a, acc_b, row0):
    ab = _dot(w2_ref[...], page.astype(BF16))
    acc_a[pl.ds(row0, SUB_ROWS), :] = ab[0:SUB_ROWS]
    acc_b[pl.ds(row0, SUB_ROWS), :] = ab[SUB_ROWS:2 * SUB_ROWS]


def _cmp_finish_rg(acc_a, acc_b, lin_ref, out_ref, ncp):
    for g in range(N_KV):
        pooled = acc_a[pl.ds(g, ncp, stride=N_KV), :] + acc_b[pl.ds(N_KV + g, ncp, stride=N_KV), :]
        out_ref[0, :, g * HEAD_DIM:(g + 1) * HEAD_DIM] = _dot(pooled.astype(BF16), lin_ref[g])


def _compress_paged_body(pt_ref, *refs, pages, n_pool_steps, ncp):
    pk = refs[0:pages]
    pv = refs[pages:2 * pages]
    nk_ref, nv_ref, w2k_ref, w2v_ref, lk_ref, lv_ref, ok_ref, ov_ref, aak, abk, aav, abv = refs[2 * pages:]
    s = pl.program_id(1)

    @pl.when(s < n_pool_steps)
    def _():
        for i in range(pages):
            row0 = pl.multiple_of((s * pages + i) * SUB_ROWS, SUB_ROWS)
            _cmp_accumulate_rg(pk[i][...], w2k_ref, aak, abk, row0)
            _cmp_accumulate_rg(pv[i][...], w2v_ref, aav, abv, row0)

    @pl.when(s == n_pool_steps)
    def _():
        row0 = n_pool_steps * pages * SUB_ROWS
        _cmp_accumulate_rg(nk_ref[0, 0], w2k_ref, aak, abk, row0)
        _cmp_accumulate_rg(nv_ref[0, 0], w2v_ref, aav, abv, row0)
        _cmp_finish_rg(aak, abk, lk_ref, ok_ref, ncp)
        _cmp_finish_rg(aav, abv, lv_ref, ov_ref, ncp)


def _page_spec(i, pages, n_pool_steps):
    def index(bi, s, pt):
        return (pt[bi, jnp.minimum(s, n_pool_steps - 1) * pages + i], 0)
    return pl.BlockSpec((PAGE_ROWS, HEAD_DIM), index)


def _compress_paged(page_table, pool_k, pool_v, new_pages, w2k, w2v, lk, lv):
    b, n_pages = page_table.shape
    pages = PAGES_PER_STEP
    n_pool_steps = n_pages // pages
    ncp = n_pages * SUB_PER_PAGE
    wspec = lambda shp: pl.BlockSpec(shp, lambda bi, s, pt: (0,) * len(shp))
    newspec = lambda idx: pl.BlockSpec((1, 1, PAGE_ROWS, HEAD_DIM), lambda bi, s, pt: (idx, bi, 0, 0))
    acc = lambda: pltpu.VMEM(((ncp + SUB_PER_PAGE) * N_KV, HEAD_DIM), F32)
    grid_spec = pltpu.PrefetchScalarGridSpec(
        num_scalar_prefetch=1,
        grid=(b, n_pool_steps + 1),
        in_specs=(
            [_page_spec(i, pages, n_pool_steps) for i in range(pages)] * 2
            + [newspec(0), newspec(1), wspec(w2k.shape), wspec(w2v.shape), wspec(lk.shape), wspec(lv.shape)]
        ),
        out_specs=[
            pl.BlockSpec((1, ncp, KV_W), lambda bi, s, pt: (bi, 0, 0)),
            pl.BlockSpec((1, ncp, KV_W), lambda bi, s, pt: (bi, 0, 0)),
        ],
        scratch_shapes=[acc(), acc(), acc(), acc()],
    )
    return pl.pallas_call(
        functools.partial(_compress_paged_body, pages=pages, n_pool_steps=n_pool_steps, ncp=ncp),
        grid_spec=grid_spec,
        out_shape=[jax.ShapeDtypeStruct((b, ncp, KV_W), F32)] * 2,
        compiler_params=_cparams(("parallel", "arbitrary")),
        name="compress_paged",
    )(page_table, *([pool_k] * pages), *([pool_v] * pages), new_pages, new_pages, w2k, w2v, lk, lv)


LOG2E = 1.4426950408889634


def _lane_fold(x, op):
    out = x[:, 0:LANES]
    for u in range(1, x.shape[1] // LANES):
        out = op(out, x[:, u * LANES:(u + 1) * LANES])
    return out


def _nsa_prompt_body(q_ref, z_ref, gt_ref, ck_ref, cv_ref, ks_ref, vs_ref, kw_ref, vw_ref, a_ref,
                     ksb, vsb, kwb, vwb, s_buf, mx_s, l_s, acc_s, *, tq, t, kc, n_cmp):
    ksb[...] = ks_ref[0].astype(BF16)
    vsb[...] = vs_ref[0].astype(BF16)
    kwb[...] = kw_ref[0].astype(BF16)
    vwb[...] = vw_ref[0].astype(BF16)

    n_blk = -(-t // L_SEL)
    ncl = ck_ref.shape[1]
    scale = HEAD_DIM ** -0.5
    c_exp = scale * LOG2E
    hs = [slice(h * HEAD_DIM, (h + 1) * HEAD_DIM) for h in range(HPG)]
    hr = [slice(h * tq, (h + 1) * tq) for h in range(HPG)]
    ckb = ck_ref[0].astype(BF16)
    cvb = cv_ref[0].astype(BF16)
    msel = _block_importance_matrix(ncl, LANES, n_blk)
    ws = min(WINDOW + tq, t)
    blk_shift = int(math.log2(L_SEL))

    def q_tile(qi, carry):
        q0 = pl.multiple_of(qi * tq, tq)
        rows = pl.ds(q0, tq)
        pos = q0 + _iota((tq, 1), 0)
        qb = jnp.concatenate([q_ref[rows, hs[h]] for h in range(HPG)], axis=0).astype(BF16)

        ci = _iota((1, ncl), 1)
        cmask = (ci * S_CMP + (L_CMP - 1) <= pos) & (ci < n_cmp)
        s_c = _dot_nt(qb, ckb) * scale
        p_c = [_softmax_masked(s_c[hr[h]], cmask) for h in range(HPG)]
        imp_c = p_c[0] + p_c[1] + p_c[2] + p_c[3]
        o_c = _dot(jnp.concatenate(p_c, axis=0).astype(BF16), cvb)

        imp = _dot_f32_by_01(imp_c, msel)
        selb = _select_blocks_t(imp, q0 + _iota((1, tq), 1), n_blk).astype(BF16)

        mx_s[...] = jnp.full(mx_s.shape, NEG, F32)

        def pass1(ki, c):
            k0 = pl.multiple_of(ki * kc, kc)
            kpos = k0 + _iota((1, kc), 1)
            blk = lax.shift_right_logical(k0 + _iota((LANES, kc), 1), blk_shift)
            expand = jnp.where(blk == _iota((LANES, kc), 0), 1.0, 0.0).astype(BF16)
            bias = (_dot(selb, expand) - 1.0) * (-NEG)
            bias = jnp.where(kpos <= pos, bias, NEG)
            s = _dot_nt(qb, ksb[pl.ds(k0, kc), :])
            for h in range(HPG):
                sh = s[hr[h]] + bias
                s_buf[ki, hr[h], :] = sh
                mx_s[hr[h], :] = jnp.maximum(mx_s[hr[h], :], _lane_fold(sh, jnp.maximum))
            return c

        n_chunks = (q0 + tq + kc - 1) // kc
        lax.fori_loop(0, n_chunks, pass1, 0)
        m_b = jnp.broadcast_to(jnp.max(mx_s[...], axis=-1, keepdims=True), mx_s.shape)
        mx_s[...] = m_b
        l_s[...] = jnp.zeros(l_s.shape, F32)
        acc_s[...] = jnp.zeros(acc_s.shape, F32)

        def pass2(ki, c):
            k0 = pl.multiple_of(ki * kc, kc)
            mb = mx_s[...]
            e = [jnp.exp2((s_buf[ki, :, u * LANES:(u + 1) * LANES] - mb) * c_exp) for u in range(kc // LANES)]
            tot = e[0]
            for u in range(1, len(e)):
                tot = tot + e[u]
            l_s[...] = l_s[...] + tot
            acc_s[...] = acc_s[...] + _dot(jnp.concatenate(e, axis=1).astype(BF16), vsb[pl.ds(k0, kc), :])
            return c

        lax.fori_loop(0, n_chunks, pass2, 0)
        o_s = acc_s[...] / jnp.sum(l_s[...], axis=-1, keepdims=True)

        w0 = pl.multiple_of(jnp.minimum(jnp.maximum(q0 - WINDOW, 0), t - ws), tq)
        wpos = w0 + _iota((1, ws), 1)
        wbias = jnp.where((wpos <= pos) & (wpos > pos - WINDOW), 0.0, NEG)
        s_w = _dot_nt(qb, kwb[pl.ds(w0, ws), :])
        e_w = []
        l_w = []
        for h in range(HPG):
            sh = s_w[hr[h]] + wbias
            eh = jnp.exp2((sh - jnp.max(sh, axis=-1, keepdims=True)) * c_exp)
            l_w.append(jnp.sum(eh, axis=-1, keepdims=True))
            e_w.append(eh)
        o_w = _dot(jnp.concatenate(e_w, axis=0).astype(BF16), vwb[pl.ds(w0, ws), :]) / jnp.concatenate(l_w, axis=0)

        gates = jax.nn.sigmoid(gt_ref[rows, :])
        for h in range(HPG):
            o = (gates[:, h:h + 1] * o_c[hr[h]] + gates[:, HPG + h:HPG + h + 1] * o_s[hr[h]]
                 + gates[:, 2 * HPG + h:2 * HPG + h + 1] * o_w[hr[h]])
            a_ref[rows, hs[h]] = (o * _silu(z_ref[rows, hs[h]])).astype(BF16)
        return carry

    lax.fori_loop(0, t // tq, q_tile, 0)


def _nsa_prompt_attend(qz, gates, ck, cv, kv, *, b, t, tq, kc):
    gq = HPG * HEAD_DIM
    n_cmp = t // S_CMP - (L_CMP // S_CMP) + 1
    ncl = ck.shape[1]
    kvspec = lambda idx: pl.BlockSpec((1, t, HEAD_DIM), lambda i, g: (idx, i, g))
    kvb = lambda: pltpu.VMEM((t, HEAD_DIM), BF16)
    stat = lambda: pltpu.VMEM((HPG * tq, LANES), F32)
    return pl.pallas_call(
        functools.partial(_nsa_prompt_body, tq=tq, t=t, kc=kc, n_cmp=n_cmp),
        grid=(b, N_KV),
        in_specs=[
            pl.BlockSpec((t, gq), lambda i, g: (i, g)),
            pl.BlockSpec((t, gq), lambda i, g: (i, N_KV + g)),
            pl.BlockSpec((t, LANES), lambda i, g: (i, g)),
            pl.BlockSpec((1, ncl, HEAD_DIM), lambda i, g: (i, 0, g)),
            pl.BlockSpec((1, ncl, HEAD_DIM), lambda i, g: (i, 0, g)),
            kvspec(2), kvspec(3), kvspec(4), kvspec(5),
        ],
        out_specs=pl.BlockSpec((t, gq), lambda i, g: (i, g)),
        out_shape=jax.ShapeDtypeStruct((b * t, N_KV * gq), BF16),
        scratch_shapes=[
            kvb(), kvb(), kvb(), kvb(),
            pltpu.VMEM((t // kc, HPG * tq, kc), F32),
            stat(), stat(), stat(),
        ],
        compiler_params=_cparams(("parallel", "parallel")),
        name="nsa_prompt_attend",
    )(qz, qz, gates, ck, cv, kv, kv, kv, kv)


def _rows_by_group(ref, n):
    return jnp.concatenate([ref[pl.ds(g, n, stride=N_KV), :] for g in range(N_KV)], axis=1)


def _diag_blocks(x, rows_per_group):
    return jnp.concatenate(
        [x[g * rows_per_group:(g + 1) * rows_per_group, g * HEAD_DIM:(g + 1) * HEAD_DIM]
         for g in range(N_KV)], axis=0)


def _pad_rows(x, n):
    return jnp.concatenate([x, jnp.zeros((n - x.shape[0], x.shape[1]), x.dtype)], axis=0)


def _nsa_sample_body(pt_ref, *refs, pages, n_pool_steps, past, n_new, wb):
    sk = refs[0:pages]
    sv = refs[pages:2 * pages]
    (nks_ref, nvs_ref, nkw_ref, nvw_ref, nkw_rg, nvw_rg, ck_ref, cv_ref, wk_ref, wv_ref, q_ref, z_ref, gt_ref,
     a_ref, wko_ref, wvo_ref, q2, selr, m_s, l_s, acc, oc, ow) = refs[2 * pages:]
    s = pl.program_id(1)
    rows = N_HEADS * n_new
    rpg = HPG * n_new
    scale = HEAD_DIM ** -0.5
    n_blk = past // L_SEL + -(-n_new // L_SEL)
    n_cmp = ck_ref.shape[1]
    sel_lanes = selr.shape[1]
    pos = past + (_iota((rows, 1), 0) % n_new)
    new_page = lambda ref: _pad_rows(ref[0], PAGE_SIZE).astype(BF16)
    new_mask = (past + _iota((1, PAGE_SIZE), 1) <= pos) & (_iota((1, PAGE_SIZE), 1) < n_new)

    @pl.when(s == 0)
    def _():
        q2[...] = jnp.zeros(q2.shape, F32)
        for hh in range(N_HEADS):
            g = hh // HPG
            q2[hh * n_new:(hh + 1) * n_new, g * HEAD_DIM:(g + 1) * HEAD_DIM] = (
                q_ref[:, hh * HEAD_DIM:(hh + 1) * HEAD_DIM])
        qb = q2[...].astype(BF16)

        ci = _iota((1, n_cmp), 1)
        cmask = ci * S_CMP + (L_CMP - 1) <= pos
        p_c = _softmax_masked(_dot_nt(qb, ck_ref[0].astype(BF16)) * scale, cmask)
        oc[...] = _diag_blocks(_dot(p_c.astype(BF16), cv_ref[0].astype(BF16)), rpg)

        imp_c = jnp.concatenate(
            [sum(p_c[(g * HPG + h) * n_new:(g * HPG + h + 1) * n_new, :] for h in range(HPG))
             for g in range(N_KV)], axis=0)
        imp = _dot_f32_by_01(imp_c, _block_importance_matrix(n_cmp, sel_lanes, n_blk))
        pos_gt = past + (_iota((N_KV * n_new, 1), 0) % n_new)
        sel = _select_blocks(imp, pos_gt, n_blk)
        selr[...] = jnp.concatenate(
            [sel[(hh // HPG) * n_new:(hh // HPG + 1) * n_new, :] for hh in range(N_HEADS)], axis=0)

        wpos = (past - wb) + _iota((1, wb), 1)
        mask = jnp.concatenate([(wpos <= pos) & (wpos > pos - WINDOW), new_mask], axis=1)
        wk = _rows_by_group(wk_ref.at[0], wb).astype(BF16)
        wv = _rows_by_group(wv_ref.at[0], wb).astype(BF16)
        sc = jnp.concatenate([_dot_nt(qb, wk), _dot_nt(qb, new_page(nkw_ref))], axis=1) * scale
        p_w = _softmax_masked(sc, mask)
        ow[...] = _diag_blocks(
            _dot(p_w[:, 0:wb].astype(BF16), wv) + _dot(p_w[:, wb:].astype(BF16), new_page(nvw_ref)), rpg)

        keep = (wb - n_new) * N_KV
        wko_ref[0, 0:keep, :] = wk_ref[0, n_new * N_KV:wb * N_KV, :]
        wko_ref[0, keep:wb * N_KV, :] = nkw_rg[...]
        wvo_ref[0, 0:keep, :] = wv_ref[0, n_new * N_KV:wb * N_KV, :]
        wvo_ref[0, keep:wb * N_KV, :] = nvw_rg[...]

        m_s[...] = jnp.full(m_s.shape, NEG, F32)
        l_s[...] = jnp.zeros(l_s.shape, F32)
        acc[...] = jnp.zeros(acc.shape, F32)

    @pl.when(s < n_pool_steps)
    def _():
        qb = q2[...].astype(BF16)
        nk = pages * PAGE_SIZE
        sc = jnp.concatenate(
            [_dot_nt(qb, _rows_by_group(sk[i], PAGE_SIZE).astype(BF16)) for i in range(pages)], axis=1) * scale
        blk = s * (nk // L_SEL) + lax.shift_right_logical(_iota((sel_lanes, nk), 1), int(math.log2(L_SEL)))
        expand = jnp.where(blk == _iota((sel_lanes, nk), 0), 1.0, 0.0).astype(BF16)
        mask = _dot(selr[...].astype(BF16), expand) > 0.5

        def pv(p):
            return sum(_dot(p[:, i * PAGE_SIZE:(i + 1) * PAGE_SIZE].astype(BF16),
                            _rows_by_group(sv[i], PAGE_SIZE).astype(BF16)) for i in range(pages))

        _online_update(m_s, l_s, acc, sc, mask, pv)

    @pl.when(s == n_pool_steps)
    def _():
        qb = q2[...].astype(BF16)
        new_blk = past // L_SEL
        mask = (selr[:, new_blk:new_blk + 1] > 0.5) & new_mask
        _online_update(m_s, l_s, acc, _dot_nt(qb, new_page(nks_ref)) * scale, mask,
                       lambda p: _dot(p.astype(BF16), new_page(nvs_ref)))
        o_s = _diag_blocks(_online_finish(l_s, acc), rpg)

        gates = jax.nn.sigmoid(gt_ref[...])

        def gate_col(br):
            return jnp.concatenate(
                [gates[:, (hh // HPG) * LANES + br * HPG + hh % HPG:(hh // HPG) * LANES + br * HPG + hh % HPG + 1]
                 for hh in range(N_HEADS)], axis=0)

        zr = jnp.concatenate([z_ref[:, hh * HEAD_DIM:(hh + 1) * HEAD_DIM] for hh in range(N_HEADS)], axis=0)
        o = gate_col(0) * oc[...] + gate_col(1) * o_s + gate_col(2) * ow[...]
        res = o * _silu(zr)
        for hh in range(N_HEADS):
            a_ref[:, hh * HEAD_DIM:(hh + 1) * HEAD_DIM] = res[hh * n_new:(hh + 1) * n_new, :]


def _nsa_sample_attend(page_table, pool_k, pool_v, kv, kw_rg, vw_rg, ck, cv, win_k, win_v, qz, gates, *, n_new):
    b, n_pages = page_table.shape
    pages = PAGES_PER_STEP
    n_pool_steps = n_pages // pages
    past = n_pages * PAGE_SIZE
    wb = win_k.shape[1] // N_KV
    rows = N_HEADS * n_new
    qw = N_HEADS * HEAD_DIM
    n_blk = past // L_SEL + -(-n_new // L_SEL)
    sel_lanes = -(-n_blk // LANES) * LANES
    newspec = lambda idx: pl.BlockSpec((1, n_new, KV_W), lambda bi, s, pt: (idx, bi, 0))
    newspec_rg = lambda: pl.BlockSpec((n_new * N_KV, HEAD_DIM), lambda bi, s, pt: (bi, 0))
    full = lambda arr: pl.BlockSpec((1,) + arr.shape[1:], lambda bi, s, pt: (bi, 0, 0))
    grid_spec = pltpu.PrefetchScalarGridSpec(
        num_scalar_prefetch=1,
        grid=(b, n_pool_steps + 1),
        in_specs=(
            [_page_spec(i, pages, n_pool_steps) for i in range(pages)] * 2
            + [newspec(2), newspec(3), newspec(4), newspec(5), newspec_rg(), newspec_rg(),
               full(ck), full(cv), full(win_k), full(win_v),
               pl.BlockSpec((n_new, qw), lambda bi, s, pt: (bi, 0)),
               pl.BlockSpec((n_new, qw), lambda bi, s, pt: (bi, 1)),
               pl.BlockSpec((n_new, N_KV * LANES), lambda bi, s, pt: (bi, 0))]
        ),
        out_specs=[
            pl.BlockSpec((n_new, qw), lambda bi, s, pt: (bi, 0)),
            full(win_k), full(win_v),
        ],
        scratch_shapes=[
            pltpu.VMEM((rows, KV_W), F32),
            pltpu.VMEM((rows, sel_lanes), F32),
            pltpu.VMEM((rows, 1), F32),
            pltpu.VMEM((rows, 1), F32),
            pltpu.VMEM((rows, KV_W), F32),
            pltpu.VMEM((rows, HEAD_DIM), F32),
            pltpu.VMEM((rows, HEAD_DIM), F32),
        ],
    )
    body = functools.partial(_nsa_sample_body, pages=pages, n_pool_steps=n_pool_steps, past=past,
                             n_new=n_new, wb=wb)

    return pl.pallas_call(
        body,
        grid_spec=grid_spec,
        out_shape=[
            jax.ShapeDtypeStruct((b * n_new, qw), F32),
            jax.ShapeDtypeStruct(win_k.shape, F32),
            jax.ShapeDtypeStruct(win_v.shape, F32),
        ],
        compiler_params=_cparams(("parallel", "arbitrary")),
        name="nsa_sample_attend",
    )(page_table, *([pool_k] * pages), *([pool_v] * pages), kv, kv, kv, kv, kw_rg, vw_rg,
      ck, cv, win_k, win_v, qz, qz, gates)


def _rope_tables(pos, dk):
    half = dk // 2
    inv = 1.0 / (ROPE_BASE ** jnp.linspace(0.0, 1.0, half))
    ang = pos.astype(F32)[:, None] * inv[None, :]
    cos = jnp.repeat(jnp.cos(ang), 2, axis=1)
    sin = jnp.stack([-jnp.sin(ang), jnp.sin(ang)], axis=-1).reshape(pos.shape[0], dk)
    return cos, sin


def _pooling_matrix(w_pos):
    eye = jnp.eye(SUB_PER_PAGE, dtype=F32)
    first = jnp.einsum("mn,lg->gmnl", eye, w_pos[:S_CMP]).reshape(N_KV, SUB_PER_PAGE, PAGE_SIZE)
    second = jnp.einsum("mn,lg->gmnl", eye, w_pos[S_CMP:]).reshape(N_KV, SUB_PER_PAGE, PAGE_SIZE)
    return jnp.concatenate([first, second], axis=1).astype(BF16)


def _pooling_matrix_rg(w_pos):
    band = _pooling_matrix(w_pos).astype(F32).reshape(N_KV, 2, SUB_PER_PAGE, PAGE_SIZE)
    full = jnp.einsum("ghmr,gk->hmgrk", band, jnp.eye(N_KV, dtype=F32))
    return full.reshape(2 * SUB_ROWS, PAGE_ROWS).astype(BF16)


def _split_nsa_w_in(w):
    qw = N_HEADS * HEAD_DIM
    ng = 3 * N_HEADS
    d = w.shape[0]
    w_qz = w[:, :2 * qw]
    w_g = w[:, 2 * qw:2 * qw + ng].reshape(d, 3, N_KV, HPG).transpose(0, 2, 1, 3).reshape(d, N_KV, 3 * HPG)
    w_g = jnp.pad(w_g, ((0, 0), (0, 0), (0, LANES - 3 * HPG))).reshape(d, N_KV * LANES)
    w_kv = w[:, 2 * qw + ng:]
    return w_qz.astype(BF16), w_g.astype(BF16), w_kv.astype(BF16)


def _tile_rows(m, pref):
    return pref if m % pref == 0 else m


def _run_group(x, *, pos0, pool_buf, ret_s0, conv_buf, lru_h0, nsa_cache, params):
    nb, t, d = x.shape
    m = nb * t
    tm = _tile_rows(m, OUT_ROWS)
    tm_in = _tile_rows(m, IN_ROWS)
    fused_layout = t % tm == 0 and t % tm_in == 0
    nt = t // tm if fused_layout else 1
    nt_in = t // tm_in if fused_layout else 1
    bm_in = lambda i, j: (i, 0)
    bm_io = lambda i: (i, 0)
    tb_in = lambda i, j: (i % nt_in, i // nt_in)
    tb_io = lambda i: (i % nt, i // nt)
    pos = pos0 + jnp.arange(t, dtype=jnp.int32)
    vec = lambda v: v.reshape(1, -1)

    depth = params["norm_pre"].shape[0]
    cur = x.reshape(m, d)
    cur_is_tm = False
    states = {k: [] for k in ("pool", "ret", "conv", "lru", "nsa")}

    def project(xv, x_in, g_pre, w):
        n = w.shape[1]
        tn = IN_COLS if n % IN_COLS == 0 else n
        return _norm_matmul(xv, x_in, g_pre, w, m=m, tm=tm_in, tn=tn, out_shape=(m, n),
                            out_block=(tm_in, tn), out_map=lambda a, b_: (a, b_))

    def to_bm_view(a):
        return (a.reshape(t, nb * d), tb_in, tb_io) if fused_layout else (a, bm_in, bm_io)

    for i in range(depth):
        kind, j = i % 4, i // 4
        g_pre = vec(params["norm_pre"][i])
        g_post = vec(params["norm_post"][i])
        if kind == 0:
            w_in = params["pool_w_in"][j].astype(BF16)
            e = w_in.shape[1] // 2
            uz = project(cur, bm_in, g_pre, w_in)
            buf16 = jnp.pad(pool_buf[j], ((0, 0), (POOL_HALO - pool_buf[j].shape[1], 0), (0, 0)))
            a, st = _pool_mixer(uz, buf16, params["pool_w_grp"][j].astype(BF16), vec(params["pool_b_grp"][j]),
                                vec(params["pool_scale"][j]), b=nb, t=t, tt=_tile_rows(t, 256), pos0=pos0)
            states["pool"].append(st[:, 1:, :])
            cur = _out_proj(a, params["pool_w_out"][j].astype(BF16), cur, bm_io, g_post, m=m, tm=tm,
                            out_shape=(m, d), out_map=bm_io)
        elif kind == 1:
            w_in = params["ret_w_in"][j].astype(BF16)
            dk = d // R_HEADS
            dv = 2 * dk
            proj = project(cur, bm_in, g_pre, w_in)
            cos_t, sin_t = _rope_tables(pos, dk)
            lg = jnp.log(1.0 - 2.0 ** (-5.0 - jnp.arange(R_HEADS, dtype=F32)))
            a, s_new = _retention_mixer(proj, cos_t, sin_t, lg, ret_s0[j], b=nb, t=t, dk=dk, dv=dv)
            states["ret"].append(s_new)
            w_out = params["ret_w_out"][j].astype(BF16)
            if fused_layout:
                cur = _out_proj(a, w_out, cur, bm_io, g_post, m=m, tm=tm,
                                out_shape=(t, nb * d), out_map=tb_io).reshape(m, d)
            else:
                cur = _out_proj(a, w_out, cur, bm_io, g_post, m=m, tm=tm, out_shape=(m, d), out_map=bm_io)
                cur = cur.reshape(nb, t, d).swapaxes(0, 1).reshape(m, d)
            cur_is_tm = True
        elif kind == 2:
            assert cur_is_tm
            w_in = params["lru_w_in"][j].astype(BF16)
            d_rnn = w_in.shape[1] // 2
            uz = project(cur, bm_in, g_pre, w_in)
            buf = conv_buf[j].swapaxes(0, 1).reshape((CONV_W - 1) * nb, d_rnn)
            tt = max(1, min(t, 512 // nb))
            a, cs_new, h_new = _lru_mixer(
                uz, buf, params["lru_conv_w"][j], vec(params["lru_conv_b"][j]),
                params["lru_w_a"][j].astype(BF16), vec(params["lru_b_a"][j]),
                params["lru_w_x"][j].astype(BF16), vec(params["lru_b_x"][j]),
                vec(params["lru_lam"][j]), lru_h0[j], nb=nb, t=t, tt=tt)
            states["conv"].append(cs_new.reshape(CONV_W - 1, nb, d_rnn).swapaxes(0, 1))
            states["lru"].append(h_new)
            cur = _out_proj(a, params["lru_w_out"][j].astype(BF16), cur, bm_io, g_post, m=m, tm=tm,
                            out_shape=(m, d), out_map=bm_io)
        else:
            if cur_is_tm and not fused_layout:
                cur = cur.reshape(t, nb, d).swapaxes(0, 1).reshape(m, d)
                cur_is_tm = False
            xv, x_in, x_io = to_bm_view(cur) if cur_is_tm else (cur, bm_in, bm_io)
            w_qz, w_g, w_kv = _split_nsa_w_in(params["nsa_w_in"][j])
            qz = project(xv, x_in, g_pre, w_qz)
            gates = project(xv, x_in, g_pre, w_g)
            n_kv_t = w_kv.shape[1] // KV_W
            kv, kv_rg = _norm_matmul_kv(xv, x_in, g_pre, w_kv, m=m, tm=tm_in)
            pos_k, pos_v = params["nsa_cmp_pos_k"][j], params["nsa_cmp_pos_v"][j]
            lin_k, lin_v = params["nsa_cmp_lin_k"][j], params["nsa_cmp_lin_v"][j]
            kv5 = [a_.reshape(nb, t, N_KV, HEAD_DIM) for a_ in kv_rg]
            if nsa_cache is None:
                ck, cv = _compress_rows(kv, _pooling_matrix(pos_k), _pooling_matrix(pos_v),
                                        lin_k.astype(BF16), lin_v.astype(BF16), b=nb, t=t)
                a = _nsa_prompt_attend(qz, gates, ck, cv, kv, b=nb, t=t, tq=NSA_TQ, kc=NSA_KC)
                wb = min(WINDOW, t)
                new_state = (kv5[0], kv5[1], kv5[2], kv5[3], kv5[4][:, t - wb:], kv5[5][:, t - wb:])
            else:
                page_table, cmp_k, cmp_v, sel_k, sel_v, win_k, win_v = nsa_cache
                pool2d = lambda c: c[j].reshape(c.shape[1] * PAGE_ROWS, HEAD_DIM)
                win2d = lambda c: c[j].reshape(c.shape[1], c.shape[2] * N_KV, HEAD_DIM)
                new_pages = jnp.pad(jnp.stack(kv_rg[0:2]).reshape(2, nb, t * N_KV, HEAD_DIM),
                                    ((0, 0), (0, 0), (0, PAGE_ROWS - t * N_KV), (0, 0)))
                ck, cv = _compress_paged(page_table, pool2d(cmp_k), pool2d(cmp_v), new_pages,
                                         _pooling_matrix_rg(pos_k), _pooling_matrix_rg(pos_v),
                                         lin_k.astype(BF16), lin_v.astype(BF16))
                a, wk_new, wv_new = _nsa_sample_attend(page_table, pool2d(sel_k), pool2d(sel_v), kv, kv_rg[4], kv_rg[5], ck, cv,
                                                       win2d(win_k), win2d(win_v), qz, gates, n_new=t)
                wshape = (nb, win_k.shape[2], N_KV, HEAD_DIM)
                new_state = (kv5[0], kv5[1], kv5[2], kv5[3], wk_new.reshape(wshape), wv_new.reshape(wshape))
            states["nsa"].append(new_state)
            cur = _out_proj(a, params["nsa_w_out"][j].astype(BF16), xv, x_io, g_post, m=m, tm=tm,
                            out_shape=(m, d), out_map=bm_io)
            cur_is_tm = False
    if cur_is_tm:
        cur = cur.reshape(t, nb, d).swapaxes(0, 1).reshape(m, d)
    return cur.reshape(nb, t, d), states


def kernel(x_prompt, x_sample, state_pool, state_ret, state_conv, state_lru, cache_cmp_k, cache_cmp_v, cache_sel_k, cache_sel_v, cache_win_k, cache_win_v, page_table, norm_pre, norm_post, pool_w_in, pool_w_grp, pool_b_grp, pool_scale, pool_w_out, ret_w_in, ret_w_out, lru_w_in, lru_conv_w, lru_conv_b, lru_w_a, lru_b_a, lru_w_x, lru_b_x, lru_lam, lru_w_out, nsa_w_in, nsa_cmp_pos_k, nsa_cmp_lin_k, nsa_cmp_pos_v, nsa_cmp_lin_v, nsa_w_out):
    params = dict(
        norm_pre=norm_pre, norm_post=norm_post, pool_w_in=pool_w_in, pool_w_grp=pool_w_grp, pool_b_grp=pool_b_grp,
        pool_scale=pool_scale, pool_w_out=pool_w_out, ret_w_in=ret_w_in, ret_w_out=ret_w_out, lru_w_in=lru_w_in,
        lru_conv_w=lru_conv_w, lru_conv_b=lru_conv_b, lru_w_a=lru_w_a, lru_b_a=lru_b_a, lru_w_x=lru_w_x,
        lru_b_x=lru_b_x, lru_lam=lru_lam, lru_w_out=lru_w_out, nsa_w_in=nsa_w_in, nsa_cmp_pos_k=nsa_cmp_pos_k,
        nsa_cmp_lin_k=nsa_cmp_lin_k, nsa_cmp_pos_v=nsa_cmp_pos_v, nsa_cmp_lin_v=nsa_cmp_lin_v, nsa_w_out=nsa_w_out)
    b = x_prompt.shape[0]
    past = page_table.shape[1] * PAGE_SIZE
    zeros_like_state = lambda s: jnp.zeros((s.shape[0], b) + s.shape[2:], F32)
    yp, sp = _run_group(
        x_prompt, pos0=0, pool_buf=zeros_like_state(state_pool), ret_s0=zeros_like_state(state_ret),
        conv_buf=zeros_like_state(state_conv), lru_h0=zeros_like_state(state_lru), nsa_cache=None, params=params)
    ys, ss = _run_group(
        x_sample, pos0=past, pool_buf=state_pool, ret_s0=state_ret, conv_buf=state_conv, lru_h0=state_lru,
        nsa_cache=(page_table, cache_cmp_k, cache_cmp_v, cache_sel_k, cache_sel_v, cache_win_k, cache_win_v),
        params=params)
    out = [yp, ys]
    for key in ("pool", "ret", "conv", "lru"):
        out += [jnp.stack(sp[key]), jnp.stack(ss[key])]
    for idx in range(6):
        out += [jnp.stack([e[idx] for e in sp["nsa"]]), jnp.stack([e[idx] for e in ss["nsa"]])]
    return tuple(out)
```

```python
import functools
import math

import jax
import jax.numpy as jnp
from jax import lax
from jax.experimental import pallas as pl
from jax.experimental.pallas import tpu as pltpu

F32 = jnp.float32
BF16 = jnp.bfloat16

EPS = 1e-6
NEG = -1e30
FORCE = 1e6

PAGE_SIZE = 128
POOL_WINDOWS = (2, 4, 8, 16)
R_HEADS = 4
R_CHUNK = 128
ROPE_BASE = 10000.0
LRU_BW = 256
CONV_W = 4
LRU_C = 8.0
N_HEADS = 16
N_KV = 4
HPG = N_HEADS // N_KV
HEAD_DIM = 128
L_CMP = 32
S_CMP = 16
L_SEL = 64
N_SEL = 16
WINDOW = 512
KV_W = N_KV * HEAD_DIM
SUB_PER_PAGE = PAGE_SIZE // S_CMP
PAGES_PER_STEP = 16
LRU_ROWS = 1024

LANES = 128
SUBLANES = 8
VMEM_LIMIT = 56 * 1024 * 1024
IN_ROWS, IN_COLS = 1024, 1024
OUT_ROWS = 512
NSA_TQ, NSA_KC = 256, 512


def _cparams(sem):
    return pltpu.CompilerParams(dimension_semantics=sem, vmem_limit_bytes=VMEM_LIMIT)


def _act_dtype(rows):
    return BF16 if rows % (2 * SUBLANES) == 0 else F32


def _dot(a, b):
    return jnp.dot(a, b, preferred_element_type=F32)


def _dot_nt(a, b):
    return lax.dot_general(a, b, (((1,), (1,)), ((), ())), preferred_element_type=F32)


def _dot_tn(a, b):
    return lax.dot_general(a, b, (((0,), (0,)), ((), ())), preferred_element_type=F32)


def _silu(x):
    return x * jax.nn.sigmoid(x)


def _iota(shape, dim):
    return lax.broadcasted_iota(jnp.int32, shape, dim)


def _norm_mm_body(x_ref, g_ref, w_ref, o_ref, h_ref):
    @pl.when(pl.program_id(1) == 0)
    def _():
        x = x_ref[...]
        ms = jnp.mean(x * x, axis=-1, keepdims=True)
        h_ref[...] = (x * lax.rsqrt(ms + EPS) * g_ref[...]).astype(BF16)

    o_ref[...] = _dot(h_ref[...], w_ref[...]).reshape(o_ref.shape)


def _norm_matmul(x, x_map, g, w, *, m, tm, tn, out_shape, out_block, out_map):
    d, n = w.shape
    return pl.pallas_call(
        _norm_mm_body,
        grid=(m // tm, n // tn),
        in_specs=[
            pl.BlockSpec((tm, d), x_map),
            pl.BlockSpec((1, d), lambda i, j: (0, 0)),
            pl.BlockSpec((d, tn), lambda i, j: (0, j)),
        ],
        out_specs=pl.BlockSpec(out_block, out_map),
        out_shape=jax.ShapeDtypeStruct(out_shape, F32),
        scratch_shapes=[pltpu.VMEM((tm, d), BF16)],
        compiler_params=_cparams(("parallel", "arbitrary")),
        name="norm_matmul",
    )(x, g, w)


def _norm_mm_kv_body(x_ref, g_ref, w_ref, o_ref, *rest):
    oi_refs, h_ref = rest[:-1], rest[-1]
    j = pl.program_id(1)

    @pl.when(j == 0)
    def _():
        x = x_ref[...]
        ms = jnp.mean(x * x, axis=-1, keepdims=True)
        h_ref[...] = (x * lax.rsqrt(ms + EPS) * g_ref[...]).astype(BF16)

    r = _dot(h_ref[...], w_ref[...])
    o_ref[0] = r
    for jj, oi_ref in enumerate(oi_refs):
        @pl.when(j == jj)
        def _():
            for g in range(N_KV):
                oi_ref[pl.ds(g, r.shape[0], stride=N_KV), :] = r[:, g * HEAD_DIM:(g + 1) * HEAD_DIM]


def _norm_matmul_kv(x, x_map, g, w, *, m, tm):
    d, n = w.shape
    n_t = n // KV_W
    outs = pl.pallas_call(
        _norm_mm_kv_body,
        grid=(m // tm, n_t),
        in_specs=[
            pl.BlockSpec((tm, d), x_map),
            pl.BlockSpec((1, d), lambda i, j: (0, 0)),
            pl.BlockSpec((d, KV_W), lambda i, j: (0, j)),
        ],
        out_specs=[pl.BlockSpec((1, tm, KV_W), lambda i, j: (j, i, 0))]
        + [pl.BlockSpec((tm * N_KV, HEAD_DIM), lambda i, j: (i, 0)) for _ in range(n_t)],
        out_shape=[jax.ShapeDtypeStruct((n_t, m, KV_W), F32)]
        + [jax.ShapeDtypeStruct((m * N_KV, HEAD_DIM), F32) for _ in range(n_t)],
        scratch_shapes=[pltpu.VMEM((tm, d), BF16)],
        compiler_params=_cparams(("parallel", "arbitrary")),
        name="norm_matmul_kv",
    )(x, g, w)
    return outs[0], outs[1:]


def _out_proj_body(a_ref, w_ref, x_ref, g_ref, o_ref):
    y = _dot(a_ref[...].astype(BF16), w_ref[...])
    ms = jnp.mean(y * y, axis=-1, keepdims=True)
    o_ref[...] = x_ref[...] + y * lax.rsqrt(ms + EPS) * g_ref[...]


def _out_proj(a, w, x, x_map, g, *, m, tm, out_shape, out_map):
    e, d = w.shape
    return pl.pallas_call(
        _out_proj_body,
        grid=(m // tm,),
        in_specs=[
            pl.BlockSpec((tm, e), lambda i: (i, 0)),
            pl.BlockSpec((e, d), lambda i: (0, 0)),
            pl.BlockSpec((tm, d), x_map),
            pl.BlockSpec((1, d), lambda i: (0, 0)),
        ],
        out_specs=pl.BlockSpec((tm, d), out_map),
        out_shape=jax.ShapeDtypeStruct(out_shape, F32),
        compiler_params=_cparams(("parallel",)),
        name="out_proj",
    )(a, w, x, g)


POOL_HALO = 16
POOL_PAD = 8


def _pool_body(u_ref, z_ref, buf_ref, wg_ref, bg_ref, sc_ref, a_ref, st_ref, ext, sa, sb,
               *, tt, pos0, gw):
    ti = pl.program_id(1)
    lo = POOL_PAD
    top = POOL_PAD + POOL_HALO
    rows = top + tt

    @pl.when(ti == 0)
    def _():
        ext[0:lo, :] = jnp.zeros((lo, ext.shape[1]), F32)
        ext[lo:top, :] = buf_ref[0]

    @pl.when(ti > 0)
    def _():
        ext[lo:top, :] = ext[lo + tt:top + tt, :]

    u = u_ref[...]
    ext[top:rows, :] = u
    sa[0:lo, :] = jnp.zeros((lo, gw), F32)
    sb[0:lo, :] = jnp.zeros((lo, gw), F32)
    pos = _iota((tt, 1), 0) + (pos0 + ti * tt)
    for g, w in enumerate(POOL_WINDOWS):
        cs = slice(g * gw, (g + 1) * gw)
        sa[lo:rows, :] = ext[lo:rows, cs] + ext[lo - 1:rows - 1, cs]
        cur, oth, sh = sa, sb, 2
        while sh < w:
            oth[lo:rows, :] = cur[lo:rows, :] + cur[lo - sh:rows - sh, :]
            cur, oth, sh = oth, cur, sh * 2
        cnt = jnp.minimum(pos + 1, w).astype(F32)
        mean = cur[top:rows, :] / cnt
        mixed = _dot((mean - u[:, cs]).astype(BF16), wg_ref[g]) + bg_ref[:, cs]
        a_ref[:, cs] = (mixed * sc_ref[:, cs] * _silu(z_ref[:, cs])).astype(a_ref.dtype)
    st_ref[0] = ext[lo + tt:top + tt, :]


def _pool_mixer(uz, buf16, wg, bg, sc, *, b, t, tt, pos0):
    e = buf16.shape[-1]
    gw = e // len(POOL_WINDOWS)
    nt = t // tt
    rows = POOL_PAD + POOL_HALO + tt
    return pl.pallas_call(
        functools.partial(_pool_body, tt=tt, pos0=pos0, gw=gw),
        grid=(b, nt),
        in_specs=[
            pl.BlockSpec((tt, e), lambda i, j: (i * nt + j, 0)),
            pl.BlockSpec((tt, e), lambda i, j: (i * nt + j, 1)),
            pl.BlockSpec((1, POOL_HALO, e), lambda i, j: (i, 0, 0)),
            pl.BlockSpec((len(POOL_WINDOWS), gw, gw), lambda i, j: (0, 0, 0)),
            pl.BlockSpec((1, e), lambda i, j: (0, 0)),
            pl.BlockSpec((1, e), lambda i, j: (0, 0)),
        ],
        out_specs=[
            pl.BlockSpec((tt, e), lambda i, j: (i * nt + j, 0)),
            pl.BlockSpec((1, POOL_HALO, e), lambda i, j: (i, 0, 0)),
        ],
        out_shape=[
            jax.ShapeDtypeStruct((b * t, e), _act_dtype(tt)),
            jax.ShapeDtypeStruct((b, POOL_HALO, e), F32),
        ],
        scratch_shapes=[
            pltpu.VMEM((rows, e), F32),
            pltpu.VMEM((rows, gw), F32),
            pltpu.VMEM((rows, gw), F32),
        ],
        compiler_params=_cparams(("parallel", "arbitrary")),
        name="pool_mixer",
    )(uz, uz, buf16, wg, bg, sc)


def _ret_body(lg_ref, q_ref, k_ref, v_ref, g_ref, cos_ref, sin_ref, s0_ref, a_ref, so_ref, s_scr,
              *, c_len, c_pad, dk, dv):
    c = pl.program_id(1)

    @pl.when(c == 0)
    def _():
        s_scr[...] = s0_ref[0]

    cosv = cos_ref[...]
    sinv = sin_ref[...]
    even = (_iota((c_len, dk), 1) & 1) == 0

    def rot(x):
        nxt = pltpu.roll(x, dk - 1, axis=1)
        prv = pltpu.roll(x, 1, axis=1)
        return x * cosv + jnp.where(even, nxt, prv) * sinv

    def pad(x):
        if c_pad == c_len:
            return x
        return jnp.concatenate([x, jnp.zeros((c_pad - c_len, x.shape[1]), x.dtype)], axis=0)

    i_col = _iota((c_pad, 1), 0).astype(F32)
    j_row = _iota((1, c_pad), 1).astype(F32)
    diff = i_col - j_row
    for h in range(R_HEADS):
        ks = slice(h * dk, (h + 1) * dk)
        vs = slice(h * dv, (h + 1) * dv)
        lg = lg_ref[h]
        q = pad(rot(q_ref[:, ks]))
        k = pad(rot(k_ref[:, ks]) * (dk ** -0.5))
        v = pad(v_ref[:, vs])
        intra = jnp.where(diff >= 0, jnp.exp(jnp.maximum(diff, 0.0) * lg), 0.0)
        q_dec = jnp.exp((i_col + 1.0) * lg)
        k_dec = jnp.exp((c_len - 1.0 - i_col) * lg)
        c_dec = jnp.exp(jnp.full((1, 1), float(c_len), F32) * lg)

        qb = q.astype(BF16)
        vb = v.astype(BF16)
        s = _dot_nt(qb, k.astype(BF16)) * intra
        s_old = s_scr[h]
        o = _dot(s.astype(BF16), vb) + _dot(qb, s_old.astype(BF16)) * q_dec
        s_scr[h] = s_old * c_dec + _dot_tn((k * k_dec).astype(BF16), vb)
        o = o[0:c_len]
        o = o * lax.rsqrt(jnp.mean(o * o, axis=-1, keepdims=True) + EPS)
        a_ref[:, vs] = (o * _silu(g_ref[:, vs])).astype(a_ref.dtype)

    @pl.when(c == pl.num_programs(1) - 1)
    def _():
        so_ref[0] = s_scr[...]


def _retention_mixer(proj, cos_t, sin_t, lg, s0, *, b, t, dk, dv):
    c_len = R_CHUNK if t % R_CHUNK == 0 else t
    c_pad = max(c_len, R_CHUNK)
    nc = t // c_len
    nh = R_HEADS
    qk_w, vg_w = nh * dk, nh * dv
    row = lambda i, c: i * nc + c
    return pl.pallas_call(
        functools.partial(_ret_body, c_len=c_len, c_pad=c_pad, dk=dk, dv=dv),
        grid=(b, nc),
        in_specs=[
            pl.BlockSpec(memory_space=pltpu.SMEM),
            pl.BlockSpec((c_len, qk_w), lambda i, c: (row(i, c), 0)),
            pl.BlockSpec((c_len, qk_w), lambda i, c: (row(i, c), 1)),
            pl.BlockSpec((c_len, vg_w), lambda i, c: (row(i, c), 2 * qk_w // vg_w)),
            pl.BlockSpec((c_len, vg_w), lambda i, c: (row(i, c), 2 * qk_w // vg_w + 1)),
            pl.BlockSpec((c_len, dk), lambda i, c: (c, 0)),
            pl.BlockSpec((c_len, dk), lambda i, c: (c, 0)),
            pl.BlockSpec((1, nh, dk, dv), lambda i, c: (i, 0, 0, 0)),
        ],
        out_specs=[
            pl.BlockSpec((c_len, vg_w), lambda i, c: (row(i, c), 0)),
            pl.BlockSpec((1, nh, dk, dv), lambda i, c: (i, 0, 0, 0)),
        ],
        out_shape=[
            jax.ShapeDtypeStruct((b * t, vg_w), _act_dtype(c_len)),
            jax.ShapeDtypeStruct((b, nh, dk, dv), F32),
        ],
        scratch_shapes=[pltpu.VMEM((nh, dk, dv), F32)],
        compiler_params=_cparams(("parallel", "arbitrary")),
        name="retention_mixer",
    )(lg, proj, proj, proj, proj, cos_t, sin_t, s0)


def _lru_body(u_ref, z_ref, buf_ref, cw_ref, cb_ref, wa_ref, ba_ref, wx_ref, bx_ref, lam_ref, h0_ref,
              a_ref, cso_ref, ho_ref, ext, a_s, b_s, h_s, hc, *, tt, nb):
    ti = pl.program_id(1)
    halo = (CONV_W - 1) * nb
    rows = tt * nb

    @pl.when(ti == 0)
    def _():
        ext[0:halo, :] = buf_ref[...]
        hc[...] = h0_ref[...]

    @pl.when(ti > 0)
    def _():
        ext[0:halo, :] = ext[rows:rows + halo, :]

    ext[halo:halo + rows, :] = u_ref[...]
    cw = cw_ref[...]
    c = cb_ref[...] + ext[0:rows, :] * cw[0:1, :]
    for kk in range(1, CONV_W):
        c = c + ext[kk * nb:kk * nb + rows, :] * cw[kk:kk + 1, :]
    cb16 = c.astype(BF16)
    r = jax.nn.sigmoid(_dot(cb16, wa_ref[0]) + ba_ref[...])
    ig = jax.nn.sigmoid(_dot(cb16, wx_ref[0]) + bx_ref[...])
    nl = -lam_ref[...]
    softplus = jnp.maximum(nl, 0.0) + jnp.log1p(jnp.exp(-jnp.abs(nl)))
    log_a = (-LRU_C) * r * softplus
    a_s[...] = jnp.exp(log_a)
    b_s[...] = jnp.sqrt(1.0 - jnp.exp(2.0 * log_a)) * (ig * c)

    def step(t, h):
        rs = pl.ds(pl.multiple_of(t * nb, SUBLANES), nb)
        h = a_s[rs, :] * h + b_s[rs, :]
        h_s[rs, :] = h
        return h

    hc[...] = lax.fori_loop(0, tt, step, hc[...], unroll=8)
    a_ref[...] = (h_s[...] * _silu(z_ref[...])).astype(BF16)
    cso_ref[...] = ext[rows:rows + halo, :]
    ho_ref[...] = hc[...]


def _lru_mixer(uz, buf, cw, cb, wa, ba, wx, bx, lam, h0, *, nb, t, tt):
    d_rnn = h0.shape[-1]
    nblk = d_rnn // LRU_BW
    nt = t // tt
    rows = tt * nb
    halo = (CONV_W - 1) * nb
    vec = lambda: pl.BlockSpec((1, LRU_BW), lambda j, i: (0, j))
    return pl.pallas_call(
        functools.partial(_lru_body, tt=tt, nb=nb),
        grid=(nblk, nt),
        in_specs=[
            pl.BlockSpec((rows, LRU_BW), lambda j, i: (i, j)),
            pl.BlockSpec((rows, LRU_BW), lambda j, i: (i, nblk + j)),
            pl.BlockSpec((halo, LRU_BW), lambda j, i: (0, j)),
            pl.BlockSpec((CONV_W, LRU_BW), lambda j, i: (0, j)),
            vec(),
            pl.BlockSpec((1, LRU_BW, LRU_BW), lambda j, i: (j, 0, 0)),
            vec(),
            pl.BlockSpec((1, LRU_BW, LRU_BW), lambda j, i: (j, 0, 0)),
            vec(),
            vec(),
            pl.BlockSpec((nb, LRU_BW), lambda j, i: (0, j)),
        ],
        out_specs=[
            pl.BlockSpec((rows, LRU_BW), lambda j, i: (i, j)),
            pl.BlockSpec((halo, LRU_BW), lambda j, i: (0, j)),
            pl.BlockSpec((nb, LRU_BW), lambda j, i: (0, j)),
        ],
        out_shape=[
            jax.ShapeDtypeStruct((t * nb, d_rnn), BF16),
            jax.ShapeDtypeStruct((halo, d_rnn), F32),
            jax.ShapeDtypeStruct((nb, d_rnn), F32),
        ],
        scratch_shapes=[
            pltpu.VMEM((halo + rows, LRU_BW), F32),
            pltpu.VMEM((rows, LRU_BW), F32),
            pltpu.VMEM((rows, LRU_BW), F32),
            pltpu.VMEM((rows, LRU_BW), F32),
            pltpu.VMEM((nb, LRU_BW), F32),
        ],
        compiler_params=_cparams(("parallel", "arbitrary")),
        name="lru_mixer",
    )(uz, uz, buf, cw, cb, wa, ba, wx, bx, lam, h0)


def _softmax_masked(s, mask):
    sm = jnp.where(mask, s, NEG)
    m = jnp.max(sm, axis=-1, keepdims=True)
    e = jnp.where(mask, jnp.exp(sm - m), 0.0)
    l = jnp.sum(e, axis=-1, keepdims=True)
    return e / jnp.where(l > 0.0, l, 1.0)


def _dot_f32_by_01(x, m01):
    hi = x.astype(BF16)
    r1 = x - hi.astype(F32)
    mid = r1.astype(BF16)
    lo = (r1 - mid.astype(F32)).astype(BF16)
    return _dot(hi, m01) + _dot(mid, m01) + _dot(lo, m01)


def _block_importance_matrix(n_rows, n_blk_lanes, n_blk, rows_per_cmp=1):
    r, s = L_SEL // S_CMP, L_CMP // S_CMP
    i = lax.shift_right_logical(_iota((n_rows, n_blk_lanes), 0), int(math.log2(rows_per_cmp)))
    j = _iota((n_rows, n_blk_lanes), 1)
    hit = (i >= r * j - (s - 1)) & (i <= r * j + (r - 1)) & (j < n_blk)
    return jnp.where(hit, 1.0, 0.0).astype(BF16)


def _select_blocks(imp, pos, n_blk):
    j = _iota(imp.shape, 1)
    cur = lax.shift_right_logical(pos, int(math.log2(L_SEL)))
    valid = j * L_SEL <= pos
    forced = (j == 0) | (j == cur) | (j == cur - 1)
    score = jnp.where(forced, FORCE, jnp.where(valid, imp, -1.0))
    cnt = jnp.zeros(imp.shape, F32)
    for i in range(n_blk):
        si = score[:, i:i + 1]
        ahead = (si > score) | ((si == score) & (j > i))
        cnt = cnt + jnp.where(ahead, 1.0, 0.0)
    keep = (cnt < float(min(N_SEL, n_blk))) & (score >= 0.0) & (j < n_blk)
    return jnp.where(keep, 1.0, 0.0)


def _select_blocks_t(imp, pos_row, n_blk):
    nb8 = -(-n_blk // SUBLANES) * SUBLANES
    imp_t = imp.T[0:nb8, :]
    j = _iota(imp_t.shape, 0)
    cur = lax.shift_right_logical(pos_row, int(math.log2(L_SEL)))
    valid = j * L_SEL <= pos_row
    forced = (j == 0) | (j == cur) | (j == cur - 1)
    score = jnp.where(forced, FORCE, jnp.where(valid, imp_t, -1.0))
    cnt = jnp.zeros(imp_t.shape, F32)
    for i in range(n_blk):
        si = score[i:i + 1, :]
        ahead = (si > score) | ((si == score) & (j > i))
        cnt = cnt + jnp.where(ahead, 1.0, 0.0)
    keep = (cnt < float(min(N_SEL, n_blk))) & (score >= 0.0) & (j < n_blk)
    sel_t = jnp.where(keep, 1.0, 0.0)
    if nb8 < LANES:
        sel_t = jnp.concatenate([sel_t, jnp.zeros((LANES - nb8, imp.shape[0]), F32)], axis=0)
    return sel_t.T


def _online_update(m_ref, l_ref, acc_ref, s, mask, pv):
    sm = jnp.where(mask, s, NEG)
    m_old = m_ref[...]
    m_new = jnp.maximum(m_old, jnp.max(sm, axis=-1, keepdims=True))
    p = jnp.where(mask, jnp.exp(sm - m_new), 0.0)
    alpha = jnp.exp(m_old - m_new)
    l_ref[...] = alpha * l_ref[...] + jnp.sum(p, axis=-1, keepdims=True)
    acc_ref[...] = alpha * acc_ref[...] + pv(p)
    m_ref[...] = m_new


def _online_finish(l_ref, acc_ref):
    l = l_ref[...]
    return acc_ref[...] / jnp.where(l > 0.0, l, 1.0)


def _cmp_accumulate(page, w2_ref, acc_a, acc_b, row0):
    pb = page.astype(BF16)
    for g in range(N_KV):
        cs = slice(g * HEAD_DIM, (g + 1) * HEAD_DIM)
        ab = _dot(w2_ref[g], pb[:, cs])
        acc_a[pl.ds(row0, SUB_PER_PAGE), cs] = ab[0:SUB_PER_PAGE]
        acc_b[pl.ds(row0, SUB_PER_PAGE), cs] = ab[SUB_PER_PAGE:2 * SUB_PER_PAGE]


def _cmp_finish(acc_a, acc_b, lin_ref, out_ref, ncp):
    pooled = acc_a[0:ncp, :] + acc_b[1:ncp + 1, :]
    for g in range(N_KV):
        cs = slice(g * HEAD_DIM, (g + 1) * HEAD_DIM)
        out_ref[0, :, cs] = _dot(pooled[:, cs].astype(BF16), lin_ref[g])


def _compress_rows_body(k_ref, v_ref, w2k_ref, w2v_ref, lk_ref, lv_ref, ok_ref, ov_ref,
                        aak, abk, aav, abv, *, pages, ncp):
    s = pl.program_id(1)

    @pl.when(s == 0)
    def _():
        tail = jnp.zeros((SUBLANES, KV_W), F32)
        abk[ncp:ncp + SUBLANES, :] = tail
        abv[ncp:ncp + SUBLANES, :] = tail

    for i in range(pages):
        row0 = pl.multiple_of((s * pages + i) * SUB_PER_PAGE, SUBLANES)
        rs = slice(i * PAGE_SIZE, (i + 1) * PAGE_SIZE)
        _cmp_accumulate(k_ref[0, rs, :], w2k_ref, aak, abk, row0)
        _cmp_accumulate(v_ref[0, rs, :], w2v_ref, aav, abv, row0)

    @pl.when(s == pl.num_programs(1) - 1)
    def _():
        _cmp_finish(aak, abk, lk_ref, ok_ref, ncp)
        _cmp_finish(aav, abv, lv_ref, ov_ref, ncp)


def _compress_rows(kv, w2k, w2v, lk, lv, *, b, t):
    pages = min(PAGES_PER_STEP, t // PAGE_SIZE)
    rows = pages * PAGE_SIZE
    ns = t // rows
    ncp = t // S_CMP
    wspec = lambda shp: pl.BlockSpec(shp, lambda i, s: (0, 0, 0))
    acc = lambda: pltpu.VMEM((ncp + SUBLANES, KV_W), F32)
    return pl.pallas_call(
        functools.partial(_compress_rows_body, pages=pages, ncp=ncp),
        grid=(b, ns),
        in_specs=[
            pl.BlockSpec((1, rows, KV_W), lambda i, s: (0, i * ns + s, 0)),
            pl.BlockSpec((1, rows, KV_W), lambda i, s: (1, i * ns + s, 0)),
            wspec(w2k.shape), wspec(w2v.shape), wspec(lk.shape), wspec(lv.shape),
        ],
        out_specs=[
            pl.BlockSpec((1, ncp, KV_W), lambda i, s: (i, 0, 0)),
            pl.BlockSpec((1, ncp, KV_W), lambda i, s: (i, 0, 0)),
        ],
        out_shape=[jax.ShapeDtypeStruct((b, ncp, KV_W), F32)] * 2,
        scratch_shapes=[acc(), acc(), acc(), acc()],
        compiler_params=_cparams(("parallel", "arbitrary")),
        name="compress_rows",
    )(kv, kv, w2k, w2v, lk, lv)


PAGE_ROWS = PAGE_SIZE * N_KV
SUB_ROWS = SUB_PER_PAGE * N_KV


def _cmp_accumulate_rg(page, w2_ref, acc_a, acc_b, row0):
    ab = _dot(w2_ref[...], page.astype(BF16))
    acc_a[pl.ds(row0, SUB_ROWS), :] = ab[0:SUB_ROWS]
    acc_b[pl.ds(row0, SUB_ROWS), :] = ab[SUB_ROWS:2 * SUB_ROWS]


def _cmp_finish_rg(acc_a, acc_b, lin_ref, out_ref, ncp):
    for g in range(N_KV):
        pooled = acc_a[pl.ds(g, ncp, stride=N_KV), :] + acc_b[pl.ds(N_KV + g, ncp, stride=N_KV), :]
        out_ref[0, :, g * HEAD_DIM:(g + 1) * HEAD_DIM] = _dot(pooled.astype(BF16), lin_ref[g])


def _compress_paged_body(pt_ref, *refs, pages, n_pool_steps, ncp):
    pk = refs[0:pages]
    pv = refs[pages:2 * pages]
    nk_ref, nv_ref, w2k_ref, w2v_ref, lk_ref, lv_ref, ok_ref, ov_ref, aak, abk, aav, abv = refs[2 * pages:]
    s = pl.program_id(1)

    @pl.when(s < n_pool_steps)
    def _():
        for i in range(pages):
            row0 = pl.multiple_of((s * pages + i) * SUB_ROWS, SUB_ROWS)
            _cmp_accumulate_rg(pk[i][...], w2k_ref, aak, abk, row0)
            _cmp_accumulate_rg(pv[i][...], w2v_ref, aav, abv, row0)

    @pl.when(s == n_pool_steps)
    def _():
        row0 = n_pool_steps * pages * SUB_ROWS
        _cmp_accumulate_rg(nk_ref[0, 0], w2k_ref, aak, abk, row0)
        _cmp_accumulate_rg(nv_ref[0, 0], w2v_ref, aav, abv, row0)
        _cmp_finish_rg(aak, abk, lk_ref, ok_ref, ncp)
        _cmp_finish_rg(aav, abv, lv_ref, ov_ref, ncp)


def _page_spec(i, pages, n_pool_steps):
    def index(bi, s, pt):
        return (pt[bi, jnp.minimum(s, n_pool_steps - 1) * pages + i], 0)
    return pl.BlockSpec((PAGE_ROWS, HEAD_DIM), index)


def _compress_paged(page_table, pool_k, pool_v, new_pages, w2k, w2v, lk, lv):
    b, n_pages = page_table.shape
    pages = PAGES_PER_STEP
    n_pool_steps = n_pages // pages
    ncp = n_pages * SUB_PER_PAGE
    wspec = lambda shp: pl.BlockSpec(shp, lambda bi, s, pt: (0,) * len(shp))
    newspec = lambda idx: pl.BlockSpec((1, 1, PAGE_ROWS, HEAD_DIM), lambda bi, s, pt: (idx, bi, 0, 0))
    acc = lambda: pltpu.VMEM(((ncp + SUB_PER_PAGE) * N_KV, HEAD_DIM), F32)
    grid_spec = pltpu.PrefetchScalarGridSpec(
        num_scalar_prefetch=1,
        grid=(b, n_pool_steps + 1),
        in_specs=(
            [_page_spec(i, pages, n_pool_steps) for i in range(pages)] * 2
            + [newspec(0), newspec(1), wspec(w2k.shape), wspec(w2v.shape), wspec(lk.shape), wspec(lv.shape)]
        ),
        out_specs=[
            pl.BlockSpec((1, ncp, KV_W), lambda bi, s, pt: (bi, 0, 0)),
            pl.BlockSpec((1, ncp, KV_W), lambda bi, s, pt: (bi, 0, 0)),
        ],
        scratch_shapes=[acc(), acc(), acc(), acc()],
    )
    return pl.pallas_call(
        functools.partial(_compress_paged_body, pages=pages, n_pool_steps=n_pool_steps, ncp=ncp),
        grid_spec=grid_spec,
        out_shape=[jax.ShapeDtypeStruct((b, ncp, KV_W), F32)] * 2,
        compiler_params=_cparams(("parallel", "arbitrary")),
        name="compress_paged",
    )(page_table, *([pool_k] * pages), *([pool_v] * pages), new_pages, new_pages, w2k, w2v, lk, lv)


LOG2E = 1.4426950408889634


def _lane_fold(x, op):
    out = x[:, 0:LANES]
    for u in range(1, x.shape[1] // LANES):
        out = op(out, x[:, u * LANES:(u + 1) * LANES])
    return out


def _nsa_prompt_body(q_ref, z_ref, gt_ref, ck_ref, cv_ref, ks_ref, vs_ref, kw_ref, vw_ref, a_ref,
                     ksb, vsb, kwb, vwb, s_buf, mx_s, acc_s, *, tq, t, kc, n_cmp):
    ones_col = jnp.where(_iota((t, LANES), 1) == 0, 1.0, 0.0).astype(BF16)
    ksb[...] = ks_ref[0].astype(BF16)
    vsb[...] = jnp.concatenate([vs_ref[0].astype(BF16), ones_col], axis=1)
    kwb[...] = kw_ref[0].astype(BF16)
    vwb[...] = jnp.concatenate([vw_ref[0].astype(BF16), ones_col], axis=1)

    n_blk = -(-t // L_SEL)
    ncl = ck_ref.shape[1]
    scale = HEAD_DIM ** -0.5
    c_exp = scale * LOG2E
    hs = [slice(h * HEAD_DIM, (h + 1) * HEAD_DIM) for h in range(HPG)]
    hr = [slice(h * tq, (h + 1) * tq) for h in range(HPG)]
    ckb = ck_ref[0].astype(BF16)
    cvb = cv_ref[0].astype(BF16)
    msel = _block_importance_matrix(ncl, LANES, n_blk)
    ws = min(WINDOW + tq, t)
    blk_shift = int(math.log2(L_SEL))

    def q_tile(qi, carry):
        q0 = pl.multiple_of(qi * tq, tq)
        rows = pl.ds(q0, tq)
        pos = q0 + _iota((tq, 1), 0)
        qb = jnp.concatenate([q_ref[rows, hs[h]] for h in range(HPG)], axis=0).astype(BF16)

        ci = _iota((1, ncl), 1)
        cmask = (ci * S_CMP + (L_CMP - 1) <= pos) & (ci < n_cmp)
        s_c = _dot_nt(qb, ckb) * scale
        p_c = [_softmax_masked(s_c[hr[h]], cmask) for h in range(HPG)]
        imp_c = p_c[0] + p_c[1] + p_c[2] + p_c[3]
        o_c = _dot(jnp.concatenate(p_c, axis=0).astype(BF16), cvb)

        imp = _dot_f32_by_01(imp_c, msel)
        selb = _select_blocks_t(imp, q0 + _iota((1, tq), 1), n_blk).astype(BF16)

        w0 = pl.multiple_of(jnp.minimum(jnp.maximum(q0 - WINDOW, 0), t - ws), tq)
        wpos = w0 + _iota((1, ws), 1)
        wbias = jnp.where((wpos <= pos) & (wpos > pos - WINDOW), 0.0, NEG)
        s_w = _dot_nt(qb, kwb[pl.ds(w0, ws), :])
        e_w = []
        for h in range(HPG):
            sh = s_w[hr[h]] + wbias
            e_w.append(jnp.exp2((sh - jnp.max(sh, axis=-1, keepdims=True)) * c_exp))
        pv_w = _dot(jnp.concatenate(e_w, axis=0).astype(BF16), vwb[pl.ds(w0, ws), :])
        o_w = pv_w[:, 0:HEAD_DIM] / pv_w[:, HEAD_DIM:HEAD_DIM + 1]

        mx_s[...] = jnp.full(mx_s.shape, NEG, F32)

        def pass1(ki, c):
            k0 = pl.multiple_of(ki * kc, kc)
            kpos = k0 + _iota((1, kc), 1)
            blk = lax.shift_right_logical(k0 + _iota((LANES, kc), 1), blk_shift)
            expand = jnp.where(blk == _iota((LANES, kc), 0), 1.0, 0.0).astype(BF16)
            bias = (_dot(selb, expand) - 1.0) * (-NEG)
            bias = jnp.where(kpos <= pos, bias, NEG)
            s = _dot_nt(qb, ksb[pl.ds(k0, kc), :])
            for h in range(HPG):
                sh = s[hr[h]] + bias
                s_buf[ki, hr[h], :] = sh
                mx_s[hr[h], :] = jnp.maximum(mx_s[hr[h], :], _lane_fold(sh, jnp.maximum))
            return c

        n_chunks = (q0 + tq + kc - 1) // kc
        lax.fori_loop(0, n_chunks, pass1, 0)
        m_b = jnp.broadcast_to(jnp.max(mx_s[...], axis=-1, keepdims=True), mx_s.shape)
        mx_s[...] = m_b
        acc_s[...] = jnp.zeros(acc_s.shape, F32)

        def pass2(ki, c):
            k0 = pl.multiple_of(ki * kc, kc)
            mb = mx_s[...]
            e = [jnp.exp2((s_buf[ki, :, u * LANES:(u + 1) * LANES] - mb) * c_exp) for u in range(kc // LANES)]
            acc_s[...] = acc_s[...] + _dot(jnp.concatenate(e, axis=1).astype(BF16), vsb[pl.ds(k0, kc), :])
            return c

        lax.fori_loop(0, n_chunks, pass2, 0)
        o_s = acc_s[:, 0:HEAD_DIM] / acc_s[:, HEAD_DIM:HEAD_DIM + 1]

        gates = jax.nn.sigmoid(gt_ref[rows, :])
        for h in range(HPG):
            o = (gates[:, h:h + 1] * o_c[hr[h]] + gates[:, HPG + h:HPG + h + 1] * o_s[hr[h]]
                 + gates[:, 2 * HPG + h:2 * HPG + h + 1] * o_w[hr[h]])
            a_ref[rows, hs[h]] = (o * _silu(z_ref[rows, hs[h]])).astype(BF16)
        return carry

    lax.fori_loop(0, t // tq, q_tile, 0)


def _nsa_prompt_attend(qz, gates, ck, cv, kv, *, b, t, tq, kc):
    gq = HPG * HEAD_DIM
    n_cmp = t // S_CMP - (L_CMP // S_CMP) + 1
    ncl = ck.shape[1]
    kvspec = lambda idx: pl.BlockSpec((1, t, HEAD_DIM), lambda i, g: (idx, i, g))
    kb = lambda: pltpu.VMEM((t, HEAD_DIM), BF16)
    vb = lambda: pltpu.VMEM((t, HEAD_DIM + LANES), BF16)
    return pl.pallas_call(
        functools.partial(_nsa_prompt_body, tq=tq, t=t, kc=kc, n_cmp=n_cmp),
        grid=(b, N_KV),
        in_specs=[
            pl.BlockSpec((t, gq), lambda i, g: (i, g)),
            pl.BlockSpec((t, gq), lambda i, g: (i, N_KV + g)),
            pl.BlockSpec((t, LANES), lambda i, g: (i, g)),
            pl.BlockSpec((1, ncl, HEAD_DIM), lambda i, g: (i, 0, g)),
            pl.BlockSpec((1, ncl, HEAD_DIM), lambda i, g: (i, 0, g)),
            kvspec(2), kvspec(3), kvspec(4), kvspec(5),
        ],
        out_specs=pl.BlockSpec((t, gq), lambda i, g: (i, g)),
        out_shape=jax.ShapeDtypeStruct((b * t, N_KV * gq), BF16),
        scratch_shapes=[
            kb(), vb(), kb(), vb(),
            pltpu.VMEM((t // kc, HPG * tq, kc), F32),
            pltpu.VMEM((HPG * tq, LANES), F32),
            pltpu.VMEM((HPG * tq, HEAD_DIM + LANES), F32),
        ],
        compiler_params=_cparams(("parallel", "parallel")),
        name="nsa_prompt_attend",
    )(qz, qz, gates, ck, cv, kv, kv, kv, kv)


def _rows_by_group(ref, n):
    return jnp.concatenate([ref[pl.ds(g, n, stride=N_KV), :] for g in range(N_KV)], axis=1)


def _diag_blocks(x, rows_per_group):
    return jnp.concatenate(
        [x[g * rows_per_group:(g + 1) * rows_per_group, g * HEAD_DIM:(g + 1) * HEAD_DIM]
         for g in range(N_KV)], axis=0)


def _pad_rows(x, n):
    return jnp.concatenate([x, jnp.zeros((n - x.shape[0], x.shape[1]), x.dtype)], axis=0)


def _nsa_sample_body(pt_ref, *refs, pages, n_pool_steps, past, n_new, wb):
    sk = refs[0:pages]
    sv = refs[pages:2 * pages]
    (nks_ref, nvs_ref, nkw_ref, nvw_ref, nkw_rg, nvw_rg, ck_ref, cv_ref, wk_ref, wv_ref, q_ref, z_ref, gt_ref,
     a_ref, wko_ref, wvo_ref, q2, selr, m_s, l_s, acc, oc, ow) = refs[2 * pages:]
    s = pl.program_id(1)
    rows = N_HEADS * n_new
    rpg = HPG * n_new
    scale = HEAD_DIM ** -0.5
    n_blk = past // L_SEL + -(-n_new // L_SEL)
    n_cmp = ck_ref.shape[1]
    sel_lanes = selr.shape[1]
    pos = past + (_iota((rows, 1), 0) % n_new)
    new_page = lambda ref: _pad_rows(ref[0], PAGE_SIZE).astype(BF16)
    new_mask = (past + _iota((1, PAGE_SIZE), 1) <= pos) & (_iota((1, PAGE_SIZE), 1) < n_new)

    @pl.when(s == 0)
    def _():
        q2[...] = jnp.zeros(q2.shape, F32)
        for hh in range(N_HEADS):
            g = hh // HPG
            q2[hh * n_new:(hh + 1) * n_new, g * HEAD_DIM:(g + 1) * HEAD_DIM] = (
                q_ref[:, hh * HEAD_DIM:(hh + 1) * HEAD_DIM])
        qb = q2[...].astype(BF16)

        ci = _iota((1, n_cmp), 1)
        cmask = ci * S_CMP + (L_CMP - 1) <= pos
        p_c = _softmax_masked(_dot_nt(qb, ck_ref[0].astype(BF16)) * scale, cmask)
        oc[...] = _diag_blocks(_dot(p_c.astype(BF16), cv_ref[0].astype(BF16)), rpg)

        imp_c = jnp.concatenate(
            [sum(p_c[(g * HPG + h) * n_new:(g * HPG + h + 1) * n_new, :] for h in range(HPG))
             for g in range(N_KV)], axis=0)
        imp = _dot_f32_by_01(imp_c, _block_importance_matrix(n_cmp, sel_lanes, n_blk))
        pos_gt = past + (_iota((N_KV * n_new, 1), 0) % n_new)
        sel = _select_blocks(imp, pos_gt, n_blk)
        selr[...] = jnp.concatenate(
            [sel[(hh // HPG) * n_new:(hh // HPG + 1) * n_new, :] for hh in range(N_HEADS)], axis=0)

        wpos = (past - wb) + _iota((1, wb), 1)
        mask = jnp.concatenate([(wpos <= pos) & (wpos > pos - WINDOW), new_mask], axis=1)
        wk = _rows_by_group(wk_ref.at[0], wb).astype(BF16)
        wv = _rows_by_group(wv_ref.at[0], wb).astype(BF16)
        sc = jnp.concatenate([_dot_nt(qb, wk), _dot_nt(qb, new_page(nkw_ref))], axis=1) * scale
        p_w = _softmax_masked(sc, mask)
        ow[...] = _diag_blocks(
            _dot(p_w[:, 0:wb].astype(BF16), wv) + _dot(p_w[:, wb:].astype(BF16), new_page(nvw_ref)), rpg)

        keep = (wb - n_new) * N_KV
        wko_ref[0, 0:keep, :] = wk_ref[0, n_new * N_KV:wb * N_KV, :]
        wko_ref[0, keep:wb * N_KV, :] = nkw_rg[...]
        wvo_ref[0, 0:keep, :] = wv_ref[0, n_new * N_KV:wb * N_KV, :]
        wvo_ref[0, keep:wb * N_KV, :] = nvw_rg[...]

        m_s[...] = jnp.full(m_s.shape, NEG, F32)
        l_s[...] = jnp.zeros(l_s.shape, F32)
        acc[...] = jnp.zeros(acc.shape, F32)

    @pl.when(s < n_pool_steps)
    def _():
        qb = q2[...].astype(BF16)
        nk = pages * PAGE_SIZE
        sc = jnp.concatenate(
            [_dot_nt(qb, _rows_by_group(sk[i], PAGE_SIZE).astype(BF16)) for i in range(pages)], axis=1) * scale
        blk = s * (nk // L_SEL) + lax.shift_right_logical(_iota((sel_lanes, nk), 1), int(math.log2(L_SEL)))
        expand = jnp.where(blk == _iota((sel_lanes, nk), 0), 1.0, 0.0).astype(BF16)
        mask = _dot(selr[...].astype(BF16), expand) > 0.5

        def pv(p):
            return sum(_dot(p[:, i * PAGE_SIZE:(i + 1) * PAGE_SIZE].astype(BF16),
                            _rows_by_group(sv[i], PAGE_SIZE).astype(BF16)) for i in range(pages))

        _online_update(m_s, l_s, acc, sc, mask, pv)

    @pl.when(s == n_pool_steps)
    def _():
        qb = q2[...].astype(BF16)
        new_blk = past // L_SEL
        mask = (selr[:, new_blk:new_blk + 1] > 0.5) & new_mask
        _online_update(m_s, l_s, acc, _dot_nt(qb, new_page(nks_ref)) * scale, mask,
                       lambda p: _dot(p.astype(BF16), new_page(nvs_ref)))
        o_s = _diag_blocks(_online_finish(l_s, acc), rpg)

        gates = jax.nn.sigmoid(gt_ref[...])

        def gate_col(br):
            return jnp.concatenate(
                [gates[:, (hh // HPG) * LANES + br * HPG + hh % HPG:(hh // HPG) * LANES + br * HPG + hh % HPG + 1]
                 for hh in range(N_HEADS)], axis=0)

        zr = jnp.concatenate([z_ref[:, hh * HEAD_DIM:(hh + 1) * HEAD_DIM] for hh in range(N_HEADS)], axis=0)
        o = gate_col(0) * oc[...] + gate_col(1) * o_s + gate_col(2) * ow[...]
        res = o * _silu(zr)
        for hh in range(N_HEADS):
            a_ref[:, hh * HEAD_DIM:(hh + 1) * HEAD_DIM] = res[hh * n_new:(hh + 1) * n_new, :]


def _nsa_sample_attend(page_table, pool_k, pool_v, kv, kw_rg, vw_rg, ck, cv, win_k, win_v, qz, gates, *, n_new):
    b, n_pages = page_table.shape
    pages = PAGES_PER_STEP
    n_pool_steps = n_pages // pages
    past = n_pages * PAGE_SIZE
    wb = win_k.shape[1] // N_KV
    rows = N_HEADS * n_new
    qw = N_HEADS * HEAD_DIM
    n_blk = past // L_SEL + -(-n_new // L_SEL)
    sel_lanes = -(-n_blk // LANES) * LANES
    newspec = lambda idx: pl.BlockSpec((1, n_new, KV_W), lambda bi, s, pt: (idx, bi, 0))
    newspec_rg = lambda: pl.BlockSpec((n_new * N_KV, HEAD_DIM), lambda bi, s, pt: (bi, 0))
    full = lambda arr: pl.BlockSpec((1,) + arr.shape[1:], lambda bi, s, pt: (bi, 0, 0))
    grid_spec = pltpu.PrefetchScalarGridSpec(
        num_scalar_prefetch=1,
        grid=(b, n_pool_steps + 1),
        in_specs=(
            [_page_spec(i, pages, n_pool_steps) for i in range(pages)] * 2
            + [newspec(2), newspec(3), newspec(4), newspec(5), newspec_rg(), newspec_rg(),
               full(ck), full(cv), full(win_k), full(win_v),
               pl.BlockSpec((n_new, qw), lambda bi, s, pt: (bi, 0)),
               pl.BlockSpec((n_new, qw), lambda bi, s, pt: (bi, 1)),
               pl.BlockSpec((n_new, N_KV * LANES), lambda bi, s, pt: (bi, 0))]
        ),
        out_specs=[
            pl.BlockSpec((n_new, qw), lambda bi, s, pt: (bi, 0)),
            full(win_k), full(win_v),
        ],
        scratch_shapes=[
            pltpu.VMEM((rows, KV_W), F32),
            pltpu.VMEM((rows, sel_lanes), F32),
            pltpu.VMEM((rows, 1), F32),
            pltpu.VMEM((rows, 1), F32),
            pltpu.VMEM((rows, KV_W), F32),
            pltpu.VMEM((rows, HEAD_DIM), F32),
            pltpu.VMEM((rows, HEAD_DIM), F32),
        ],
    )
    body = functools.partial(_nsa_sample_body, pages=pages, n_pool_steps=n_pool_steps, past=past,
                             n_new=n_new, wb=wb)

    return pl.pallas_call(
        body,
        grid_spec=grid_spec,
        out_shape=[
            jax.ShapeDtypeStruct((b * n_new, qw), F32),
            jax.ShapeDtypeStruct(win_k.shape, F32),
            jax.ShapeDtypeStruct(win_v.shape, F32),
        ],
        compiler_params=_cparams(("parallel", "arbitrary")),
        name="nsa_sample_attend",
    )(page_table, *([pool_k] * pages), *([pool_v] * pages), kv, kv, kv, kv, kw_rg, vw_rg,
      ck, cv, win_k, win_v, qz, qz, gates)


def _rope_tables(pos, dk):
    half = dk // 2
    inv = 1.0 / (ROPE_BASE ** jnp.linspace(0.0, 1.0, half))
    ang = pos.astype(F32)[:, None] * inv[None, :]
    cos = jnp.repeat(jnp.cos(ang), 2, axis=1)
    sin = jnp.stack([-jnp.sin(ang), jnp.sin(ang)], axis=-1).reshape(pos.shape[0], dk)
    return cos, sin


def _pooling_matrix(w_pos):
    eye = jnp.eye(SUB_PER_PAGE, dtype=F32)
    first = jnp.einsum("mn,lg->gmnl", eye, w_pos[:S_CMP]).reshape(N_KV, SUB_PER_PAGE, PAGE_SIZE)
    second = jnp.einsum("mn,lg->gmnl", eye, w_pos[S_CMP:]).reshape(N_KV, SUB_PER_PAGE, PAGE_SIZE)
    return jnp.concatenate([first, second], axis=1).astype(BF16)


def _pooling_matrix_rg(w_pos):
    band = _pooling_matrix(w_pos).astype(F32).reshape(N_KV, 2, SUB_PER_PAGE, PAGE_SIZE)
    full = jnp.einsum("ghmr,gk->hmgrk", band, jnp.eye(N_KV, dtype=F32))
    return full.reshape(2 * SUB_ROWS, PAGE_ROWS).astype(BF16)


def _split_nsa_w_in(w):
    qw = N_HEADS * HEAD_DIM
    ng = 3 * N_HEADS
    d = w.shape[0]
    w_qz = w[:, :2 * qw]
    w_g = w[:, 2 * qw:2 * qw + ng].reshape(d, 3, N_KV, HPG).transpose(0, 2, 1, 3).reshape(d, N_KV, 3 * HPG)
    w_g = jnp.pad(w_g, ((0, 0), (0, 0), (0, LANES - 3 * HPG))).reshape(d, N_KV * LANES)
    w_kv = w[:, 2 * qw + ng:]
    return w_qz.astype(BF16), w_g.astype(BF16), w_kv.astype(BF16)


def _tile_rows(m, pref):
    return pref if m % pref == 0 else m


def _run_group(x, *, pos0, pool_buf, ret_s0, conv_buf, lru_h0, nsa_cache, params):
    nb, t, d = x.shape
    m = nb * t
    tm = _tile_rows(m, OUT_ROWS)
    tm_in = _tile_rows(m, IN_ROWS)
    fused_layout = t % tm == 0 and t % tm_in == 0
    nt = t // tm if fused_layout else 1
    nt_in = t // tm_in if fused_layout else 1
    bm_in = lambda i, j: (i, 0)
    bm_io = lambda i: (i, 0)
    tb_in = lambda i, j: (i % nt_in, i // nt_in)
    tb_io = lambda i: (i % nt, i // nt)
    pos = pos0 + jnp.arange(t, dtype=jnp.int32)
    vec = lambda v: v.reshape(1, -1)

    depth = params["norm_pre"].shape[0]
    cur = x.reshape(m, d)
    cur_is_tm = False
    states = {k: [] for k in ("pool", "ret", "conv", "lru", "nsa")}

    def project(xv, x_in, g_pre, w):
        n = w.shape[1]
        tn = IN_COLS if n % IN_COLS == 0 else n
        return _norm_matmul(xv, x_in, g_pre, w, m=m, tm=tm_in, tn=tn, out_shape=(m, n),
                            out_block=(tm_in, tn), out_map=lambda a, b_: (a, b_))

    def to_bm_view(a):
        return (a.reshape(t, nb * d), tb_in, tb_io) if fused_layout else (a, bm_in, bm_io)

    for i in range(depth):
        kind, j = i % 4, i // 4
        g_pre = vec(params["norm_pre"][i])
        g_post = vec(params["norm_post"][i])
        if kind == 0:
            w_in = params["pool_w_in"][j].astype(BF16)
            e = w_in.shape[1] // 2
            uz = project(cur, bm_in, g_pre, w_in)
            buf16 = jnp.pad(pool_buf[j], ((0, 0), (POOL_HALO - pool_buf[j].shape[1], 0), (0, 0)))
            a, st = _pool_mixer(uz, buf16, params["pool_w_grp"][j].astype(BF16), vec(params["pool_b_grp"][j]),
                                vec(params["pool_scale"][j]), b=nb, t=t, tt=_tile_rows(t, 256), pos0=pos0)
            states["pool"].append(st[:, 1:, :])
            cur = _out_proj(a, params["pool_w_out"][j].astype(BF16), cur, bm_io, g_post, m=m, tm=tm,
                            out_shape=(m, d), out_map=bm_io)
        elif kind == 1:
            w_in = params["ret_w_in"][j].astype(BF16)
            dk = d // R_HEADS
            dv = 2 * dk
            proj = project(cur, bm_in, g_pre, w_in)
            cos_t, sin_t = _rope_tables(pos, dk)
            lg = jnp.log(1.0 - 2.0 ** (-5.0 - jnp.arange(R_HEADS, dtype=F32)))
            a, s_new = _retention_mixer(proj, cos_t, sin_t, lg, ret_s0[j], b=nb, t=t, dk=dk, dv=dv)
            states["ret"].append(s_new)
            w_out = params["ret_w_out"][j].astype(BF16)
            if fused_layout:
                cur = _out_proj(a, w_out, cur, bm_io, g_post, m=m, tm=tm,
                                out_shape=(t, nb * d), out_map=tb_io).reshape(m, d)
            else:
                cur = _out_proj(a, w_out, cur, bm_io, g_post, m=m, tm=tm, out_shape=(m, d), out_map=bm_io)
                cur = cur.reshape(nb, t, d).swapaxes(0, 1).reshape(m, d)
            cur_is_tm = True
        elif kind == 2:
            assert cur_is_tm
            w_in = params["lru_w_in"][j].astype(BF16)
            d_rnn = w_in.shape[1] // 2
            uz = project(cur, bm_in, g_pre, w_in)
            buf = conv_buf[j].swapaxes(0, 1).reshape((CONV_W - 1) * nb, d_rnn)
            tt = max(1, min(t, LRU_ROWS // nb))
            a, cs_new, h_new = _lru_mixer(
                uz, buf, params["lru_conv_w"][j], vec(params["lru_conv_b"][j]),
                params["lru_w_a"][j].astype(BF16), vec(params["lru_b_a"][j]),
                params["lru_w_x"][j].astype(BF16), vec(params["lru_b_x"][j]),
                vec(params["lru_lam"][j]), lru_h0[j], nb=nb, t=t, tt=tt)
            states["conv"].append(cs_new.reshape(CONV_W - 1, nb, d_rnn).swapaxes(0, 1))
            states["lru"].append(h_new)
            cur = _out_proj(a, params["lru_w_out"][j].astype(BF16), cur, bm_io, g_post, m=m, tm=tm,
                            out_shape=(m, d), out_map=bm_io)
        else:
            if cur_is_tm and not fused_layout:
                cur = cur.reshape(t, nb, d).swapaxes(0, 1).reshape(m, d)
                cur_is_tm = False
            xv, x_in, x_io = to_bm_view(cur) if cur_is_tm else (cur, bm_in, bm_io)
            w_qz, w_g, w_kv = _split_nsa_w_in(params["nsa_w_in"][j])
            qz = project(xv, x_in, g_pre, w_qz)
            gates = project(xv, x_in, g_pre, w_g)
            n_kv_t = w_kv.shape[1] // KV_W
            kv, kv_rg = _norm_matmul_kv(xv, x_in, g_pre, w_kv, m=m, tm=tm_in)
            pos_k, pos_v = params["nsa_cmp_pos_k"][j], params["nsa_cmp_pos_v"][j]
            lin_k, lin_v = params["nsa_cmp_lin_k"][j], params["nsa_cmp_lin_v"][j]
            kv5 = [a_.reshape(nb, t, N_KV, HEAD_DIM) for a_ in kv_rg]
            if nsa_cache is None:
                ck, cv = _compress_rows(kv, _pooling_matrix(pos_k), _pooling_matrix(pos_v),
                                        lin_k.astype(BF16), lin_v.astype(BF16), b=nb, t=t)
                a = _nsa_prompt_attend(qz, gates, ck, cv, kv, b=nb, t=t, tq=NSA_TQ, kc=NSA_KC)
                wb = min(WINDOW, t)
                new_state = (kv5[0], kv5[1], kv5[2], kv5[3], kv5[4][:, t - wb:], kv5[5][:, t - wb:])
            else:
                page_table, cmp_k, cmp_v, sel_k, sel_v, win_k, win_v = nsa_cache
                pool2d = lambda c: c[j].reshape(c.shape[1] * PAGE_ROWS, HEAD_DIM)
                win2d = lambda c: c[j].reshape(c.shape[1], c.shape[2] * N_KV, HEAD_DIM)
                new_pages = jnp.pad(jnp.stack(kv_rg[0:2]).reshape(2, nb, t * N_KV, HEAD_DIM),
                                    ((0, 0), (0, 0), (0, PAGE_ROWS - t * N_KV), (0, 0)))
                ck, cv = _compress_paged(page_table, pool2d(cmp_k), pool2d(cmp_v), new_pages,
                                         _pooling_matrix_rg(pos_k), _pooling_matrix_rg(pos_v),
                                         lin_k.astype(BF16), lin_v.astype(BF16))
                a, wk_new, wv_new = _nsa_sample_attend(page_table, pool2d(sel_k), pool2d(sel_v), kv, kv_rg[4], kv_rg[5], ck, cv,
                                                       win2d(win_k), win2d(win_v), qz, gates, n_new=t)
                wshape = (nb, win_k.shape[2], N_KV, HEAD_DIM)
                new_state = (kv5[0], kv5[1], kv5[2], kv5[3], wk_new.reshape(wshape), wv_new.reshape(wshape))
            states["nsa"].append(new_state)
            cur = _out_proj(a, params["nsa_w_out"][j].astype(BF16), xv, x_io, g_post, m=m, tm=tm,
                            out_shape=(m, d), out_map=bm_io)
            cur_is_tm = False
    if cur_is_tm:
        cur = cur.reshape(t, nb, d).swapaxes(0, 1).reshape(m, d)
    return cur.reshape(nb, t, d), states


def kernel(x_prompt, x_sample, state_pool, state_ret, state_conv, state_lru, cache_cmp_k, cache_cmp_v, cache_sel_k, cache_sel_v, cache_win_k, cache_win_v, page_table, norm_pre, norm_post, pool_w_in, pool_w_grp, pool_b_grp, pool_scale, pool_w_out, ret_w_in, ret_w_out, lru_w_in, lru_conv_w, lru_conv_b, lru_w_a, lru_b_a, lru_w_x, lru_b_x, lru_lam, lru_w_out, nsa_w_in, nsa_cmp_pos_k, nsa_cmp_lin_k, nsa_cmp_pos_v, nsa_cmp_lin_v, nsa_w_out):
    params = dict(
        norm_pre=norm_pre, norm_post=norm_post, pool_w_in=pool_w_in, pool_w_grp=pool_w_grp, pool_b_grp=pool_b_grp,
        pool_scale=pool_scale, pool_w_out=pool_w_out, ret_w_in=ret_w_in, ret_w_out=ret_w_out, lru_w_in=lru_w_in,
        lru_conv_w=lru_conv_w, lru_conv_b=lru_conv_b, lru_w_a=lru_w_a, lru_b_a=lru_b_a, lru_w_x=lru_w_x,
        lru_b_x=lru_b_x, lru_lam=lru_lam, lru_w_out=lru_w_out, nsa_w_in=nsa_w_in, nsa_cmp_pos_k=nsa_cmp_pos_k,
        nsa_cmp_lin_k=nsa_cmp_lin_k, nsa_cmp_pos_v=nsa_cmp_pos_v, nsa_cmp_lin_v=nsa_cmp_lin_v, nsa_w_out=nsa_w_out)
    b = x_prompt.shape[0]
    past = page_table.shape[1] * PAGE_SIZE
    zeros_like_state = lambda s: jnp.zeros((s.shape[0], b) + s.shape[2:], F32)
    yp, sp = _run_group(
        x_prompt, pos0=0, pool_buf=zeros_like_state(state_pool), ret_s0=zeros_like_state(state_ret),
        conv_buf=zeros_like_state(state_conv), lru_h0=zeros_like_state(state_lru), nsa_cache=None, params=params)
    ys, ss = _run_group(
        x_sample, pos0=past, pool_buf=state_pool, ret_s0=state_ret, conv_buf=state_conv, lru_h0=state_lru,
        nsa_cache=(page_table, cache_cmp_k, cache_cmp_v, cache_sel_k, cache_sel_v, cache_win_k, cache_win_v),
        params=params)
    out = [yp, ys]
    for key in ("pool", "ret", "conv", "lru"):
        out += [jnp.stack(sp[key]), jnp.stack(ss[key])]
    for idx in range(6):
        out += [jnp.stack([e[idx] for e in sp["nsa"]]), jnp.stack([e[idx] for e in ss["nsa"]])]
    return tuple(out)
```

```python
import functools
import math

import jax
import jax.numpy as jnp
from jax import lax
from jax.experimental import pallas as pl
from jax.experimental.pallas import tpu as pltpu

F32 = jnp.float32
BF16 = jnp.bfloat16

EPS = 1e-6
NEG = -1e30
FORCE = 1e6

PAGE_SIZE = 128
POOL_WINDOWS = (2, 4, 8, 16)
R_HEADS = 4
R_CHUNK = 128
ROPE_BASE = 10000.0
LRU_BW = 256
CONV_W = 4
LRU_C = 8.0
N_HEADS = 16
N_KV = 4
HPG = N_HEADS // N_KV
HEAD_DIM = 128
L_CMP = 32
S_CMP = 16
L_SEL = 64
N_SEL = 16
WINDOW = 512
KV_W = N_KV * HEAD_DIM
SUB_PER_PAGE = PAGE_SIZE // S_CMP
PAGES_PER_STEP = 16
LRU_ROWS = 1024
POOL_ROWS = 512

LANES = 128
SUBLANES = 8
VMEM_LIMIT = 56 * 1024 * 1024
IN_ROWS, IN_COLS = 2048, 1024
KV_ROWS = 1024
OUT_ROWS = 512
NSA_TQ, NSA_KC = 256, 512


def _cparams(sem):
    return pltpu.CompilerParams(dimension_semantics=sem, vmem_limit_bytes=VMEM_LIMIT)


def _act_dtype(rows):
    return BF16 if rows % (2 * SUBLANES) == 0 else F32


def _dot(a, b):
    return jnp.dot(a, b, preferred_element_type=F32)


def _dot_nt(a, b):
    return lax.dot_general(a, b, (((1,), (1,)), ((), ())), preferred_element_type=F32)


def _dot_tn(a, b):
    return lax.dot_general(a, b, (((0,), (0,)), ((), ())), preferred_element_type=F32)


def _silu(x):
    return x * jax.nn.sigmoid(x)


def _iota(shape, dim):
    return lax.broadcasted_iota(jnp.int32, shape, dim)


def _norm_mm_body(x_ref, g_ref, w_ref, o_ref, h_ref):
    @pl.when(pl.program_id(1) == 0)
    def _():
        x = x_ref[...]
        ms = jnp.mean(x * x, axis=-1, keepdims=True)
        h_ref[...] = (x * lax.rsqrt(ms + EPS) * g_ref[...]).astype(BF16)

    o_ref[...] = _dot(h_ref[...], w_ref[...]).reshape(o_ref.shape)


def _norm_matmul(x, x_map, g, w, *, m, tm, tn, out_shape, out_block, out_map):
    d, n = w.shape
    return pl.pallas_call(
        _norm_mm_body,
        grid=(m // tm, n // tn),
        in_specs=[
            pl.BlockSpec((tm, d), x_map),
            pl.BlockSpec((1, d), lambda i, j: (0, 0)),
            pl.BlockSpec((d, tn), lambda i, j: (0, j)),
        ],
        out_specs=pl.BlockSpec(out_block, out_map),
        out_shape=jax.ShapeDtypeStruct(out_shape, F32),
        scratch_shapes=[pltpu.VMEM((tm, d), BF16)],
        compiler_params=_cparams(("parallel", "arbitrary")),
        name="norm_matmul",
    )(x, g, w)


def _norm_mm_kv_body(x_ref, g_ref, w_ref, o_ref, *rest):
    oi_refs, h_ref = rest[:-1], rest[-1]
    j = pl.program_id(1)

    @pl.when(j == 0)
    def _():
        x = x_ref[...]
        ms = jnp.mean(x * x, axis=-1, keepdims=True)
        h_ref[...] = (x * lax.rsqrt(ms + EPS) * g_ref[...]).astype(BF16)

    r = _dot(h_ref[...], w_ref[...])
    o_ref[0] = r
    for jj, oi_ref in enumerate(oi_refs):
        @pl.when(j == jj)
        def _():
            for g in range(N_KV):
                oi_ref[pl.ds(g, r.shape[0], stride=N_KV), :] = r[:, g * HEAD_DIM:(g + 1) * HEAD_DIM]


def _norm_matmul_kv(x, x_map, g, w, *, m, tm):
    d, n = w.shape
    n_t = n // KV_W
    outs = pl.pallas_call(
        _norm_mm_kv_body,
        grid=(m // tm, n_t),
        in_specs=[
            pl.BlockSpec((tm, d), x_map),
            pl.BlockSpec((1, d), lambda i, j: (0, 0)),
            pl.BlockSpec((d, KV_W), lambda i, j: (0, j)),
        ],
        out_specs=[pl.BlockSpec((1, tm, KV_W), lambda i, j: (j, i, 0))]
        + [pl.BlockSpec((tm * N_KV, HEAD_DIM), lambda i, j: (i, 0)) for _ in range(n_t)],
        out_shape=[jax.ShapeDtypeStruct((n_t, m, KV_W), F32)]
        + [jax.ShapeDtypeStruct((m * N_KV, HEAD_DIM), F32) for _ in range(n_t)],
        scratch_shapes=[pltpu.VMEM((tm, d), BF16)],
        compiler_params=_cparams(("parallel", "arbitrary")),
        name="norm_matmul_kv",
    )(x, g, w)
    return outs[0], outs[1:]


def _out_proj_body(a_ref, w_ref, x_ref, g_ref, o_ref):
    y = _dot(a_ref[...].astype(BF16), w_ref[...])
    ms = jnp.mean(y * y, axis=-1, keepdims=True)
    o_ref[...] = x_ref[...] + y * lax.rsqrt(ms + EPS) * g_ref[...]


def _out_proj(a, w, x, x_map, g, *, m, tm, out_shape, out_map):
    e, d = w.shape
    return pl.pallas_call(
        _out_proj_body,
        grid=(m // tm,),
        in_specs=[
            pl.BlockSpec((tm, e), lambda i: (i, 0)),
            pl.BlockSpec((e, d), lambda i: (0, 0)),
            pl.BlockSpec((tm, d), x_map),
            pl.BlockSpec((1, d), lambda i: (0, 0)),
        ],
        out_specs=pl.BlockSpec((tm, d), out_map),
        out_shape=jax.ShapeDtypeStruct(out_shape, F32),
        compiler_params=_cparams(("parallel",)),
        name="out_proj",
    )(a, w, x, g)


POOL_HALO = 16
POOL_PAD = 8


def _pool_body(x_ref, g_ref, w_ref, buf_ref, wg_ref, bg_ref, sc_ref, a_ref, st_ref, ext, sa, sb,
               *, tt, pos0, gw):
    ti = pl.program_id(1)
    e = ext.shape[1]
    x = x_ref[...]
    ms = jnp.mean(x * x, axis=-1, keepdims=True)
    h = (x * lax.rsqrt(ms + EPS) * g_ref[...]).astype(BF16)
    lo = POOL_PAD
    top = POOL_PAD + POOL_HALO
    rows = top + tt

    @pl.when(ti == 0)
    def _():
        ext[0:lo, :] = jnp.zeros((lo, ext.shape[1]), F32)
        ext[lo:top, :] = buf_ref[0]

    @pl.when(ti > 0)
    def _():
        ext[lo:top, :] = ext[lo + tt:top + tt, :]

    u = _dot(h, w_ref[:, 0:e])
    ext[top:rows, :] = u
    sa[0:lo, :] = jnp.zeros((lo, gw), F32)
    sb[0:lo, :] = jnp.zeros((lo, gw), F32)
    pos = _iota((tt, 1), 0) + (pos0 + ti * tt)
    for g, w in enumerate(POOL_WINDOWS):
        cs = slice(g * gw, (g + 1) * gw)
        sa[lo:rows, :] = ext[lo:rows, cs] + ext[lo - 1:rows - 1, cs]
        cur, oth, sh = sa, sb, 2
        while sh < w:
            oth[lo:rows, :] = cur[lo:rows, :] + cur[lo - sh:rows - sh, :]
            cur, oth, sh = oth, cur, sh * 2
        cnt = jnp.minimum(pos + 1, w).astype(F32)
        mean = cur[top:rows, :] / cnt
        mixed = _dot((mean - u[:, cs]).astype(BF16), wg_ref[g]) + bg_ref[:, cs]
        z = _dot(h, w_ref[:, e + g * gw:e + (g + 1) * gw])
        a_ref[:, cs] = (mixed * sc_ref[:, cs] * _silu(z)).astype(a_ref.dtype)
    st_ref[0] = ext[lo + tt:top + tt, :]


def _pool_mixer(x, g_pre, w_in, buf16, wg, bg, sc, *, b, t, tt, pos0):
    d = x.shape[1]
    e = buf16.shape[-1]
    gw = e // len(POOL_WINDOWS)
    nt = t // tt
    rows = POOL_PAD + POOL_HALO + tt
    return pl.pallas_call(
        functools.partial(_pool_body, tt=tt, pos0=pos0, gw=gw),
        grid=(b, nt),
        in_specs=[
            pl.BlockSpec((tt, d), lambda i, j: (i * nt + j, 0)),
            pl.BlockSpec((1, d), lambda i, j: (0, 0)),
            pl.BlockSpec((d, 2 * e), lambda i, j: (0, 0)),
            pl.BlockSpec((1, POOL_HALO, e), lambda i, j: (i, 0, 0)),
            pl.BlockSpec((len(POOL_WINDOWS), gw, gw), lambda i, j: (0, 0, 0)),
            pl.BlockSpec((1, e), lambda i, j: (0, 0)),
            pl.BlockSpec((1, e), lambda i, j: (0, 0)),
        ],
        out_specs=[
            pl.BlockSpec((tt, e), lambda i, j: (i * nt + j, 0)),
            pl.BlockSpec((1, POOL_HALO, e), lambda i, j: (i, 0, 0)),
        ],
        out_shape=[
            jax.ShapeDtypeStruct((b * t, e), _act_dtype(tt)),
            jax.ShapeDtypeStruct((b, POOL_HALO, e), F32),
        ],
        scratch_shapes=[
            pltpu.VMEM((rows, e), F32),
            pltpu.VMEM((rows, gw), F32),
            pltpu.VMEM((rows, gw), F32),
        ],
        compiler_params=_cparams(("parallel", "arbitrary")),
        name="pool_mixer",
    )(x, g_pre, w_in, buf16, wg, bg, sc)


def _ret_body(lg_ref, q_ref, k_ref, v_ref, g_ref, cos_ref, sin_ref, s0_ref, a_ref, so_ref, s_scr,
              *, c_len, c_pad, dk, dv):
    c = pl.program_id(1)

    @pl.when(c == 0)
    def _():
        s_scr[...] = s0_ref[0]

    cosv = cos_ref[...]
    sinv = sin_ref[...]
    even = (_iota((c_len, dk), 1) & 1) == 0

    def rot(x):
        nxt = pltpu.roll(x, dk - 1, axis=1)
        prv = pltpu.roll(x, 1, axis=1)
        return x * cosv + jnp.where(even, nxt, prv) * sinv

    def pad(x):
        if c_pad == c_len:
            return x
        return jnp.concatenate([x, jnp.zeros((c_pad - c_len, x.shape[1]), x.dtype)], axis=0)

    i_col = _iota((c_pad, 1), 0).astype(F32)
    j_row = _iota((1, c_pad), 1).astype(F32)
    diff = i_col - j_row
    for h in range(R_HEADS):
        ks = slice(h * dk, (h + 1) * dk)
        vs = slice(h * dv, (h + 1) * dv)
        lg = lg_ref[h]
        q = pad(rot(q_ref[:, ks]))
        k = pad(rot(k_ref[:, ks]) * (dk ** -0.5))
        v = pad(v_ref[:, vs])
        intra = jnp.where(diff >= 0, jnp.exp(jnp.maximum(diff, 0.0) * lg), 0.0)
        q_dec = jnp.exp((i_col + 1.0) * lg)
        k_dec = jnp.exp((c_len - 1.0 - i_col) * lg)
        c_dec = jnp.exp(jnp.full((1, 1), float(c_len), F32) * lg)

        qb = q.astype(BF16)
        vb = v.astype(BF16)
        s = _dot_nt(qb, k.astype(BF16)) * intra
        s_old = s_scr[h]
        o = _dot(s.astype(BF16), vb) + _dot(qb, s_old.astype(BF16)) * q_dec
        s_scr[h] = s_old * c_dec + _dot_tn((k * k_dec).astype(BF16), vb)
        o = o[0:c_len]
        o = o * lax.rsqrt(jnp.mean(o * o, axis=-1, keepdims=True) + EPS)
        a_ref[:, vs] = (o * _silu(g_ref[:, vs])).astype(a_ref.dtype)

    @pl.when(c == pl.num_programs(1) - 1)
    def _():
        so_ref[0] = s_scr[...]


def _retention_mixer(proj, cos_t, sin_t, lg, s0, *, b, t, dk, dv):
    c_len = R_CHUNK if t % R_CHUNK == 0 else t
    c_pad = max(c_len, R_CHUNK)
    nc = t // c_len
    nh = R_HEADS
    qk_w, vg_w = nh * dk, nh * dv
    row = lambda i, c: i * nc + c
    return pl.pallas_call(
        functools.partial(_ret_body, c_len=c_len, c_pad=c_pad, dk=dk, dv=dv),
        grid=(b, nc),
        in_specs=[
            pl.BlockSpec(memory_space=pltpu.SMEM),
            pl.BlockSpec((c_len, qk_w), lambda i, c: (row(i, c), 0)),
            pl.BlockSpec((c_len, qk_w), lambda i, c: (row(i, c), 1)),
            pl.BlockSpec((c_len, vg_w), lambda i, c: (row(i, c), 2 * qk_w // vg_w)),
            pl.BlockSpec((c_len, vg_w), lambda i, c: (row(i, c), 2 * qk_w // vg_w + 1)),
            pl.BlockSpec((c_len, dk), lambda i, c: (c, 0)),
            pl.BlockSpec((c_len, dk), lambda i, c: (c, 0)),
            pl.BlockSpec((1, nh, dk, dv), lambda i, c: (i, 0, 0, 0)),
        ],
        out_specs=[
            pl.BlockSpec((c_len, vg_w), lambda i, c: (row(i, c), 0)),
            pl.BlockSpec((1, nh, dk, dv), lambda i, c: (i, 0, 0, 0)),
        ],
        out_shape=[
            jax.ShapeDtypeStruct((b * t, vg_w), _act_dtype(c_len)),
            jax.ShapeDtypeStruct((b, nh, dk, dv), F32),
        ],
        scratch_shapes=[pltpu.VMEM((nh, dk, dv), F32)],
        compiler_params=_cparams(("parallel", "arbitrary")),
        name="retention_mixer",
    )(lg, proj, proj, proj, proj, cos_t, sin_t, s0)


def _lru_body(u_ref, z_ref, buf_ref, cw_ref, cb_ref, wa_ref, ba_ref, wx_ref, bx_ref, lam_ref, h0_ref,
              a_ref, cso_ref, ho_ref, ext, a_s, b_s, h_s, hc, *, tt, nb):
    ti = pl.program_id(1)
    halo = (CONV_W - 1) * nb
    rows = tt * nb

    @pl.when(ti == 0)
    def _():
        ext[0:halo, :] = buf_ref[...]
        hc[...] = h0_ref[...]

    @pl.when(ti > 0)
    def _():
        ext[0:halo, :] = ext[rows:rows + halo, :]

    ext[halo:halo + rows, :] = u_ref[...]
    cw = cw_ref[...]
    c = cb_ref[...] + ext[0:rows, :] * cw[0:1, :]
    for kk in range(1, CONV_W):
        c = c + ext[kk * nb:kk * nb + rows, :] * cw[kk:kk + 1, :]
    cb16 = c.astype(BF16)
    r = jax.nn.sigmoid(_dot(cb16, wa_ref[0]) + ba_ref[...])
    ig = jax.nn.sigmoid(_dot(cb16, wx_ref[0]) + bx_ref[...])
    nl = -lam_ref[...]
    softplus = jnp.maximum(nl, 0.0) + jnp.log1p(jnp.exp(-jnp.abs(nl)))
    log_a = (-LRU_C) * r * softplus
    a = jnp.exp(log_a)
    a_s[...] = a
    b_s[...] = jnp.sqrt(1.0 - a * a) * (ig * c)

    def step(t, h):
        rs = pl.ds(pl.multiple_of(t * nb, SUBLANES), nb)
        h = a_s[rs, :] * h + b_s[rs, :]
        h_s[rs, :] = h
        return h

    hc[...] = lax.fori_loop(0, tt, step, hc[...], unroll=8)
    a_ref[...] = (h_s[...] * _silu(z_ref[...])).astype(BF16)
    cso_ref[...] = ext[rows:rows + halo, :]
    ho_ref[...] = hc[...]


def _lru_mixer(uz, buf, cw, cb, wa, ba, wx, bx, lam, h0, *, nb, t, tt):
    d_rnn = h0.shape[-1]
    nblk = d_rnn // LRU_BW
    nt = t // tt
    rows = tt * nb
    halo = (CONV_W - 1) * nb
    vec = lambda: pl.BlockSpec((1, LRU_BW), lambda j, i: (0, j))
    return pl.pallas_call(
        functools.partial(_lru_body, tt=tt, nb=nb),
        grid=(nblk, nt),
        in_specs=[
            pl.BlockSpec((rows, LRU_BW), lambda j, i: (i, j)),
            pl.BlockSpec((rows, LRU_BW), lambda j, i: (i, nblk + j)),
            pl.BlockSpec((halo, LRU_BW), lambda j, i: (0, j)),
            pl.BlockSpec((CONV_W, LRU_BW), lambda j, i: (0, j)),
            vec(),
            pl.BlockSpec((1, LRU_BW, LRU_BW), lambda j, i: (j, 0, 0)),
            vec(),
            pl.BlockSpec((1, LRU_BW, LRU_BW), lambda j, i: (j, 0, 0)),
            vec(),
            vec(),
            pl.BlockSpec((nb, LRU_BW), lambda j, i: (0, j)),
        ],
        out_specs=[
            pl.BlockSpec((rows, LRU_BW), lambda j, i: (i, j)),
            pl.BlockSpec((halo, LRU_BW), lambda j, i: (0, j)),
            pl.BlockSpec((nb, LRU_BW), lambda j, i: (0, j)),
        ],
        out_shape=[
            jax.ShapeDtypeStruct((t * nb, d_rnn), BF16),
            jax.ShapeDtypeStruct((halo, d_rnn), F32),
            jax.ShapeDtypeStruct((nb, d_rnn), F32),
        ],
        scratch_shapes=[
            pltpu.VMEM((halo + rows, LRU_BW), F32),
            pltpu.VMEM((rows, LRU_BW), F32),
            pltpu.VMEM((rows, LRU_BW), F32),
            pltpu.VMEM((rows, LRU_BW), F32),
            pltpu.VMEM((nb, LRU_BW), F32),
        ],
        compiler_params=_cparams(("parallel", "arbitrary")),
        name="lru_mixer",
    )(uz, uz, buf, cw, cb, wa, ba, wx, bx, lam, h0)


def _softmax_masked(s, mask):
    sm = jnp.where(mask, s, NEG)
    m = jnp.max(sm, axis=-1, keepdims=True)
    e = jnp.where(mask, jnp.exp(sm - m), 0.0)
    l = jnp.sum(e, axis=-1, keepdims=True)
    return e / jnp.where(l > 0.0, l, 1.0)


def _dot_f32_by_01(x, m01):
    hi = x.astype(BF16)
    r1 = x - hi.astype(F32)
    mid = r1.astype(BF16)
    lo = (r1 - mid.astype(F32)).astype(BF16)
    return _dot(hi, m01) + _dot(mid, m01) + _dot(lo, m01)


def _block_importance_matrix(n_rows, n_blk_lanes, n_blk, rows_per_cmp=1):
    r, s = L_SEL // S_CMP, L_CMP // S_CMP
    i = lax.shift_right_logical(_iota((n_rows, n_blk_lanes), 0), int(math.log2(rows_per_cmp)))
    j = _iota((n_rows, n_blk_lanes), 1)
    hit = (i >= r * j - (s - 1)) & (i <= r * j + (r - 1)) & (j < n_blk)
    return jnp.where(hit, 1.0, 0.0).astype(BF16)


def _select_blocks(imp, pos, n_blk):
    j = _iota(imp.shape, 1)
    cur = lax.shift_right_logical(pos, int(math.log2(L_SEL)))
    valid = j * L_SEL <= pos
    forced = (j == 0) | (j == cur) | (j == cur - 1)
    score = jnp.where(forced, FORCE, jnp.where(valid, imp, -1.0))
    cnt = jnp.zeros(imp.shape, F32)
    for i in range(n_blk):
        si = score[:, i:i + 1]
        ahead = (si > score) | ((si == score) & (j > i))
        cnt = cnt + jnp.where(ahead, 1.0, 0.0)
    keep = (cnt < float(min(N_SEL, n_blk))) & (score >= 0.0) & (j < n_blk)
    return jnp.where(keep, 1.0, 0.0)


def _select_blocks_t(imp, pos_row, n_blk):
    nb8 = -(-n_blk // SUBLANES) * SUBLANES
    imp_t = imp.T[0:nb8, :]
    j = _iota(imp_t.shape, 0)
    cur = lax.shift_right_logical(pos_row, int(math.log2(L_SEL)))
    valid = j * L_SEL <= pos_row
    forced = (j == 0) | (j == cur) | (j == cur - 1)
    score = jnp.where(forced, FORCE, jnp.where(valid, imp_t, -1.0))
    cnt = jnp.zeros(imp_t.shape, F32)
    for i in range(n_blk):
        si = score[i:i + 1, :]
        ahead = (si > score) | ((si == score) & (j > i))
        cnt = cnt + jnp.where(ahead, 1.0, 0.0)
    keep = (cnt < float(min(N_SEL, n_blk))) & (score >= 0.0) & (j < n_blk)
    sel_t = jnp.where(keep, 1.0, 0.0)
    if nb8 < LANES:
        sel_t = jnp.concatenate([sel_t, jnp.zeros((LANES - nb8, imp.shape[0]), F32)], axis=0)
    return sel_t.T


def _online_update(m_ref, l_ref, acc_ref, s, mask, pv):
    sm = jnp.where(mask, s, NEG)
    m_old = m_ref[...]
    m_new = jnp.maximum(m_old, jnp.max(sm, axis=-1, keepdims=True))
    p = jnp.where(mask, jnp.exp(sm - m_new), 0.0)
    alpha = jnp.exp(m_old - m_new)
    l_ref[...] = alpha * l_ref[...] + jnp.sum(p, axis=-1, keepdims=True)
    acc_ref[...] = alpha * acc_ref[...] + pv(p)
    m_ref[...] = m_new


def _online_finish(l_ref, acc_ref):
    l = l_ref[...]
    return acc_ref[...] / jnp.where(l > 0.0, l, 1.0)


def _cmp_accumulate(page, w2_ref, acc_a, acc_b, row0):
    pb = page.astype(BF16)
    for g in range(N_KV):
        cs = slice(g * HEAD_DIM, (g + 1) * HEAD_DIM)
        ab = _dot(w2_ref[g], pb[:, cs])
        acc_a[pl.ds(row0, SUB_PER_PAGE), cs] = ab[0:SUB_PER_PAGE]
        acc_b[pl.ds(row0, SUB_PER_PAGE), cs] = ab[SUB_PER_PAGE:2 * SUB_PER_PAGE]


def _cmp_finish(acc_a, acc_b, lin_ref, out_ref, ncp):
    pooled = acc_a[0:ncp, :] + acc_b[1:ncp + 1, :]
    for g in range(N_KV):
        cs = slice(g * HEAD_DIM, (g + 1) * HEAD_DIM)
        out_ref[0, :, cs] = _dot(pooled[:, cs].astype(BF16), lin_ref[g])


def _compress_rows_body(k_ref, v_ref, w2k_ref, w2v_ref, lk_ref, lv_ref, ok_ref, ov_ref,
                        aak, abk, aav, abv, *, pages, ncp):
    s = pl.program_id(1)

    @pl.when(s == 0)
    def _():
        tail = jnp.zeros((SUBLANES, KV_W), F32)
        abk[ncp:ncp + SUBLANES, :] = tail
        abv[ncp:ncp + SUBLANES, :] = tail

    for i in range(pages):
        row0 = pl.multiple_of((s * pages + i) * SUB_PER_PAGE, SUBLANES)
        rs = slice(i * PAGE_SIZE, (i + 1) * PAGE_SIZE)
        _cmp_accumulate(k_ref[0, rs, :], w2k_ref, aak, abk, row0)
        _cmp_accumulate(v_ref[0, rs, :], w2v_ref, aav, abv, row0)

    @pl.when(s == pl.num_programs(1) - 1)
    def _():
        _cmp_finish(aak, abk, lk_ref, ok_ref, ncp)
        _cmp_finish(aav, abv, lv_ref, ov_ref, ncp)


def _compress_rows(kv, w2k, w2v, lk, lv, *, b, t):
    pages = min(PAGES_PER_STEP, t // PAGE_SIZE)
    rows = pages * PAGE_SIZE
    ns = t // rows
    ncp = t // S_CMP
    wspec = lambda shp: pl.BlockSpec(shp, lambda i, s: (0, 0, 0))
    acc = lambda: pltpu.VMEM((ncp + SUBLANES, KV_W), F32)
    return pl.pallas_call(
        functools.partial(_compress_rows_body, pages=pages, ncp=ncp),
        grid=(b, ns),
        in_specs=[
            pl.BlockSpec((1, rows, KV_W), lambda i, s: (0, i * ns + s, 0)),
            pl.BlockSpec((1, rows, KV_W), lambda i, s: (1, i * ns + s, 0)),
            wspec(w2k.shape), wspec(w2v.shape), wspec(lk.shape), wspec(lv.shape),
        ],
        out_specs=[
            pl.BlockSpec((1, ncp, KV_W), lambda i, s: (i, 0, 0)),
            pl.BlockSpec((1, ncp, KV_W), lambda i, s: (i, 0, 0)),
        ],
        out_shape=[jax.ShapeDtypeStruct((b, ncp, KV_W), F32)] * 2,
        scratch_shapes=[acc(), acc(), acc(), acc()],
        compiler_params=_cparams(("parallel", "arbitrary")),
        name="compress_rows",
    )(kv, kv, w2k, w2v, lk, lv)


PAGE_ROWS = PAGE_SIZE * N_KV
SUB_ROWS = SUB_PER_PAGE * N_KV


def _cmp_accumulate_rg(page, w2_ref, acc_a, acc_b, row0):
    ab = _dot(w2_ref[...], page.astype(BF16))
    acc_a[pl.ds(row0, SUB_ROWS), :] = ab[0:SUB_ROWS]
    acc_b[pl.ds(row0, SUB_ROWS), :] = ab[SUB_ROWS:2 * SUB_ROWS]


def _cmp_finish_rg(acc_a, acc_b, lin_ref, out_ref, ncp):
    for g in range(N_KV):
        pooled = acc_a[pl.ds(g, ncp, stride=N_KV), :] + acc_b[pl.ds(N_KV + g, ncp, stride=N_KV), :]
        out_ref[0, :, g * HEAD_DIM:(g + 1) * HEAD_DIM] = _dot(pooled.astype(BF16), lin_ref[g])


def _compress_paged_body(pt_ref, *refs, pages, n_pool_steps, ncp):
    pk = refs[0:pages]
    pv = refs[pages:2 * pages]
    nk_ref, nv_ref, w2k_ref, w2v_ref, lk_ref, lv_ref, ok_ref, ov_ref, aak, abk, aav, abv = refs[2 * pages:]
    s = pl.program_id(1)

    @pl.when(s < n_pool_steps)
    def _():
        for i in range(pages):
            row0 = pl.multiple_of((s * pages + i) * SUB_ROWS, SUB_ROWS)
            _cmp_accumulate_rg(pk[i][...], w2k_ref, aak, abk, row0)
            _cmp_accumulate_rg(pv[i][...], w2v_ref, aav, abv, row0)

    @pl.when(s == n_pool_steps)
    def _():
        row0 = n_pool_steps * pages * SUB_ROWS
        _cmp_accumulate_rg(nk_ref[0, 0], w2k_ref, aak, abk, row0)
        _cmp_accumulate_rg(nv_ref[0, 0], w2v_ref, aav, abv, row0)
        _cmp_finish_rg(aak, abk, lk_ref, ok_ref, ncp)
        _cmp_finish_rg(aav, abv, lv_ref, ov_ref, ncp)


def _page_spec(i, pages, n_pool_steps):
    def index(bi, s, pt):
        return (pt[bi, jnp.minimum(s, n_pool_steps - 1) * pages + i], 0)
    return pl.BlockSpec((PAGE_ROWS, HEAD_DIM), index)


def _compress_paged(page_table, pool_k, pool_v, new_pages, w2k, w2v, lk, lv):
    b, n_pages = page_table.shape
    pages = PAGES_PER_STEP
    n_pool_steps = n_pages // pages
    ncp = n_pages * SUB_PER_PAGE
    wspec = lambda shp: pl.BlockSpec(shp, lambda bi, s, pt: (0,) * len(shp))
    newspec = lambda idx: pl.BlockSpec((1, 1, PAGE_ROWS, HEAD_DIM), lambda bi, s, pt: (idx, bi, 0, 0))
    acc = lambda: pltpu.VMEM(((ncp + SUB_PER_PAGE) * N_KV, HEAD_DIM), F32)
    grid_spec = pltpu.PrefetchScalarGridSpec(
        num_scalar_prefetch=1,
        grid=(b, n_pool_steps + 1),
        in_specs=(
            [_page_spec(i, pages, n_pool_steps) for i in range(pages)] * 2
            + [newspec(0), newspec(1), wspec(w2k.shape), wspec(w2v.shape), wspec(lk.shape), wspec(lv.shape)]
        ),
        out_specs=[
            pl.BlockSpec((1, ncp, KV_W), lambda bi, s, pt: (bi, 0, 0)),
            pl.BlockSpec((1, ncp, KV_W), lambda bi, s, pt: (bi, 0, 0)),
        ],
        scratch_shapes=[acc(), acc(), acc(), acc()],
    )
    return pl.pallas_call(
        functools.partial(_compress_paged_body, pages=pages, n_pool_steps=n_pool_steps, ncp=ncp),
        grid_spec=grid_spec,
        out_shape=[jax.ShapeDtypeStruct((b, ncp, KV_W), F32)] * 2,
        compiler_params=_cparams(("parallel", "arbitrary")),
        name="compress_paged",
    )(page_table, *([pool_k] * pages), *([pool_v] * pages), new_pages, new_pages, w2k, w2v, lk, lv)


LOG2E = 1.4426950408889634


def _lane_fold(x, op):
    out = x[:, 0:LANES]
    for u in range(1, x.shape[1] // LANES):
        out = op(out, x[:, u * LANES:(u + 1) * LANES])
    return out


def _nsa_prompt_body(q_ref, z_ref, gt_ref, ck_ref, cv_ref, ks_ref, vs_ref, kw_ref, vw_ref, a_ref,
                     ksb, vsb, kwb, vwb, s_buf, mx_s, acc_s, *, tq, t, kc, n_cmp):
    ones_col = jnp.where(_iota((t, LANES), 1) == 0, 1.0, 0.0).astype(BF16)
    ksb[...] = ks_ref[0].astype(BF16)
    vsb[...] = jnp.concatenate([vs_ref[0].astype(BF16), ones_col], axis=1)
    kwb[...] = kw_ref[0].astype(BF16)
    vwb[...] = jnp.concatenate([vw_ref[0].astype(BF16), ones_col], axis=1)

    n_blk = -(-t // L_SEL)
    ncl = ck_ref.shape[1]
    scale = HEAD_DIM ** -0.5
    c_exp = scale * LOG2E
    hs = [slice(h * HEAD_DIM, (h + 1) * HEAD_DIM) for h in range(HPG)]
    hr = [slice(h * tq, (h + 1) * tq) for h in range(HPG)]
    ckb = ck_ref[0].astype(BF16)
    cvb = cv_ref[0].astype(BF16)
    msel = _block_importance_matrix(ncl, LANES, n_blk)
    ws = min(WINDOW + tq, t)
    blk_shift = int(math.log2(L_SEL))

    def q_tile(qi, carry):
        q0 = pl.multiple_of(qi * tq, tq)
        rows = pl.ds(q0, tq)
        pos = q0 + _iota((tq, 1), 0)
        qb = jnp.concatenate([q_ref[rows, hs[h]] for h in range(HPG)], axis=0).astype(BF16)

        ci = _iota((1, ncl), 1)
        cmask = (ci * S_CMP + (L_CMP - 1) <= pos) & (ci < n_cmp)
        s_c = _dot_nt(qb, ckb) * scale
        p_c = [_softmax_masked(s_c[hr[h]], cmask) for h in range(HPG)]
        imp_c = p_c[0] + p_c[1] + p_c[2] + p_c[3]
        o_c = _dot(jnp.concatenate(p_c, axis=0).astype(BF16), cvb)

        imp = _dot_f32_by_01(imp_c, msel)
        selb = _select_blocks_t(imp, q0 + _iota((1, tq), 1), n_blk).astype(BF16)

        w0 = pl.multiple_of(jnp.minimum(jnp.maximum(q0 - WINDOW, 0), t - ws), tq)
        wpos = w0 + _iota((1, ws), 1)
        wbias = jnp.where((wpos <= pos) & (wpos > pos - WINDOW), 0.0, NEG)
        s_w = _dot_nt(qb, kwb[pl.ds(w0, ws), :])
        e_w = []
        for h in range(HPG):
            sh = s_w[hr[h]] + wbias
            e_w.append(jnp.exp2((sh - jnp.max(sh, axis=-1, keepdims=True)) * c_exp))
        pv_w = _dot(jnp.concatenate(e_w, axis=0).astype(BF16), vwb[pl.ds(w0, ws), :])
        o_w = pv_w[:, 0:HEAD_DIM] / pv_w[:, HEAD_DIM:HEAD_DIM + 1]

        mx_s[...] = jnp.full(mx_s.shape, NEG, F32)

        def pass1(ki, c):
            k0 = pl.multiple_of(ki * kc, kc)
            kpos = k0 + _iota((1, kc), 1)
            blk = lax.shift_right_logical(k0 + _iota((LANES, kc), 1), blk_shift)
            expand = jnp.where(blk == _iota((LANES, kc), 0), 1.0, 0.0).astype(BF16)
            bias = (_dot(selb, expand) - 1.0) * (-NEG)
            bias = jnp.where(kpos <= pos, bias, NEG)
            s = _dot_nt(qb, ksb[pl.ds(k0, kc), :])
            for h in range(HPG):
                sh = s[hr[h]] + bias
                s_buf[ki, hr[h], :] = sh
                mx_s[hr[h], :] = jnp.maximum(mx_s[hr[h], :], _lane_fold(sh, jnp.maximum))
            return c

        n_chunks = (q0 + tq + kc - 1) // kc
        lax.fori_loop(0, n_chunks, pass1, 0)
        m_b = jnp.broadcast_to(jnp.max(mx_s[...], axis=-1, keepdims=True), mx_s.shape)
        mx_s[...] = m_b
        acc_s[...] = jnp.zeros(acc_s.shape, F32)

        def pass2(ki, c):
            k0 = pl.multiple_of(ki * kc, kc)
            mb = mx_s[...]
            e = [jnp.exp2((s_buf[ki, :, u * LANES:(u + 1) * LANES] - mb) * c_exp) for u in range(kc // LANES)]
            acc_s[...] = acc_s[...] + _dot(jnp.concatenate(e, axis=1).astype(BF16), vsb[pl.ds(k0, kc), :])
            return c

        lax.fori_loop(0, n_chunks, pass2, 0)
        o_s = acc_s[:, 0:HEAD_DIM] / acc_s[:, HEAD_DIM:HEAD_DIM + 1]

        gates = jax.nn.sigmoid(gt_ref[rows, :])
        for h in range(HPG):
            o = (gates[:, h:h + 1] * o_c[hr[h]] + gates[:, HPG + h:HPG + h + 1] * o_s[hr[h]]
                 + gates[:, 2 * HPG + h:2 * HPG + h + 1] * o_w[hr[h]])
            a_ref[rows, hs[h]] = (o * _silu(z_ref[rows, hs[h]])).astype(BF16)
        return carry

    lax.fori_loop(0, t // tq, q_tile, 0)


def _nsa_prompt_attend(qz, gates, ck, cv, kv, *, b, t, tq, kc):
    gq = HPG * HEAD_DIM
    n_cmp = t // S_CMP - (L_CMP // S_CMP) + 1
    ncl = ck.shape[1]
    kvspec = lambda idx: pl.BlockSpec((1, t, HEAD_DIM), lambda i, g: (idx, i, g))
    kb = lambda: pltpu.VMEM((t, HEAD_DIM), BF16)
    vb = lambda: pltpu.VMEM((t, HEAD_DIM + LANES), BF16)
    return pl.pallas_call(
        functools.partial(_nsa_prompt_body, tq=tq, t=t, kc=kc, n_cmp=n_cmp),
        grid=(b, N_KV),
        in_specs=[
            pl.BlockSpec((t, gq), lambda i, g: (i, g)),
            pl.BlockSpec((t, gq), lambda i, g: (i, N_KV + g)),
            pl.BlockSpec((t, LANES), lambda i, g: (i, g)),
            pl.BlockSpec((1, ncl, HEAD_DIM), lambda i, g: (i, 0, g)),
            pl.BlockSpec((1, ncl, HEAD_DIM), lambda i, g: (i, 0, g)),
            kvspec(2), kvspec(3), kvspec(4), kvspec(5),
        ],
        out_specs=pl.BlockSpec((t, gq), lambda i, g: (i, g)),
        out_shape=jax.ShapeDtypeStruct((b * t, N_KV * gq), BF16),
        scratch_shapes=[
            kb(), vb(), kb(), vb(),
            pltpu.VMEM((t // kc, HPG * tq, kc), F32),
            pltpu.VMEM((HPG * tq, LANES), F32),
            pltpu.VMEM((HPG * tq, HEAD_DIM + LANES), F32),
        ],
        compiler_params=_cparams(("parallel", "parallel")),
        name="nsa_prompt_attend",
    )(qz, qz, gates, ck, cv, kv, kv, kv, kv)


def _rows_by_group(ref, n):
    return jnp.concatenate([ref[pl.ds(g, n, stride=N_KV), :] for g in range(N_KV)], axis=1)


def _diag_blocks(x, rows_per_group):
    return jnp.concatenate(
        [x[g * rows_per_group:(g + 1) * rows_per_group, g * HEAD_DIM:(g + 1) * HEAD_DIM]
         for g in range(N_KV)], axis=0)


def _pad_rows(x, n):
    return jnp.concatenate([x, jnp.zeros((n - x.shape[0], x.shape[1]), x.dtype)], axis=0)


def _nsa_sample_body(pt_ref, *refs, pages, n_pool_steps, past, n_new, wb):
    sk = refs[0:pages]
    sv = refs[pages:2 * pages]
    (nks_ref, nvs_ref, nkw_ref, nvw_ref, nkw_rg, nvw_rg, ck_ref, cv_ref, wk_ref, wv_ref, q_ref, z_ref, gt_ref,
     a_ref, wko_ref, wvo_ref, q2, selr, m_s, l_s, acc, oc, ow) = refs[2 * pages:]
    s = pl.program_id(1)
    rows = N_HEADS * n_new
    rpg = HPG * n_new
    scale = HEAD_DIM ** -0.5
    n_blk = past // L_SEL + -(-n_new // L_SEL)
    n_cmp = ck_ref.shape[1]
    sel_lanes = selr.shape[1]
    pos = past + (_iota((rows, 1), 0) % n_new)
    new_page = lambda ref: _pad_rows(ref[0], PAGE_SIZE).astype(BF16)
    new_mask = (past + _iota((1, PAGE_SIZE), 1) <= pos) & (_iota((1, PAGE_SIZE), 1) < n_new)

    @pl.when(s == 0)
    def _():
        q2[...] = jnp.zeros(q2.shape, F32)
        for hh in range(N_HEADS):
            g = hh // HPG
            q2[hh * n_new:(hh + 1) * n_new, g * HEAD_DIM:(g + 1) * HEAD_DIM] = (
                q_ref[:, hh * HEAD_DIM:(hh + 1) * HEAD_DIM])
        qb = q2[...].astype(BF16)

        ci = _iota((1, n_cmp), 1)
        cmask = ci * S_CMP + (L_CMP - 1) <= pos
        p_c = _softmax_masked(_dot_nt(qb, ck_ref[0].astype(BF16)) * scale, cmask)
        oc[...] = _diag_blocks(_dot(p_c.astype(BF16), cv_ref[0].astype(BF16)), rpg)

        imp_c = jnp.concatenate(
            [sum(p_c[(g * HPG + h) * n_new:(g * HPG + h + 1) * n_new, :] for h in range(HPG))
             for g in range(N_KV)], axis=0)
        imp = _dot_f32_by_01(imp_c, _block_importance_matrix(n_cmp, sel_lanes, n_blk))
        pos_gt = past + (_iota((N_KV * n_new, 1), 0) % n_new)
        sel = _select_blocks(imp, pos_gt, n_blk)
        selr[...] = jnp.concatenate(
            [sel[(hh // HPG) * n_new:(hh // HPG + 1) * n_new, :] for hh in range(N_HEADS)], axis=0)

        wpos = (past - wb) + _iota((1, wb), 1)
        mask = jnp.concatenate([(wpos <= pos) & (wpos > pos - WINDOW), new_mask], axis=1)
        wk = _rows_by_group(wk_ref.at[0], wb).astype(BF16)
        wv = _rows_by_group(wv_ref.at[0], wb).astype(BF16)
        sc = jnp.concatenate([_dot_nt(qb, wk), _dot_nt(qb, new_page(nkw_ref))], axis=1) * scale
        p_w = _softmax_masked(sc, mask)
        ow[...] = _diag_blocks(
            _dot(p_w[:, 0:wb].astype(BF16), wv) + _dot(p_w[:, wb:].astype(BF16), new_page(nvw_ref)), rpg)

        keep = (wb - n_new) * N_KV
        wko_ref[0, 0:keep, :] = wk_ref[0, n_new * N_KV:wb * N_KV, :]
        wko_ref[0, keep:wb * N_KV, :] = nkw_rg[...]
        wvo_ref[0, 0:keep, :] = wv_ref[0, n_new * N_KV:wb * N_KV, :]
        wvo_ref[0, keep:wb * N_KV, :] = nvw_rg[...]

        m_s[...] = jnp.full(m_s.shape, NEG, F32)
        l_s[...] = jnp.zeros(l_s.shape, F32)
        acc[...] = jnp.zeros(acc.shape, F32)

    @pl.when(s < n_pool_steps)
    def _():
        qb = q2[...].astype(BF16)
        nk = pages * PAGE_SIZE
        sc = jnp.concatenate(
            [_dot_nt(qb, _rows_by_group(sk[i], PAGE_SIZE).astype(BF16)) for i in range(pages)], axis=1) * scale
        blk = s * (nk // L_SEL) + lax.shift_right_logical(_iota((sel_lanes, nk), 1), int(math.log2(L_SEL)))
        expand = jnp.where(blk == _iota((sel_lanes, nk), 0), 1.0, 0.0).astype(BF16)
        mask = _dot(selr[...].astype(BF16), expand) > 0.5

        def pv(p):
            return sum(_dot(p[:, i * PAGE_SIZE:(i + 1) * PAGE_SIZE].astype(BF16),
                            _rows_by_group(sv[i], PAGE_SIZE).astype(BF16)) for i in range(pages))

        _online_update(m_s, l_s, acc, sc, mask, pv)

    @pl.when(s == n_pool_steps)
    def _():
        qb = q2[...].astype(BF16)
        new_blk = past // L_SEL
        mask = (selr[:, new_blk:new_blk + 1] > 0.5) & new_mask
        _online_update(m_s, l_s, acc, _dot_nt(qb, new_page(nks_ref)) * scale, mask,
                       lambda p: _dot(p.astype(BF16), new_page(nvs_ref)))
        o_s = _diag_blocks(_online_finish(l_s, acc), rpg)

        gates = jax.nn.sigmoid(gt_ref[...])

        def gate_col(br):
            return jnp.concatenate(
                [gates[:, (hh // HPG) * LANES + br * HPG + hh % HPG:(hh // HPG) * LANES + br * HPG + hh % HPG + 1]
                 for hh in range(N_HEADS)], axis=0)

        zr = jnp.concatenate([z_ref[:, hh * HEAD_DIM:(hh + 1) * HEAD_DIM] for hh in range(N_HEADS)], axis=0)
        o = gate_col(0) * oc[...] + gate_col(1) * o_s + gate_col(2) * ow[...]
        res = o * _silu(zr)
        for hh in range(N_HEADS):
            a_ref[:, hh * HEAD_DIM:(hh + 1) * HEAD_DIM] = res[hh * n_new:(hh + 1) * n_new, :]


def _nsa_sample_attend(page_table, pool_k, pool_v, kv, kw_rg, vw_rg, ck, cv, win_k, win_v, qz, gates, *, n_new):
    b, n_pages = page_table.shape
    pages = PAGES_PER_STEP
    n_pool_steps = n_pages // pages
    past = n_pages * PAGE_SIZE
    wb = win_k.shape[1] // N_KV
    rows = N_HEADS * n_new
    qw = N_HEADS * HEAD_DIM
    n_blk = past // L_SEL + -(-n_new // L_SEL)
    sel_lanes = -(-n_blk // LANES) * LANES
    newspec = lambda idx: pl.BlockSpec((1, n_new, KV_W), lambda bi, s, pt: (idx, bi, 0))
    newspec_rg = lambda: pl.BlockSpec((n_new * N_KV, HEAD_DIM), lambda bi, s, pt: (bi, 0))
    full = lambda arr: pl.BlockSpec((1,) + arr.shape[1:], lambda bi, s, pt: (bi, 0, 0))
    grid_spec = pltpu.PrefetchScalarGridSpec(
        num_scalar_prefetch=1,
        grid=(b, n_pool_steps + 1),
        in_specs=(
            [_page_spec(i, pages, n_pool_steps) for i in range(pages)] * 2
            + [newspec(2), newspec(3), newspec(4), newspec(5), newspec_rg(), newspec_rg(),
               full(ck), full(cv), full(win_k), full(win_v),
               pl.BlockSpec((n_new, qw), lambda bi, s, pt: (bi, 0)),
               pl.BlockSpec((n_new, qw), lambda bi, s, pt: (bi, 1)),
               pl.BlockSpec((n_new, N_KV * LANES), lambda bi, s, pt: (bi, 0))]
        ),
        out_specs=[
            pl.BlockSpec((n_new, qw), lambda bi, s, pt: (bi, 0)),
            full(win_k), full(win_v),
        ],
        scratch_shapes=[
            pltpu.VMEM((rows, KV_W), F32),
            pltpu.VMEM((rows, sel_lanes), F32),
            pltpu.VMEM((rows, 1), F32),
            pltpu.VMEM((rows, 1), F32),
            pltpu.VMEM((rows, KV_W), F32),
            pltpu.VMEM((rows, HEAD_DIM), F32),
            pltpu.VMEM((rows, HEAD_DIM), F32),
        ],
    )
    body = functools.partial(_nsa_sample_body, pages=pages, n_pool_steps=n_pool_steps, past=past,
                             n_new=n_new, wb=wb)

    return pl.pallas_call(
        body,
        grid_spec=grid_spec,
        out_shape=[
            jax.ShapeDtypeStruct((b * n_new, qw), F32),
            jax.ShapeDtypeStruct(win_k.shape, F32),
            jax.ShapeDtypeStruct(win_v.shape, F32),
        ],
        compiler_params=_cparams(("parallel", "arbitrary")),
        name="nsa_sample_attend",
    )(page_table, *([pool_k] * pages), *([pool_v] * pages), kv, kv, kv, kv, kw_rg, vw_rg,
      ck, cv, win_k, win_v, qz, qz, gates)


def _rope_tables(pos, dk):
    half = dk // 2
    inv = 1.0 / (ROPE_BASE ** jnp.linspace(0.0, 1.0, half))
    ang = pos.astype(F32)[:, None] * inv[None, :]
    cos = jnp.repeat(jnp.cos(ang), 2, axis=1)
    sin = jnp.stack([-jnp.sin(ang), jnp.sin(ang)], axis=-1).reshape(pos.shape[0], dk)
    return cos, sin


def _pooling_matrix(w_pos):
    eye = jnp.eye(SUB_PER_PAGE, dtype=F32)
    first = jnp.einsum("mn,lg->gmnl", eye, w_pos[:S_CMP]).reshape(N_KV, SUB_PER_PAGE, PAGE_SIZE)
    second = jnp.einsum("mn,lg->gmnl", eye, w_pos[S_CMP:]).reshape(N_KV, SUB_PER_PAGE, PAGE_SIZE)
    return jnp.concatenate([first, second], axis=1).astype(BF16)


def _pooling_matrix_rg(w_pos):
    band = _pooling_matrix(w_pos).astype(F32).reshape(N_KV, 2, SUB_PER_PAGE, PAGE_SIZE)
    full = jnp.einsum("ghmr,gk->hmgrk", band, jnp.eye(N_KV, dtype=F32))
    return full.reshape(2 * SUB_ROWS, PAGE_ROWS).astype(BF16)


def _split_nsa_w_in(w):
    qw = N_HEADS * HEAD_DIM
    ng = 3 * N_HEADS
    d = w.shape[0]
    w_qz = w[:, :2 * qw]
    w_g = w[:, 2 * qw:2 * qw + ng].reshape(d, 3, N_KV, HPG).transpose(0, 2, 1, 3).reshape(d, N_KV, 3 * HPG)
    w_g = jnp.pad(w_g, ((0, 0), (0, 0), (0, LANES - 3 * HPG))).reshape(d, N_KV * LANES)
    w_kv = w[:, 2 * qw + ng:]
    return w_qz.astype(BF16), w_g.astype(BF16), w_kv.astype(BF16)


def _tile_rows(m, pref):
    return pref if m % pref == 0 else m


def _run_group(x, *, pos0, pool_buf, ret_s0, conv_buf, lru_h0, nsa_cache, params):
    nb, t, d = x.shape
    m = nb * t
    tm = _tile_rows(m, OUT_ROWS)
    tm_in = _tile_rows(m, IN_ROWS)
    tm_kv = _tile_rows(m, KV_ROWS)
    fused_layout = all(t % rows == 0 for rows in (tm, tm_in, tm_kv))
    nt = t // tm if fused_layout else 1
    bm_in = lambda rows: (lambda i, j: (i, 0))
    bm_io = lambda i: (i, 0)
    tb_in = lambda rows: (lambda i, j: (i % (t // rows), i // (t // rows)))
    tb_io = lambda i: (i % nt, i // nt)
    pos = pos0 + jnp.arange(t, dtype=jnp.int32)
    vec = lambda v: v.reshape(1, -1)

    depth = params["norm_pre"].shape[0]
    cur = x.reshape(m, d)
    cur_is_tm = False
    states = {k: [] for k in ("pool", "ret", "conv", "lru", "nsa")}

    def project(xv, x_in, g_pre, w):
        n = w.shape[1]
        tn = IN_COLS if n % IN_COLS == 0 else n
        return _norm_matmul(xv, x_in(tm_in), g_pre, w, m=m, tm=tm_in, tn=tn, out_shape=(m, n),
                            out_block=(tm_in, tn), out_map=lambda a, b_: (a, b_))

    def to_bm_view(a):
        return (a.reshape(t, nb * d), tb_in, tb_io) if fused_layout else (a, bm_in, bm_io)

    for i in range(depth):
        kind, j = i % 4, i // 4
        g_pre = vec(params["norm_pre"][i])
        g_post = vec(params["norm_post"][i])
        if kind == 0:
            w_in = params["pool_w_in"][j].astype(BF16)
            buf16 = jnp.pad(pool_buf[j], ((0, 0), (POOL_HALO - pool_buf[j].shape[1], 0), (0, 0)))
            a, st = _pool_mixer(cur, g_pre, w_in, buf16, params["pool_w_grp"][j].astype(BF16),
                                vec(params["pool_b_grp"][j]), vec(params["pool_scale"][j]),
                                b=nb, t=t, tt=_tile_rows(t, POOL_ROWS), pos0=pos0)
            states["pool"].append(st[:, 1:, :])
            cur = _out_proj(a, params["pool_w_out"][j].astype(BF16), cur, bm_io, g_post, m=m, tm=tm,
                            out_shape=(m, d), out_map=bm_io)
        elif kind == 1:
            w_in = params["ret_w_in"][j].astype(BF16)
            dk = d // R_HEADS
            dv = 2 * dk
            proj = project(cur, bm_in, g_pre, w_in)
            cos_t, sin_t = _rope_tables(pos, dk)
            lg = jnp.log(1.0 - 2.0 ** (-5.0 - jnp.arange(R_HEADS, dtype=F32)))
            a, s_new = _retention_mixer(proj, cos_t, sin_t, lg, ret_s0[j], b=nb, t=t, dk=dk, dv=dv)
            states["ret"].append(s_new)
            w_out = params["ret_w_out"][j].astype(BF16)
            if fused_layout:
                cur = _out_proj(a, w_out, cur, bm_io, g_post, m=m, tm=tm,
                                out_shape=(t, nb * d), out_map=tb_io).reshape(m, d)
            else:
                cur = _out_proj(a, w_out, cur, bm_io, g_post, m=m, tm=tm, out_shape=(m, d), out_map=bm_io)
                cur = cur.reshape(nb, t, d).swapaxes(0, 1).reshape(m, d)
            cur_is_tm = True
        elif kind == 2:
            assert cur_is_tm
            w_in = params["lru_w_in"][j].astype(BF16)
            d_rnn = w_in.shape[1] // 2
            uz = project(cur, bm_in, g_pre, w_in)
            buf = conv_buf[j].swapaxes(0, 1).reshape((CONV_W - 1) * nb, d_rnn)
            tt = max(1, min(t, LRU_ROWS // nb))
            a, cs_new, h_new = _lru_mixer(
                uz, buf, params["lru_conv_w"][j], vec(params["lru_conv_b"][j]),
                params["lru_w_a"][j].astype(BF16), vec(params["lru_b_a"][j]),
                params["lru_w_x"][j].astype(BF16), vec(params["lru_b_x"][j]),
                vec(params["lru_lam"][j]), lru_h0[j], nb=nb, t=t, tt=tt)
            states["conv"].append(cs_new.reshape(CONV_W - 1, nb, d_rnn).swapaxes(0, 1))
            states["lru"].append(h_new)
            cur = _out_proj(a, params["lru_w_out"][j].astype(BF16), cur, bm_io, g_post, m=m, tm=tm,
                            out_shape=(m, d), out_map=bm_io)
        else:
            if cur_is_tm and not fused_layout:
                cur = cur.reshape(t, nb, d).swapaxes(0, 1).reshape(m, d)
                cur_is_tm = False
            xv, x_in, x_io = to_bm_view(cur) if cur_is_tm else (cur, bm_in, bm_io)
            w_qz, w_g, w_kv = _split_nsa_w_in(params["nsa_w_in"][j])
            qz = project(xv, x_in, g_pre, w_qz)
            gates = project(xv, x_in, g_pre, w_g)
            n_kv_t = w_kv.shape[1] // KV_W
            kv, kv_rg = _norm_matmul_kv(xv, x_in(tm_kv), g_pre, w_kv, m=m, tm=tm_kv)
            pos_k, pos_v = params["nsa_cmp_pos_k"][j], params["nsa_cmp_pos_v"][j]
            lin_k, lin_v = params["nsa_cmp_lin_k"][j], params["nsa_cmp_lin_v"][j]
            kv5 = [a_.reshape(nb, t, N_KV, HEAD_DIM) for a_ in kv_rg]
            if nsa_cache is None:
                ck, cv = _compress_rows(kv, _pooling_matrix(pos_k), _pooling_matrix(pos_v),
                                        lin_k.astype(BF16), lin_v.astype(BF16), b=nb, t=t)
                a = _nsa_prompt_attend(qz, gates, ck, cv, kv, b=nb, t=t, tq=NSA_TQ, kc=NSA_KC)
                wb = min(WINDOW, t)
                new_state = (kv5[0], kv5[1], kv5[2], kv5[3], kv5[4][:, t - wb:], kv5[5][:, t - wb:])
            else:
                page_table, cmp_k, cmp_v, sel_k, sel_v, win_k, win_v = nsa_cache
                pool2d = lambda c: c[j].reshape(c.shape[1] * PAGE_ROWS, HEAD_DIM)
                win2d = lambda c: c[j].reshape(c.shape[1], c.shape[2] * N_KV, HEAD_DIM)
                new_pages = jnp.pad(jnp.stack(kv_rg[0:2]).reshape(2, nb, t * N_KV, HEAD_DIM),
                                    ((0, 0), (0, 0), (0, PAGE_ROWS - t * N_KV), (0, 0)))
                ck, cv = _compress_paged(page_table, pool2d(cmp_k), pool2d(cmp_v), new_pages,
                                         _pooling_matrix_rg(pos_k), _pooling_matrix_rg(pos_v),
                                         lin_k.astype(BF16), lin_v.astype(BF16))
                a, wk_new, wv_new = _nsa_sample_attend(page_table, pool2d(sel_k), pool2d(sel_v), kv, kv_rg[4], kv_rg[5], ck, cv,
                                                       win2d(win_k), win2d(win_v), qz, gates, n_new=t)
                wshape = (nb, win_k.shape[2], N_KV, HEAD_DIM)
                new_state = (kv5[0], kv5[1], kv5[2], kv5[3], wk_new.reshape(wshape), wv_new.reshape(wshape))
            states["nsa"].append(new_state)
            cur = _out_proj(a, params["nsa_w_out"][j].astype(BF16), xv, x_io, g_post, m=m, tm=tm,
                            out_shape=(m, d), out_map=bm_io)
            cur_is_tm = False
    if cur_is_tm:
        cur = cur.reshape(t, nb, d).swapaxes(0, 1).reshape(m, d)
    return cur.reshape(nb, t, d), states


def kernel(x_prompt, x_sample, state_pool, state_ret, state_conv, state_lru, cache_cmp_k, cache_cmp_v, cache_sel_k, cache_sel_v, cache_win_k, cache_win_v, page_table, norm_pre, norm_post, pool_w_in, pool_w_grp, pool_b_grp, pool_scale, pool_w_out, ret_w_in, ret_w_out, lru_w_in, lru_conv_w, lru_conv_b, lru_w_a, lru_b_a, lru_w_x, lru_b_x, lru_lam, lru_w_out, nsa_w_in, nsa_cmp_pos_k, nsa_cmp_lin_k, nsa_cmp_pos_v, nsa_cmp_lin_v, nsa_w_out):
    params = dict(
        norm_pre=norm_pre, norm_post=norm_post, pool_w_in=pool_w_in, pool_w_grp=pool_w_grp, pool_b_grp=pool_b_grp,
        pool_scale=pool_scale, pool_w_out=pool_w_out, ret_w_in=ret_w_in, ret_w_out=ret_w_out, lru_w_in=lru_w_in,
        lru_conv_w=lru_conv_w, lru_conv_b=lru_conv_b, lru_w_a=lru_w_a, lru_b_a=lru_b_a, lru_w_x=lru_w_x,
        lru_b_x=lru_b_x, lru_lam=lru_lam, lru_w_out=lru_w_out, nsa_w_in=nsa_w_in, nsa_cmp_pos_k=nsa_cmp_pos_k,
        nsa_cmp_lin_k=nsa_cmp_lin_k, nsa_cmp_pos_v=nsa_cmp_pos_v, nsa_cmp_lin_v=nsa_cmp_lin_v, nsa_w_out=nsa_w_out)
    b = x_prompt.shape[0]
    past = page_table.shape[1] * PAGE_SIZE
    zeros_like_state = lambda s: jnp.zeros((s.shape[0], b) + s.shape[2:], F32)
    yp, sp = _run_group(
        x_prompt, pos0=0, pool_buf=zeros_like_state(state_pool), ret_s0=zeros_like_state(state_ret),
        conv_buf=zeros_like_state(state_conv), lru_h0=zeros_like_state(state_lru), nsa_cache=None, params=params)
    ys, ss = _run_group(
        x_sample, pos0=past, pool_buf=state_pool, ret_s0=state_ret, conv_buf=state_conv, lru_h0=state_lru,
        nsa_cache=(page_table, cache_cmp_k, cache_cmp_v, cache_sel_k, cache_sel_v, cache_win_k, cache_win_v),
        params=params)
    out = [yp, ys]
    for key in ("pool", "ret", "conv", "lru"):
        out += [jnp.stack(sp[key]), jnp.stack(ss[key])]
    for idx in range(6):
        out += [jnp.stack([e[idx] for e in sp["nsa"]]), jnp.stack([e[idx] for e in ss["nsa"]])]
    return tuple(out)
```

```python
import functools
import math

import jax
import jax.numpy as jnp
from jax import lax
from jax.experimental import pallas as pl
from jax.experimental.pallas import tpu as pltpu

F32 = jnp.float32
BF16 = jnp.bfloat16

EPS = 1e-6
NEG = -1e30
FORCE = 1e6

PAGE_SIZE = 128
POOL_WINDOWS = (2, 4, 8, 16)
R_HEADS = 4
R_CHUNK = 128
ROPE_BASE = 10000.0
LRU_BW = 256
CONV_W = 4
LRU_C = 8.0
N_HEADS = 16
N_KV = 4
HPG = N_HEADS // N_KV
HEAD_DIM = 128
L_CMP = 32
S_CMP = 16
L_SEL = 64
N_SEL = 16
WINDOW = 512
KV_W = N_KV * HEAD_DIM
SUB_PER_PAGE = PAGE_SIZE // S_CMP
PAGES_PER_STEP = 16
LRU_ROWS = 1024
POOL_ROWS = 512

LANES = 128
SUBLANES = 8
VMEM_LIMIT = 56 * 1024 * 1024
IN_ROWS, IN_COLS = 2048, 1024
KV_ROWS = 1024
OUT_ROWS = 512
NSA_TQ, NSA_KC = 256, 512


def _cparams(sem):
    return pltpu.CompilerParams(dimension_semantics=sem, vmem_limit_bytes=VMEM_LIMIT)


def _act_dtype(rows):
    return BF16 if rows % (2 * SUBLANES) == 0 else F32


def _dot(a, b):
    return jnp.dot(a, b, preferred_element_type=F32)


def _dot_nt(a, b):
    return lax.dot_general(a, b, (((1,), (1,)), ((), ())), preferred_element_type=F32)


def _dot_tn(a, b):
    return lax.dot_general(a, b, (((0,), (0,)), ((), ())), preferred_element_type=F32)


def _silu(x):
    return x * jax.nn.sigmoid(x)


def _iota(shape, dim):
    return lax.broadcasted_iota(jnp.int32, shape, dim)


def _norm_mm_body(x_ref, g_ref, w_ref, o_ref, h_ref):
    @pl.when(pl.program_id(1) == 0)
    def _():
        x = x_ref[...]
        ms = jnp.mean(x * x, axis=-1, keepdims=True)
        h_ref[...] = (x * lax.rsqrt(ms + EPS) * g_ref[...]).astype(BF16)

    o_ref[...] = _dot(h_ref[...], w_ref[...]).reshape(o_ref.shape)


def _norm_matmul(x, x_map, g, w, *, m, tm, tn, out_shape, out_block, out_map):
    d, n = w.shape
    return pl.pallas_call(
        _norm_mm_body,
        grid=(m // tm, n // tn),
        in_specs=[
            pl.BlockSpec((tm, d), x_map),
            pl.BlockSpec((1, d), lambda i, j: (0, 0)),
            pl.BlockSpec((d, tn), lambda i, j: (0, j)),
        ],
        out_specs=pl.BlockSpec(out_block, out_map),
        out_shape=jax.ShapeDtypeStruct(out_shape, F32),
        scratch_shapes=[pltpu.VMEM((tm, d), BF16)],
        compiler_params=_cparams(("parallel", "arbitrary")),
        name="norm_matmul",
    )(x, g, w)


def _norm_mm_kv_body(x_ref, g_ref, w_ref, o_ref, *rest):
    oi_refs, h_ref = rest[:-1], rest[-1]
    j = pl.program_id(1)

    @pl.when(j == 0)
    def _():
        x = x_ref[...]
        ms = jnp.mean(x * x, axis=-1, keepdims=True)
        h_ref[...] = (x * lax.rsqrt(ms + EPS) * g_ref[...]).astype(BF16)

    r = _dot(h_ref[...], w_ref[...])
    o_ref[0] = r
    for jj, oi_ref in enumerate(oi_refs):
        @pl.when(j == jj)
        def _():
            for g in range(N_KV):
                oi_ref[pl.ds(g, r.shape[0], stride=N_KV), :] = r[:, g * HEAD_DIM:(g + 1) * HEAD_DIM]


def _norm_matmul_kv(x, x_map, g, w, *, m, tm):
    d, n = w.shape
    n_t = n // KV_W
    outs = pl.pallas_call(
        _norm_mm_kv_body,
        grid=(m // tm, n_t),
        in_specs=[
            pl.BlockSpec((tm, d), x_map),
            pl.BlockSpec((1, d), lambda i, j: (0, 0)),
            pl.BlockSpec((d, KV_W), lambda i, j: (0, j)),
        ],
        out_specs=[pl.BlockSpec((1, tm, KV_W), lambda i, j: (j, i, 0))]
        + [pl.BlockSpec((tm * N_KV, HEAD_DIM), lambda i, j: (i, 0)) for _ in range(n_t)],
        out_shape=[jax.ShapeDtypeStruct((n_t, m, KV_W), F32)]
        + [jax.ShapeDtypeStruct((m * N_KV, HEAD_DIM), F32) for _ in range(n_t)],
        scratch_shapes=[pltpu.VMEM((tm, d), BF16)],
        compiler_params=_cparams(("parallel", "arbitrary")),
        name="norm_matmul_kv",
    )(x, g, w)
    return outs[0], outs[1:]


def _out_proj_body(a_ref, w_ref, x_ref, g_ref, o_ref):
    y = _dot(a_ref[...].astype(BF16), w_ref[...])
    ms = jnp.mean(y * y, axis=-1, keepdims=True)
    o_ref[...] = x_ref[...] + y * lax.rsqrt(ms + EPS) * g_ref[...]


def _out_proj(a, w, x, x_map, g, *, m, tm, out_shape, out_map):
    e, d = w.shape
    return pl.pallas_call(
        _out_proj_body,
        grid=(m // tm,),
        in_specs=[
            pl.BlockSpec((tm, e), lambda i: (i, 0)),
            pl.BlockSpec((e, d), lambda i: (0, 0)),
            pl.BlockSpec((tm, d), x_map),
            pl.BlockSpec((1, d), lambda i: (0, 0)),
        ],
        out_specs=pl.BlockSpec((tm, d), out_map),
        out_shape=jax.ShapeDtypeStruct(out_shape, F32),
        compiler_params=_cparams(("parallel",)),
        name="out_proj",
    )(a, w, x, g)


POOL_HALO = 16
POOL_PAD = 8


def _pool_body(x_ref, g_ref, w_ref, buf_ref, wg_ref, bg_ref, sc_ref, a_ref, st_ref, ext, sa, sb,
               *, tt, pos0, gw):
    ti = pl.program_id(1)
    e = ext.shape[1]
    x = x_ref[...]
    ms = jnp.mean(x * x, axis=-1, keepdims=True)
    h = (x * lax.rsqrt(ms + EPS) * g_ref[...]).astype(BF16)
    lo = POOL_PAD
    top = POOL_PAD + POOL_HALO
    rows = top + tt

    @pl.when(ti == 0)
    def _():
        ext[0:lo, :] = jnp.zeros((lo, ext.shape[1]), F32)
        ext[lo:top, :] = buf_ref[0]

    @pl.when(ti > 0)
    def _():
        ext[lo:top, :] = ext[lo + tt:top + tt, :]

    u = _dot(h, w_ref[:, 0:e])
    ext[top:rows, :] = u
    sa[0:lo, :] = jnp.zeros((lo, gw), F32)
    sb[0:lo, :] = jnp.zeros((lo, gw), F32)
    pos = _iota((tt, 1), 0) + (pos0 + ti * tt)
    for g, w in enumerate(POOL_WINDOWS):
        cs = slice(g * gw, (g + 1) * gw)
        sa[lo:rows, :] = ext[lo:rows, cs] + ext[lo - 1:rows - 1, cs]
        cur, oth, sh = sa, sb, 2
        while sh < w:
            oth[lo:rows, :] = cur[lo:rows, :] + cur[lo - sh:rows - sh, :]
            cur, oth, sh = oth, cur, sh * 2
        cnt = jnp.minimum(pos + 1, w).astype(F32)
        mean = cur[top:rows, :] / cnt
        mixed = _dot((mean - u[:, cs]).astype(BF16), wg_ref[g]) + bg_ref[:, cs]
        z = _dot(h, w_ref[:, e + g * gw:e + (g + 1) * gw])
        a_ref[:, cs] = (mixed * sc_ref[:, cs] * _silu(z)).astype(a_ref.dtype)
    st_ref[0] = ext[lo + tt:top + tt, :]


def _pool_mixer(x, g_pre, w_in, buf16, wg, bg, sc, *, b, t, tt, pos0):
    d = x.shape[1]
    e = buf16.shape[-1]
    gw = e // len(POOL_WINDOWS)
    nt = t // tt
    rows = POOL_PAD + POOL_HALO + tt
    return pl.pallas_call(
        functools.partial(_pool_body, tt=tt, pos0=pos0, gw=gw),
        grid=(b, nt),
        in_specs=[
            pl.BlockSpec((tt, d), lambda i, j: (i * nt + j, 0)),
            pl.BlockSpec((1, d), lambda i, j: (0, 0)),
            pl.BlockSpec((d, 2 * e), lambda i, j: (0, 0)),
            pl.BlockSpec((1, POOL_HALO, e), lambda i, j: (i, 0, 0)),
            pl.BlockSpec((len(POOL_WINDOWS), gw, gw), lambda i, j: (0, 0, 0)),
            pl.BlockSpec((1, e), lambda i, j: (0, 0)),
            pl.BlockSpec((1, e), lambda i, j: (0, 0)),
        ],
        out_specs=[
            pl.BlockSpec((tt, e), lambda i, j: (i * nt + j, 0)),
            pl.BlockSpec((1, POOL_HALO, e), lambda i, j: (i, 0, 0)),
        ],
        out_shape=[
            jax.ShapeDtypeStruct((b * t, e), _act_dtype(tt)),
            jax.ShapeDtypeStruct((b, POOL_HALO, e), F32),
        ],
        scratch_shapes=[
            pltpu.VMEM((rows, e), F32),
            pltpu.VMEM((rows, gw), F32),
            pltpu.VMEM((rows, gw), F32),
        ],
        compiler_params=_cparams(("parallel", "arbitrary")),
        name="pool_mixer",
    )(x, g_pre, w_in, buf16, wg, bg, sc)


def _ret_body(lg_ref, q_ref, k_ref, v_ref, g_ref, cos_ref, sin_ref, s0_ref, a_ref, so_ref, s_scr,
              *, c_len, c_pad, dk, dv):
    c = pl.program_id(1)

    @pl.when(c == 0)
    def _():
        s_scr[...] = s0_ref[0]

    cosv = cos_ref[...]
    sinv = sin_ref[...]
    even = (_iota((c_len, dk), 1) & 1) == 0

    def rot(x):
        nxt = pltpu.roll(x, dk - 1, axis=1)
        prv = pltpu.roll(x, 1, axis=1)
        return x * cosv + jnp.where(even, nxt, prv) * sinv

    def pad(x):
        if c_pad == c_len:
            return x
        return jnp.concatenate([x, jnp.zeros((c_pad - c_len, x.shape[1]), x.dtype)], axis=0)

    i_col = _iota((c_pad, 1), 0).astype(F32)
    j_row = _iota((1, c_pad), 1).astype(F32)
    diff = i_col - j_row
    for h in range(R_HEADS):
        ks = slice(h * dk, (h + 1) * dk)
        vs = slice(h * dv, (h + 1) * dv)
        lg = lg_ref[h]
        q = pad(rot(q_ref[:, ks]))
        k = pad(rot(k_ref[:, ks]) * (dk ** -0.5))
        v = pad(v_ref[:, vs])
        intra = jnp.where(diff >= 0, jnp.exp(jnp.maximum(diff, 0.0) * lg), 0.0)
        q_dec = jnp.exp((i_col + 1.0) * lg)
        k_dec = jnp.exp((c_len - 1.0 - i_col) * lg)
        c_dec = jnp.exp(jnp.full((1, 1), float(c_len), F32) * lg)

        qb = q.astype(BF16)
        vb = v.astype(BF16)
        s = _dot_nt(qb, k.astype(BF16)) * intra
        s_old = s_scr[h]
        o = _dot(s.astype(BF16), vb) + _dot(qb, s_old.astype(BF16)) * q_dec
        s_scr[h] = s_old * c_dec + _dot_tn((k * k_dec).astype(BF16), vb)
        o = o[0:c_len]
        o = o * lax.rsqrt(jnp.mean(o * o, axis=-1, keepdims=True) + EPS)
        a_ref[:, vs] = (o * _silu(g_ref[:, vs])).astype(a_ref.dtype)

    @pl.when(c == pl.num_programs(1) - 1)
    def _():
        so_ref[0] = s_scr[...]


def _retention_mixer(proj, cos_t, sin_t, lg, s0, *, b, t, dk, dv):
    c_len = R_CHUNK if t % R_CHUNK == 0 else t
    c_pad = max(c_len, R_CHUNK)
    nc = t // c_len
    nh = R_HEADS
    qk_w, vg_w = nh * dk, nh * dv
    row = lambda i, c: i * nc + c
    return pl.pallas_call(
        functools.partial(_ret_body, c_len=c_len, c_pad=c_pad, dk=dk, dv=dv),
        grid=(b, nc),
        in_specs=[
            pl.BlockSpec(memory_space=pltpu.SMEM),
            pl.BlockSpec((c_len, qk_w), lambda i, c: (row(i, c), 0)),
            pl.BlockSpec((c_len, qk_w), lambda i, c: (row(i, c), 1)),
            pl.BlockSpec((c_len, vg_w), lambda i, c: (row(i, c), 2 * qk_w // vg_w)),
            pl.BlockSpec((c_len, vg_w), lambda i, c: (row(i, c), 2 * qk_w // vg_w + 1)),
            pl.BlockSpec((c_len, dk), lambda i, c: (c, 0)),
            pl.BlockSpec((c_len, dk), lambda i, c: (c, 0)),
            pl.BlockSpec((1, nh, dk, dv), lambda i, c: (i, 0, 0, 0)),
        ],
        out_specs=[
            pl.BlockSpec((c_len, vg_w), lambda i, c: (row(i, c), 0)),
            pl.BlockSpec((1, nh, dk, dv), lambda i, c: (i, 0, 0, 0)),
        ],
        out_shape=[
            jax.ShapeDtypeStruct((b * t, vg_w), _act_dtype(c_len)),
            jax.ShapeDtypeStruct((b, nh, dk, dv), F32),
        ],
        scratch_shapes=[pltpu.VMEM((nh, dk, dv), F32)],
        compiler_params=_cparams(("parallel", "arbitrary")),
        name="retention_mixer",
    )(lg, proj, proj, proj, proj, cos_t, sin_t, s0)


def _lru_body(u_ref, z_ref, buf_ref, cw_ref, cb_ref, wa_ref, ba_ref, wx_ref, bx_ref, lam_ref, h0_ref,
              a_ref, cso_ref, ho_ref, ext, a_s, b_s, h_s, hc, *, tt, nb):
    ti = pl.program_id(1)
    halo = (CONV_W - 1) * nb
    rows = tt * nb

    @pl.when(ti == 0)
    def _():
        ext[0:halo, :] = buf_ref[...]
        hc[...] = h0_ref[...]

    @pl.when(ti > 0)
    def _():
        ext[0:halo, :] = ext[rows:rows + halo, :]

    ext[halo:halo + rows, :] = u_ref[...]
    cw = cw_ref[...]
    c = cb_ref[...] + ext[0:rows, :] * cw[0:1, :]
    for kk in range(1, CONV_W):
        c = c + ext[kk * nb:kk * nb + rows, :] * cw[kk:kk + 1, :]
    cb16 = c.astype(BF16)
    r = jax.nn.sigmoid(_dot(cb16, wa_ref[0]) + ba_ref[...])
    ig = jax.nn.sigmoid(_dot(cb16, wx_ref[0]) + bx_ref[...])
    nl = -lam_ref[...]
    softplus = jnp.maximum(nl, 0.0) + jnp.log1p(jnp.exp(-jnp.abs(nl)))
    log_a = (-LRU_C) * r * softplus
    a = jnp.exp(log_a)
    a_s[...] = a
    b_s[...] = jnp.sqrt(1.0 - a * a) * (ig * c)

    def step(t, h):
        rs = pl.ds(pl.multiple_of(t * nb, SUBLANES), nb)
        h = a_s[rs, :] * h + b_s[rs, :]
        h_s[rs, :] = h
        return h

    hc[...] = lax.fori_loop(0, tt, step, hc[...], unroll=8)
    a_ref[...] = (h_s[...] * _silu(z_ref[...])).astype(BF16)
    cso_ref[...] = ext[rows:rows + halo, :]
    ho_ref[...] = hc[...]


def _lru_mixer(uz, buf, cw, cb, wa, ba, wx, bx, lam, h0, *, nb, t, tt):
    d_rnn = h0.shape[-1]
    nblk = d_rnn // LRU_BW
    nt = t // tt
    rows = tt * nb
    halo = (CONV_W - 1) * nb
    vec = lambda: pl.BlockSpec((1, LRU_BW), lambda j, i: (0, j))
    return pl.pallas_call(
        functools.partial(_lru_body, tt=tt, nb=nb),
        grid=(nblk, nt),
        in_specs=[
            pl.BlockSpec((rows, LRU_BW), lambda j, i: (i, j)),
            pl.BlockSpec((rows, LRU_BW), lambda j, i: (i, nblk + j)),
            pl.BlockSpec((halo, LRU_BW), lambda j, i: (0, j)),
            pl.BlockSpec((CONV_W, LRU_BW), lambda j, i: (0, j)),
            vec(),
            pl.BlockSpec((1, LRU_BW, LRU_BW), lambda j, i: (j, 0, 0)),
            vec(),
            pl.BlockSpec((1, LRU_BW, LRU_BW), lambda j, i: (j, 0, 0)),
            vec(),
            vec(),
            pl.BlockSpec((nb, LRU_BW), lambda j, i: (0, j)),
        ],
        out_specs=[
            pl.BlockSpec((rows, LRU_BW), lambda j, i: (i, j)),
            pl.BlockSpec((halo, LRU_BW), lambda j, i: (0, j)),
            pl.BlockSpec((nb, LRU_BW), lambda j, i: (0, j)),
        ],
        out_shape=[
            jax.ShapeDtypeStruct((t * nb, d_rnn), BF16),
            jax.ShapeDtypeStruct((halo, d_rnn), F32),
            jax.ShapeDtypeStruct((nb, d_rnn), F32),
        ],
        scratch_shapes=[
            pltpu.VMEM((halo + rows, LRU_BW), F32),
            pltpu.VMEM((rows, LRU_BW), F32),
            pltpu.VMEM((rows, LRU_BW), F32),
            pltpu.VMEM((rows, LRU_BW), F32),
            pltpu.VMEM((nb, LRU_BW), F32),
        ],
        compiler_params=_cparams(("parallel", "arbitrary")),
        name="lru_mixer",
    )(uz, uz, buf, cw, cb, wa, ba, wx, bx, lam, h0)


def _softmax_masked(s, mask):
    sm = jnp.where(mask, s, NEG)
    m = jnp.max(sm, axis=-1, keepdims=True)
    e = jnp.where(mask, jnp.exp(sm - m), 0.0)
    l = jnp.sum(e, axis=-1, keepdims=True)
    return e / jnp.where(l > 0.0, l, 1.0)


def _dot_f32_by_01(x, m01):
    hi = x.astype(BF16)
    r1 = x - hi.astype(F32)
    mid = r1.astype(BF16)
    lo = (r1 - mid.astype(F32)).astype(BF16)
    return _dot(hi, m01) + _dot(mid, m01) + _dot(lo, m01)


def _block_importance_matrix(n_rows, n_blk_lanes, n_blk, rows_per_cmp=1):
    r, s = L_SEL // S_CMP, L_CMP // S_CMP
    i = lax.shift_right_logical(_iota((n_rows, n_blk_lanes), 0), int(math.log2(rows_per_cmp)))
    j = _iota((n_rows, n_blk_lanes), 1)
    hit = (i >= r * j - (s - 1)) & (i <= r * j + (r - 1)) & (j < n_blk)
    return jnp.where(hit, 1.0, 0.0).astype(BF16)


def _select_blocks(imp, pos, n_blk):
    j = _iota(imp.shape, 1)
    cur = lax.shift_right_logical(pos, int(math.log2(L_SEL)))
    valid = j * L_SEL <= pos
    forced = (j == 0) | (j == cur) | (j == cur - 1)
    score = jnp.where(forced, FORCE, jnp.where(valid, imp, -1.0))
    cnt = jnp.zeros(imp.shape, F32)
    for i in range(n_blk):
        si = score[:, i:i + 1]
        ahead = (si > score) | ((si == score) & (j > i))
        cnt = cnt + jnp.where(ahead, 1.0, 0.0)
    keep = (cnt < float(min(N_SEL, n_blk))) & (score >= 0.0) & (j < n_blk)
    return jnp.where(keep, 1.0, 0.0)


def _select_blocks_t(imp, pos_row, n_blk):
    nb8 = -(-n_blk // SUBLANES) * SUBLANES
    imp_t = imp.T[0:nb8, :]
    j = _iota(imp_t.shape, 0)
    cur = lax.shift_right_logical(pos_row, int(math.log2(L_SEL)))
    valid = j * L_SEL <= pos_row
    forced = (j == 0) | (j == cur) | (j == cur - 1)
    score = jnp.where(forced, FORCE, jnp.where(valid, imp_t, -1.0))
    cnt = jnp.zeros(imp_t.shape, F32)
    for i in range(n_blk):
        si = score[i:i + 1, :]
        ahead = (si > score) | ((si == score) & (j > i))
        cnt = cnt + jnp.where(ahead, 1.0, 0.0)
    keep = (cnt < float(min(N_SEL, n_blk))) & (score >= 0.0) & (j < n_blk)
    sel_t = jnp.where(keep, 1.0, 0.0)
    if nb8 < LANES:
        sel_t = jnp.concatenate([sel_t, jnp.zeros((LANES - nb8, imp.shape[0]), F32)], axis=0)
    return sel_t.T


def _online_update(m_ref, l_ref, acc_ref, s, mask, pv):
    sm = jnp.where(mask, s, NEG)
    m_old = m_ref[...]
    m_new = jnp.maximum(m_old, jnp.max(sm, axis=-1, keepdims=True))
    p = jnp.where(mask, jnp.exp(sm - m_new), 0.0)
    alpha = jnp.exp(m_old - m_new)
    l_ref[...] = alpha * l_ref[...] + jnp.sum(p, axis=-1, keepdims=True)
    acc_ref[...] = alpha * acc_ref[...] + pv(p)
    m_ref[...] = m_new


def _online_finish(l_ref, acc_ref):
    l = l_ref[...]
    return acc_ref[...] / jnp.where(l > 0.0, l, 1.0)


def _cmp_accumulate(page, w2_ref, acc_a, acc_b, row0):
    pb = page.astype(BF16)
    for g in range(N_KV):
        cs = slice(g * HEAD_DIM, (g + 1) * HEAD_DIM)
        ab = _dot(w2_ref[g], pb[:, cs])
        acc_a[pl.ds(row0, SUB_PER_PAGE), cs] = ab[0:SUB_PER_PAGE]
        acc_b[pl.ds(row0, SUB_PER_PAGE), cs] = ab[SUB_PER_PAGE:2 * SUB_PER_PAGE]


def _cmp_finish(acc_a, acc_b, lin_ref, out_ref, ncp):
    pooled = acc_a[0:ncp, :] + acc_b[1:ncp + 1, :]
    for g in range(N_KV):
        cs = slice(g * HEAD_DIM, (g + 1) * HEAD_DIM)
        out_ref[0, :, cs] = _dot(pooled[:, cs].astype(BF16), lin_ref[g])


def _compress_rows_body(k_ref, v_ref, w2k_ref, w2v_ref, lk_ref, lv_ref, ok_ref, ov_ref,
                        aak, abk, aav, abv, *, pages, ncp):
    s = pl.program_id(1)

    @pl.when(s == 0)
    def _():
        tail = jnp.zeros((SUBLANES, KV_W), F32)
        abk[ncp:ncp + SUBLANES, :] = tail
        abv[ncp:ncp + SUBLANES, :] = tail

    for i in range(pages):
        row0 = pl.multiple_of((s * pages + i) * SUB_PER_PAGE, SUBLANES)
        rs = slice(i * PAGE_SIZE, (i + 1) * PAGE_SIZE)
        _cmp_accumulate(k_ref[0, rs, :], w2k_ref, aak, abk, row0)
        _cmp_accumulate(v_ref[0, rs, :], w2v_ref, aav, abv, row0)

    @pl.when(s == pl.num_programs(1) - 1)
    def _():
        _cmp_finish(aak, abk, lk_ref, ok_ref, ncp)
        _cmp_finish(aav, abv, lv_ref, ov_ref, ncp)


def _compress_rows(kv, w2k, w2v, lk, lv, *, b, t):
    pages = min(PAGES_PER_STEP, t // PAGE_SIZE)
    rows = pages * PAGE_SIZE
    ns = t // rows
    ncp = t // S_CMP
    wspec = lambda shp: pl.BlockSpec(shp, lambda i, s: (0, 0, 0))
    acc = lambda: pltpu.VMEM((ncp + SUBLANES, KV_W), F32)
    return pl.pallas_call(
        functools.partial(_compress_rows_body, pages=pages, ncp=ncp),
        grid=(b, ns),
        in_specs=[
            pl.BlockSpec((1, rows, KV_W), lambda i, s: (0, i * ns + s, 0)),
            pl.BlockSpec((1, rows, KV_W), lambda i, s: (1, i * ns + s, 0)),
            wspec(w2k.shape), wspec(w2v.shape), wspec(lk.shape), wspec(lv.shape),
        ],
        out_specs=[
            pl.BlockSpec((1, ncp, KV_W), lambda i, s: (i, 0, 0)),
            pl.BlockSpec((1, ncp, KV_W), lambda i, s: (i, 0, 0)),
        ],
        out_shape=[jax.ShapeDtypeStruct((b, ncp, KV_W), F32)] * 2,
        scratch_shapes=[acc(), acc(), acc(), acc()],
        compiler_params=_cparams(("parallel", "arbitrary")),
        name="compress_rows",
    )(kv, kv, w2k, w2v, lk, lv)


PAGE_ROWS = PAGE_SIZE * N_KV
SUB_ROWS = SUB_PER_PAGE * N_KV


def _cmp_accumulate_rg(page, w2_ref, acc_a, acc_b, row0):
    ab = _dot(w2_ref[...], page.astype(BF16))
    acc_a[pl.ds(row0, SUB_ROWS), :] = ab[0:SUB_ROWS]
    acc_b[pl.ds(row0, SUB_ROWS), :] = ab[SUB_ROWS:2 * SUB_ROWS]


def _cmp_finish_rg(acc_a, acc_b, lin_ref, out_ref, ncp):
    for g in range(N_KV):
        pooled = acc_a[pl.ds(g, ncp, stride=N_KV), :] + acc_b[pl.ds(N_KV + g, ncp, stride=N_KV), :]
        out_ref[0, :, g * HEAD_DIM:(g + 1) * HEAD_DIM] = _dot(pooled.astype(BF16), lin_ref[g])


def _compress_paged_body(pt_ref, *refs, pages, n_pool_steps, ncp):
    pk = refs[0:pages]
    pv = refs[pages:2 * pages]
    nk_ref, nv_ref, w2k_ref, w2v_ref, lk_ref, lv_ref, ok_ref, ov_ref, aak, abk, aav, abv = refs[2 * pages:]
    s = pl.program_id(1)

    for i in range(pages):
        row0 = pl.multiple_of((s * pages + i) * SUB_ROWS, SUB_ROWS)
        _cmp_accumulate_rg(pk[i][...], w2k_ref, aak, abk, row0)
        _cmp_accumulate_rg(pv[i][...], w2v_ref, aav, abv, row0)

    @pl.when(s == n_pool_steps - 1)
    def _():
        row0 = n_pool_steps * pages * SUB_ROWS
        _cmp_accumulate_rg(nk_ref[0, 0], w2k_ref, aak, abk, row0)
        _cmp_accumulate_rg(nv_ref[0, 0], w2v_ref, aav, abv, row0)
        _cmp_finish_rg(aak, abk, lk_ref, ok_ref, ncp)
        _cmp_finish_rg(aav, abv, lv_ref, ov_ref, ncp)


def _page_spec(i, pages):
    return pl.BlockSpec((PAGE_ROWS, HEAD_DIM), lambda bi, s, pt: (pt[bi, s * pages + i], 0))


def _compress_paged(page_table, pool_k, pool_v, new_pages, w2k, w2v, lk, lv):
    b, n_pages = page_table.shape
    pages = PAGES_PER_STEP
    n_pool_steps = n_pages // pages
    ncp = n_pages * SUB_PER_PAGE
    wspec = lambda shp: pl.BlockSpec(shp, lambda bi, s, pt: (0,) * len(shp))
    newspec = lambda idx: pl.BlockSpec((1, 1, PAGE_ROWS, HEAD_DIM), lambda bi, s, pt: (idx, bi, 0, 0))
    acc = lambda: pltpu.VMEM(((ncp + SUB_PER_PAGE) * N_KV, HEAD_DIM), F32)
    grid_spec = pltpu.PrefetchScalarGridSpec(
        num_scalar_prefetch=1,
        grid=(b, n_pool_steps),
        in_specs=(
            [_page_spec(i, pages) for i in range(pages)] * 2
            + [newspec(0), newspec(1), wspec(w2k.shape), wspec(w2v.shape), wspec(lk.shape), wspec(lv.shape)]
        ),
        out_specs=[
            pl.BlockSpec((1, ncp, KV_W), lambda bi, s, pt: (bi, 0, 0)),
            pl.BlockSpec((1, ncp, KV_W), lambda bi, s, pt: (bi, 0, 0)),
        ],
        scratch_shapes=[acc(), acc(), acc(), acc()],
    )
    return pl.pallas_call(
        functools.partial(_compress_paged_body, pages=pages, n_pool_steps=n_pool_steps, ncp=ncp),
        grid_spec=grid_spec,
        out_shape=[jax.ShapeDtypeStruct((b, ncp, KV_W), F32)] * 2,
        compiler_params=_cparams(("parallel", "arbitrary")),
        name="compress_paged",
    )(page_table, *([pool_k] * pages), *([pool_v] * pages), new_pages, new_pages, w2k, w2v, lk, lv)


LOG2E = 1.4426950408889634


def _lane_fold(x, op):
    out = x[:, 0:LANES]
    for u in range(1, x.shape[1] // LANES):
        out = op(out, x[:, u * LANES:(u + 1) * LANES])
    return out


def _nsa_prompt_body(q_ref, z_ref, gt_ref, ck_ref, cv_ref, ks_ref, vs_ref, kw_ref, vw_ref, a_ref,
                     ksb, vsb, kwb, vwb, s_buf, mx_s, acc_s, *, tq, t, kc, n_cmp):
    ones_col = jnp.where(_iota((t, LANES), 1) == 0, 1.0, 0.0).astype(BF16)
    ksb[...] = ks_ref[0].astype(BF16)
    vsb[...] = jnp.concatenate([vs_ref[0].astype(BF16), ones_col], axis=1)
    kwb[...] = kw_ref[0].astype(BF16)
    vwb[...] = jnp.concatenate([vw_ref[0].astype(BF16), ones_col], axis=1)

    n_blk = -(-t // L_SEL)
    ncl = ck_ref.shape[1]
    scale = HEAD_DIM ** -0.5
    c_exp = scale * LOG2E
    hs = [slice(h * HEAD_DIM, (h + 1) * HEAD_DIM) for h in range(HPG)]
    hr = [slice(h * tq, (h + 1) * tq) for h in range(HPG)]
    ckb = ck_ref[0].astype(BF16)
    cvb = cv_ref[0].astype(BF16)
    msel = _block_importance_matrix(ncl, LANES, n_blk)
    ws = min(WINDOW + tq, t)
    blk_shift = int(math.log2(L_SEL))

    def q_tile(qi, carry):
        q0 = pl.multiple_of(qi * tq, tq)
        rows = pl.ds(q0, tq)
        pos = q0 + _iota((tq, 1), 0)
        qb = jnp.concatenate([q_ref[rows, hs[h]] for h in range(HPG)], axis=0).astype(BF16)

        ci = _iota((1, ncl), 1)
        cmask = (ci * S_CMP + (L_CMP - 1) <= pos) & (ci < n_cmp)
        s_c = _dot_nt(qb, ckb) * scale
        p_c = [_softmax_masked(s_c[hr[h]], cmask) for h in range(HPG)]
        imp_c = p_c[0] + p_c[1] + p_c[2] + p_c[3]
        o_c = _dot(jnp.concatenate(p_c, axis=0).astype(BF16), cvb)

        imp = _dot_f32_by_01(imp_c, msel)
        selb = _select_blocks_t(imp, q0 + _iota((1, tq), 1), n_blk).astype(BF16)

        w0 = pl.multiple_of(jnp.minimum(jnp.maximum(q0 - WINDOW, 0), t - ws), tq)
        wpos = w0 + _iota((1, ws), 1)
        wbias = jnp.where((wpos <= pos) & (wpos > pos - WINDOW), 0.0, NEG)
        s_w = _dot_nt(qb, kwb[pl.ds(w0, ws), :])
        e_w = []
        for h in range(HPG):
            sh = s_w[hr[h]] + wbias
            e_w.append(jnp.exp2((sh - jnp.max(sh, axis=-1, keepdims=True)) * c_exp))
        pv_w = _dot(jnp.concatenate(e_w, axis=0).astype(BF16), vwb[pl.ds(w0, ws), :])
        o_w = pv_w[:, 0:HEAD_DIM] / pv_w[:, HEAD_DIM:HEAD_DIM + 1]

        mx_s[...] = jnp.full(mx_s.shape, NEG, F32)

        def pass1(ki, c):
            k0 = pl.multiple_of(ki * kc, kc)
            kpos = k0 + _iota((1, kc), 1)
            blk = lax.shift_right_logical(k0 + _iota((LANES, kc), 1), blk_shift)
            expand = jnp.where(blk == _iota((LANES, kc), 0), 1.0, 0.0).astype(BF16)
            bias = (_dot(selb, expand) - 1.0) * (-NEG)
            bias = jnp.where(kpos <= pos, bias, NEG)
            s = _dot_nt(qb, ksb[pl.ds(k0, kc), :])
            for h in range(HPG):
                sh = s[hr[h]] + bias
                s_buf[ki, hr[h], :] = sh
                mx_s[hr[h], :] = jnp.maximum(mx_s[hr[h], :], _lane_fold(sh, jnp.maximum))
            return c

        n_chunks = (q0 + tq + kc - 1) // kc
        lax.fori_loop(0, n_chunks, pass1, 0)
        m_b = jnp.broadcast_to(jnp.max(mx_s[...], axis=-1, keepdims=True), mx_s.shape)
        mx_s[...] = m_b
        acc_s[...] = jnp.zeros(acc_s.shape, F32)

        def pass2(ki, c):
            k0 = pl.multiple_of(ki * kc, kc)
            mb = mx_s[...]
            e = [jnp.exp2((s_buf[ki, :, u * LANES:(u + 1) * LANES] - mb) * c_exp) for u in range(kc // LANES)]
            acc_s[...] = acc_s[...] + _dot(jnp.concatenate(e, axis=1).astype(BF16), vsb[pl.ds(k0, kc), :])
            return c

        lax.fori_loop(0, n_chunks, pass2, 0)
        o_s = acc_s[:, 0:HEAD_DIM] / acc_s[:, HEAD_DIM:HEAD_DIM + 1]

        gates = jax.nn.sigmoid(gt_ref[rows, :])
        for h in range(HPG):
            o = (gates[:, h:h + 1] * o_c[hr[h]] + gates[:, HPG + h:HPG + h + 1] * o_s[hr[h]]
                 + gates[:, 2 * HPG + h:2 * HPG + h + 1] * o_w[hr[h]])
            a_ref[rows, hs[h]] = (o * _silu(z_ref[rows, hs[h]])).astype(BF16)
        return carry

    lax.fori_loop(0, t // tq, q_tile, 0)


def _nsa_prompt_attend(qz, gates, ck, cv, kv, *, b, t, tq, kc):
    gq = HPG * HEAD_DIM
    n_cmp = t // S_CMP - (L_CMP // S_CMP) + 1
    ncl = ck.shape[1]
    kvspec = lambda idx: pl.BlockSpec((1, t, HEAD_DIM), lambda i, g: (idx, i, g))
    kb = lambda: pltpu.VMEM((t, HEAD_DIM), BF16)
    vb = lambda: pltpu.VMEM((t, HEAD_DIM + LANES), BF16)
    return pl.pallas_call(
        functools.partial(_nsa_prompt_body, tq=tq, t=t, kc=kc, n_cmp=n_cmp),
        grid=(b, N_KV),
        in_specs=[
            pl.BlockSpec((t, gq), lambda i, g: (i, g)),
            pl.BlockSpec((t, gq), lambda i, g: (i, N_KV + g)),
            pl.BlockSpec((t, LANES), lambda i, g: (i, g)),
            pl.BlockSpec((1, ncl, HEAD_DIM), lambda i, g: (i, 0, g)),
            pl.BlockSpec((1, ncl, HEAD_DIM), lambda i, g: (i, 0, g)),
            kvspec(2), kvspec(3), kvspec(4), kvspec(5),
        ],
        out_specs=pl.BlockSpec((t, gq), lambda i, g: (i, g)),
        out_shape=jax.ShapeDtypeStruct((b * t, N_KV * gq), BF16),
        scratch_shapes=[
            kb(), vb(), kb(), vb(),
            pltpu.VMEM((t // kc, HPG * tq, kc), F32),
            pltpu.VMEM((HPG * tq, LANES), F32),
            pltpu.VMEM((HPG * tq, HEAD_DIM + LANES), F32),
        ],
        compiler_params=_cparams(("parallel", "parallel")),
        name="nsa_prompt_attend",
    )(qz, qz, gates, ck, cv, kv, kv, kv, kv)


def _rows_by_group(ref, n):
    return jnp.concatenate([ref[pl.ds(g, n, stride=N_KV), :] for g in range(N_KV)], axis=1)


def _diag_blocks(x, rows_per_group):
    return jnp.concatenate(
        [x[g * rows_per_group:(g + 1) * rows_per_group, g * HEAD_DIM:(g + 1) * HEAD_DIM]
         for g in range(N_KV)], axis=0)


def _pad_rows(x, n):
    return jnp.concatenate([x, jnp.zeros((n - x.shape[0], x.shape[1]), x.dtype)], axis=0)


def _nsa_sample_body(pt_ref, *refs, pages, n_pool_steps, past, n_new, wb):
    sk = refs[0:pages]
    sv = refs[pages:2 * pages]
    (nks_ref, nvs_ref, nkw_ref, nvw_ref, nkw_rg, nvw_rg, ck_ref, cv_ref, wk_ref, wv_ref, q_ref, z_ref, gt_ref,
     a_ref, wko_ref, wvo_ref, q2, selr, m_s, l_s, acc, oc, ow) = refs[2 * pages:]
    s = pl.program_id(1)
    rows = N_HEADS * n_new
    rpg = HPG * n_new
    scale = HEAD_DIM ** -0.5
    n_blk = past // L_SEL + -(-n_new // L_SEL)
    n_cmp = ck_ref.shape[1]
    sel_lanes = selr.shape[1]
    pos = past + (_iota((rows, 1), 0) % n_new)
    new_page = lambda ref: _pad_rows(ref[0], PAGE_SIZE).astype(BF16)
    new_mask = (past + _iota((1, PAGE_SIZE), 1) <= pos) & (_iota((1, PAGE_SIZE), 1) < n_new)

    @pl.when(s == 0)
    def _():
        q2[...] = jnp.zeros(q2.shape, F32)
        for hh in range(N_HEADS):
            g = hh // HPG
            q2[hh * n_new:(hh + 1) * n_new, g * HEAD_DIM:(g + 1) * HEAD_DIM] = (
                q_ref[:, hh * HEAD_DIM:(hh + 1) * HEAD_DIM])
        qb = q2[...].astype(BF16)

        ci = _iota((1, n_cmp), 1)
        cmask = ci * S_CMP + (L_CMP - 1) <= pos
        p_c = _softmax_masked(_dot_nt(qb, ck_ref[0].astype(BF16)) * scale, cmask)
        oc[...] = _diag_blocks(_dot(p_c.astype(BF16), cv_ref[0].astype(BF16)), rpg)

        imp_c = jnp.concatenate(
            [sum(p_c[(g * HPG + h) * n_new:(g * HPG + h + 1) * n_new, :] for h in range(HPG))
             for g in range(N_KV)], axis=0)
        imp = _dot_f32_by_01(imp_c, _block_importance_matrix(n_cmp, sel_lanes, n_blk))
        pos_gt = past + (_iota((N_KV * n_new, 1), 0) % n_new)
        sel = _select_blocks(imp, pos_gt, n_blk)
        selr[...] = jnp.concatenate(
            [sel[(hh // HPG) * n_new:(hh // HPG + 1) * n_new, :] for hh in range(N_HEADS)], axis=0)

        wpos = (past - wb) + _iota((1, wb), 1)
        mask = jnp.concatenate([(wpos <= pos) & (wpos > pos - WINDOW), new_mask], axis=1)
        wk = _rows_by_group(wk_ref.at[0], wb).astype(BF16)
        wv = _rows_by_group(wv_ref.at[0], wb).astype(BF16)
        sc = jnp.concatenate([_dot_nt(qb, wk), _dot_nt(qb, new_page(nkw_ref))], axis=1) * scale
        p_w = _softmax_masked(sc, mask)
        ow[...] = _diag_blocks(
            _dot(p_w[:, 0:wb].astype(BF16), wv) + _dot(p_w[:, wb:].astype(BF16), new_page(nvw_ref)), rpg)

        keep = (wb - n_new) * N_KV
        wko_ref[0, 0:keep, :] = wk_ref[0, n_new * N_KV:wb * N_KV, :]
        wko_ref[0, keep:wb * N_KV, :] = nkw_rg[...]
        wvo_ref[0, 0:keep, :] = wv_ref[0, n_new * N_KV:wb * N_KV, :]
        wvo_ref[0, keep:wb * N_KV, :] = nvw_rg[...]

        m_s[...] = jnp.full(m_s.shape, NEG, F32)
        l_s[...] = jnp.zeros(l_s.shape, F32)
        acc[...] = jnp.zeros(acc.shape, F32)

    def page_step():
        qb = q2[...].astype(BF16)
        nk = pages * PAGE_SIZE
        sc = jnp.concatenate(
            [_dot_nt(qb, _rows_by_group(sk[i], PAGE_SIZE).astype(BF16)) for i in range(pages)], axis=1) * scale
        blk = s * (nk // L_SEL) + lax.shift_right_logical(_iota((sel_lanes, nk), 1), int(math.log2(L_SEL)))
        expand = jnp.where(blk == _iota((sel_lanes, nk), 0), 1.0, 0.0).astype(BF16)
        mask = _dot(selr[...].astype(BF16), expand) > 0.5

        def pv(p):
            return sum(_dot(p[:, i * PAGE_SIZE:(i + 1) * PAGE_SIZE].astype(BF16),
                            _rows_by_group(sv[i], PAGE_SIZE).astype(BF16)) for i in range(pages))

        _online_update(m_s, l_s, acc, sc, mask, pv)

    page_step()

    @pl.when(s == n_pool_steps - 1)
    def _():
        qb = q2[...].astype(BF16)
        new_blk = past // L_SEL
        mask = (selr[:, new_blk:new_blk + 1] > 0.5) & new_mask
        _online_update(m_s, l_s, acc, _dot_nt(qb, new_page(nks_ref)) * scale, mask,
                       lambda p: _dot(p.astype(BF16), new_page(nvs_ref)))
        o_s = _diag_blocks(_online_finish(l_s, acc), rpg)

        gates = jax.nn.sigmoid(gt_ref[...])

        def gate_col(br):
            return jnp.concatenate(
                [gates[:, (hh // HPG) * LANES + br * HPG + hh % HPG:(hh // HPG) * LANES + br * HPG + hh % HPG + 1]
                 for hh in range(N_HEADS)], axis=0)

        zr = jnp.concatenate([z_ref[:, hh * HEAD_DIM:(hh + 1) * HEAD_DIM] for hh in range(N_HEADS)], axis=0)
        o = gate_col(0) * oc[...] + gate_col(1) * o_s + gate_col(2) * ow[...]
        res = o * _silu(zr)
        for hh in range(N_HEADS):
            a_ref[:, hh * HEAD_DIM:(hh + 1) * HEAD_DIM] = res[hh * n_new:(hh + 1) * n_new, :]


def _nsa_sample_attend(page_table, pool_k, pool_v, kv, kw_rg, vw_rg, ck, cv, win_k, win_v, qz, gates, *, n_new):
    b, n_pages = page_table.shape
    pages = PAGES_PER_STEP
    n_pool_steps = n_pages // pages
    past = n_pages * PAGE_SIZE
    wb = win_k.shape[1] // N_KV
    rows = N_HEADS * n_new
    qw = N_HEADS * HEAD_DIM
    n_blk = past // L_SEL + -(-n_new // L_SEL)
    sel_lanes = -(-n_blk // LANES) * LANES
    newspec = lambda idx: pl.BlockSpec((1, n_new, KV_W), lambda bi, s, pt: (idx, bi, 0))
    newspec_rg = lambda: pl.BlockSpec((n_new * N_KV, HEAD_DIM), lambda bi, s, pt: (bi, 0))
    full = lambda arr: pl.BlockSpec((1,) + arr.shape[1:], lambda bi, s, pt: (bi, 0, 0))
    grid_spec = pltpu.PrefetchScalarGridSpec(
        num_scalar_prefetch=1,
        grid=(b, n_pool_steps),
        in_specs=(
            [_page_spec(i, pages) for i in range(pages)] * 2
            + [newspec(2), newspec(3), newspec(4), newspec(5), newspec_rg(), newspec_rg(),
               full(ck), full(cv), full(win_k), full(win_v),
               pl.BlockSpec((n_new, qw), lambda bi, s, pt: (bi, 0)),
               pl.BlockSpec((n_new, qw), lambda bi, s, pt: (bi, 1)),
               pl.BlockSpec((n_new, N_KV * LANES), lambda bi, s, pt: (bi, 0))]
        ),
        out_specs=[
            pl.BlockSpec((n_new, qw), lambda bi, s, pt: (bi, 0)),
            full(win_k), full(win_v),
        ],
        scratch_shapes=[
            pltpu.VMEM((rows, KV_W), F32),
            pltpu.VMEM((rows, sel_lanes), F32),
            pltpu.VMEM((rows, 1), F32),
            pltpu.VMEM((rows, 1), F32),
            pltpu.VMEM((rows, KV_W), F32),
            pltpu.VMEM((rows, HEAD_DIM), F32),
            pltpu.VMEM((rows, HEAD_DIM), F32),
        ],
    )
    body = functools.partial(_nsa_sample_body, pages=pages, n_pool_steps=n_pool_steps, past=past,
                             n_new=n_new, wb=wb)

    return pl.pallas_call(
        body,
        grid_spec=grid_spec,
        out_shape=[
            jax.ShapeDtypeStruct((b * n_new, qw), F32),
            jax.ShapeDtypeStruct(win_k.shape, F32),
            jax.ShapeDtypeStruct(win_v.shape, F32),
        ],
        compiler_params=_cparams(("parallel", "arbitrary")),
        name="nsa_sample_attend",
    )(page_table, *([pool_k] * pages), *([pool_v] * pages), kv, kv, kv, kv, kw_rg, vw_rg,
      ck, cv, win_k, win_v, qz, qz, gates)


def _rope_tables(pos, dk):
    half = dk // 2
    inv = 1.0 / (ROPE_BASE ** jnp.linspace(0.0, 1.0, half))
    ang = pos.astype(F32)[:, None] * inv[None, :]
    cos = jnp.repeat(jnp.cos(ang), 2, axis=1)
    sin = jnp.stack([-jnp.sin(ang), jnp.sin(ang)], axis=-1).reshape(pos.shape[0], dk)
    return cos, sin


def _pooling_matrix(w_pos):
    eye = jnp.eye(SUB_PER_PAGE, dtype=F32)
    first = jnp.einsum("mn,lg->gmnl", eye, w_pos[:S_CMP]).reshape(N_KV, SUB_PER_PAGE, PAGE_SIZE)
    second = jnp.einsum("mn,lg->gmnl", eye, w_pos[S_CMP:]).reshape(N_KV, SUB_PER_PAGE, PAGE_SIZE)
    return jnp.concatenate([first, second], axis=1).astype(BF16)


def _pooling_matrix_rg(w_pos):
    band = _pooling_matrix(w_pos).astype(F32).reshape(N_KV, 2, SUB_PER_PAGE, PAGE_SIZE)
    full = jnp.einsum("ghmr,gk->hmgrk", band, jnp.eye(N_KV, dtype=F32))
    return full.reshape(2 * SUB_ROWS, PAGE_ROWS).astype(BF16)


def _split_nsa_w_in(w):
    qw = N_HEADS * HEAD_DIM
    ng = 3 * N_HEADS
    d = w.shape[0]
    w_qz = w[:, :2 * qw]
    w_g = w[:, 2 * qw:2 * qw + ng].reshape(d, 3, N_KV, HPG).transpose(0, 2, 1, 3).reshape(d, N_KV, 3 * HPG)
    w_g = jnp.pad(w_g, ((0, 0), (0, 0), (0, LANES - 3 * HPG))).reshape(d, N_KV * LANES)
    w_kv = w[:, 2 * qw + ng:]
    return w_qz.astype(BF16), w_g.astype(BF16), w_kv.astype(BF16)


def _tile_rows(m, pref):
    return pref if m % pref == 0 else m


def _run_group(x, *, pos0, pool_buf, ret_s0, conv_buf, lru_h0, nsa_cache, params):
    nb, t, d = x.shape
    m = nb * t
    tm = _tile_rows(m, OUT_ROWS)
    tm_in = _tile_rows(m, IN_ROWS)
    tm_kv = _tile_rows(m, KV_ROWS)
    fused_layout = all(t % rows == 0 for rows in (tm, tm_in, tm_kv))
    nt = t // tm if fused_layout else 1
    bm_in = lambda rows: (lambda i, j: (i, 0))
    bm_io = lambda i: (i, 0)
    tb_in = lambda rows: (lambda i, j: (i % (t // rows), i // (t // rows)))
    tb_io = lambda i: (i % nt, i // nt)
    pos = pos0 + jnp.arange(t, dtype=jnp.int32)
    vec = lambda v: v.reshape(1, -1)

    depth = params["norm_pre"].shape[0]
    cur = x.reshape(m, d)
    cur_is_tm = False
    states = {k: [] for k in ("pool", "ret", "conv", "lru", "nsa")}

    def project(xv, x_in, g_pre, w):
        n = w.shape[1]
        tn = IN_COLS if n % IN_COLS == 0 else n
        return _norm_matmul(xv, x_in(tm_in), g_pre, w, m=m, tm=tm_in, tn=tn, out_shape=(m, n),
                            out_block=(tm_in, tn), out_map=lambda a, b_: (a, b_))

    def to_bm_view(a):
        return (a.reshape(t, nb * d), tb_in, tb_io) if fused_layout else (a, bm_in, bm_io)

    for i in range(depth):
        kind, j = i % 4, i // 4
        g_pre = vec(params["norm_pre"][i])
        g_post = vec(params["norm_post"][i])
        if kind == 0:
            w_in = params["pool_w_in"][j].astype(BF16)
            buf16 = jnp.pad(pool_buf[j], ((0, 0), (POOL_HALO - pool_buf[j].shape[1], 0), (0, 0)))
            a, st = _pool_mixer(cur, g_pre, w_in, buf16, params["pool_w_grp"][j].astype(BF16),
                                vec(params["pool_b_grp"][j]), vec(params["pool_scale"][j]),
                                b=nb, t=t, tt=_tile_rows(t, POOL_ROWS), pos0=pos0)
            states["pool"].append(st[:, 1:, :])
            cur = _out_proj(a, params["pool_w_out"][j].astype(BF16), cur, bm_io, g_post, m=m, tm=tm,
                            out_shape=(m, d), out_map=bm_io)
        elif kind == 1:
            w_in = params["ret_w_in"][j].astype(BF16)
            dk = d // R_HEADS
            dv = 2 * dk
            proj = project(cur, bm_in, g_pre, w_in)
            cos_t, sin_t = _rope_tables(pos, dk)
            lg = jnp.log(1.0 - 2.0 ** (-5.0 - jnp.arange(R_HEADS, dtype=F32)))
            a, s_new = _retention_mixer(proj, cos_t, sin_t, lg, ret_s0[j], b=nb, t=t, dk=dk, dv=dv)
            states["ret"].append(s_new)
            w_out = params["ret_w_out"][j].astype(BF16)
            if fused_layout:
                cur = _out_proj(a, w_out, cur, bm_io, g_post, m=m, tm=tm,
                                out_shape=(t, nb * d), out_map=tb_io).reshape(m, d)
            else:
                cur = _out_proj(a, w_out, cur, bm_io, g_post, m=m, tm=tm, out_shape=(m, d), out_map=bm_io)
                cur = cur.reshape(nb, t, d).swapaxes(0, 1).reshape(m, d)
            cur_is_tm = True
        elif kind == 2:
            assert cur_is_tm
            w_in = params["lru_w_in"][j].astype(BF16)
            d_rnn = w_in.shape[1] // 2
            uz = project(cur, bm_in, g_pre, w_in)
            buf = conv_buf[j].swapaxes(0, 1).reshape((CONV_W - 1) * nb, d_rnn)
            tt = max(1, min(t, LRU_ROWS // nb))
            a, cs_new, h_new = _lru_mixer(
                uz, buf, params["lru_conv_w"][j], vec(params["lru_conv_b"][j]),
                params["lru_w_a"][j].astype(BF16), vec(params["lru_b_a"][j]),
                params["lru_w_x"][j].astype(BF16), vec(params["lru_b_x"][j]),
                vec(params["lru_lam"][j]), lru_h0[j], nb=nb, t=t, tt=tt)
            states["conv"].append(cs_new.reshape(CONV_W - 1, nb, d_rnn).swapaxes(0, 1))
            states["lru"].append(h_new)
            cur = _out_proj(a, params["lru_w_out"][j].astype(BF16), cur, bm_io, g_post, m=m, tm=tm,
                            out_shape=(m, d), out_map=bm_io)
        else:
            if cur_is_tm and not fused_layout:
                cur = cur.reshape(t, nb, d).swapaxes(0, 1).reshape(m, d)
                cur_is_tm = False
            xv, x_in, x_io = to_bm_view(cur) if cur_is_tm else (cur, bm_in, bm_io)
            w_qz, w_g, w_kv = _split_nsa_w_in(params["nsa_w_in"][j])
            qz = project(xv, x_in, g_pre, w_qz)
            gates = project(xv, x_in, g_pre, w_g)
            n_kv_t = w_kv.shape[1] // KV_W
            kv, kv_rg = _norm_matmul_kv(xv, x_in(tm_kv), g_pre, w_kv, m=m, tm=tm_kv)
            pos_k, pos_v = params["nsa_cmp_pos_k"][j], params["nsa_cmp_pos_v"][j]
            lin_k, lin_v = params["nsa_cmp_lin_k"][j], params["nsa_cmp_lin_v"][j]
            kv5 = [a_.reshape(nb, t, N_KV, HEAD_DIM) for a_ in kv_rg]
            if nsa_cache is None:
                ck, cv = _compress_rows(kv, _pooling_matrix(pos_k), _pooling_matrix(pos_v),
                                        lin_k.astype(BF16), lin_v.astype(BF16), b=nb, t=t)
                a = _nsa_prompt_attend(qz, gates, ck, cv, kv, b=nb, t=t, tq=NSA_TQ, kc=NSA_KC)
                wb = min(WINDOW, t)
                new_state = (kv5[0], kv5[1], kv5[2], kv5[3], kv5[4][:, t - wb:], kv5[5][:, t - wb:])
            else:
                page_table, cmp_k, cmp_v, sel_k, sel_v, win_k, win_v = nsa_cache
                pool2d = lambda c: c[j].reshape(c.shape[1] * PAGE_ROWS, HEAD_DIM)
                win2d = lambda c: c[j].reshape(c.shape[1], c.shape[2] * N_KV, HEAD_DIM)
                new_pages = jnp.pad(jnp.stack(kv_rg[0:2]).reshape(2, nb, t * N_KV, HEAD_DIM),
                                    ((0, 0), (0, 0), (0, PAGE_ROWS - t * N_KV), (0, 0)))
                ck, cv = _compress_paged(page_table, pool2d(cmp_k), pool2d(cmp_v), new_pages,
                                         _pooling_matrix_rg(pos_k), _pooling_matrix_rg(pos_v),
                                         lin_k.astype(BF16), lin_v.astype(BF16))
                a, wk_new, wv_new = _nsa_sample_attend(page_table, pool2d(sel_k), pool2d(sel_v), kv, kv_rg[4], kv_rg[5], ck, cv,
                                                       win2d(win_k), win2d(win_v), qz, gates, n_new=t)
                wshape = (nb, win_k.shape[2], N_KV, HEAD_DIM)
                new_state = (kv5[0], kv5[1], kv5[2], kv5[3], wk_new.reshape(wshape), wv_new.reshape(wshape))
            states["nsa"].append(new_state)
            cur = _out_proj(a, params["nsa_w_out"][j].astype(BF16), xv, x_io, g_post, m=m, tm=tm,
                            out_shape=(m, d), out_map=bm_io)
            cur_is_tm = False
    if cur_is_tm:
        cur = cur.reshape(t, nb, d).swapaxes(0, 1).reshape(m, d)
    return cur.reshape(nb, t, d), states


def kernel(x_prompt, x_sample, state_pool, state_ret, state_conv, state_lru, cache_cmp_k, cache_cmp_v, cache_sel_k, cache_sel_v, cache_win_k, cache_win_v, page_table, norm_pre, norm_post, pool_w_in, pool_w_grp, pool_b_grp, pool_scale, pool_w_out, ret_w_in, ret_w_out, lru_w_in, lru_conv_w, lru_conv_b, lru_w_a, lru_b_a, lru_w_x, lru_b_x, lru_lam, lru_w_out, nsa_w_in, nsa_cmp_pos_k, nsa_cmp_lin_k, nsa_cmp_pos_v, nsa_cmp_lin_v, nsa_w_out):
    params = dict(
        norm_pre=norm_pre, norm_post=norm_post, pool_w_in=pool_w_in, pool_w_grp=pool_w_grp, pool_b_grp=pool_b_grp,
        pool_scale=pool_scale, pool_w_out=pool_w_out, ret_w_in=ret_w_in, ret_w_out=ret_w_out, lru_w_in=lru_w_in,
        lru_conv_w=lru_conv_w, lru_conv_b=lru_conv_b, lru_w_a=lru_w_a, lru_b_a=lru_b_a, lru_w_x=lru_w_x,
        lru_b_x=lru_b_x, lru_lam=lru_lam, lru_w_out=lru_w_out, nsa_w_in=nsa_w_in, nsa_cmp_pos_k=nsa_cmp_pos_k,
        nsa_cmp_lin_k=nsa_cmp_lin_k, nsa_cmp_pos_v=nsa_cmp_pos_v, nsa_cmp_lin_v=nsa_cmp_lin_v, nsa_w_out=nsa_w_out)
    b = x_prompt.shape[0]
    past = page_table.shape[1] * PAGE_SIZE
    zeros_like_state = lambda s: jnp.zeros((s.shape[0], b) + s.shape[2:], F32)
    yp, sp = _run_group(
        x_prompt, pos0=0, pool_buf=zeros_like_state(state_pool), ret_s0=zeros_like_state(state_ret),
        conv_buf=zeros_like_state(state_conv), lru_h0=zeros_like_state(state_lru), nsa_cache=None, params=params)
    ys, ss = _run_group(
        x_sample, pos0=past, pool_buf=state_pool, ret_s0=state_ret, conv_buf=state_conv, lru_h0=state_lru,
        nsa_cache=(page_table, cache_cmp_k, cache_cmp_v, cache_sel_k, cache_sel_v, cache_win_k, cache_win_v),
        params=params)
    out = [yp, ys]
    for key in ("pool", "ret", "conv", "lru"):
        out += [jnp.stack(sp[key]), jnp.stack(ss[key])]
    for idx in range(6):
        out += [jnp.stack([e[idx] for e in sp["nsa"]]), jnp.stack([e[idx] for e in ss["nsa"]])]
    return tuple(out)
```

```python
import functools
import math

import jax
import jax.numpy as jnp
from jax import lax
from jax.experimental import pallas as pl
from jax.experimental.pallas import tpu as pltpu

F32 = jnp.float32
BF16 = jnp.bfloat16

EPS = 1e-6
NEG = -1e30
FORCE = 1e6

PAGE_SIZE = 128
POOL_WINDOWS = (2, 4, 8, 16)
R_HEADS = 4
R_CHUNK = 128
ROPE_BASE = 10000.0
LRU_BW = 256
CONV_W = 4
LRU_C = 8.0
N_HEADS = 16
N_KV = 4
HPG = N_HEADS // N_KV
HEAD_DIM = 128
L_CMP = 32
S_CMP = 16
L_SEL = 64
N_SEL = 16
WINDOW = 512
KV_W = N_KV * HEAD_DIM
SUB_PER_PAGE = PAGE_SIZE // S_CMP
PAGES_PER_STEP = 32
LRU_ROWS = 1024
POOL_ROWS = 512

LANES = 128
SUBLANES = 8
VMEM_LIMIT = 56 * 1024 * 1024
IN_ROWS, IN_COLS = 2048, 1024
KV_ROWS = 1024
OUT_ROWS = 512
NSA_TQ, NSA_KC = 256, 512


def _cparams(sem):
    return pltpu.CompilerParams(dimension_semantics=sem, vmem_limit_bytes=VMEM_LIMIT)


def _act_dtype(rows):
    return BF16 if rows % (2 * SUBLANES) == 0 else F32


def _dot(a, b):
    return jnp.dot(a, b, preferred_element_type=F32)


def _dot_nt(a, b):
    return lax.dot_general(a, b, (((1,), (1,)), ((), ())), preferred_element_type=F32)


def _dot_tn(a, b):
    return lax.dot_general(a, b, (((0,), (0,)), ((), ())), preferred_element_type=F32)


def _silu(x):
    return x * jax.nn.sigmoid(x)


def _iota(shape, dim):
    return lax.broadcasted_iota(jnp.int32, shape, dim)


def _norm_mm_body(x_ref, g_ref, w_ref, o_ref, h_ref):
    @pl.when(pl.program_id(1) == 0)
    def _():
        x = x_ref[...]
        ms = jnp.mean(x * x, axis=-1, keepdims=True)
        h_ref[...] = (x * lax.rsqrt(ms + EPS) * g_ref[...]).astype(BF16)

    o_ref[...] = _dot(h_ref[...], w_ref[...]).reshape(o_ref.shape)


def _norm_matmul(x, x_map, g, w, *, m, tm, tn, out_shape, out_block, out_map):
    d, n = w.shape
    return pl.pallas_call(
        _norm_mm_body,
        grid=(m // tm, n // tn),
        in_specs=[
            pl.BlockSpec((tm, d), x_map),
            pl.BlockSpec((1, d), lambda i, j: (0, 0)),
            pl.BlockSpec((d, tn), lambda i, j: (0, j)),
        ],
        out_specs=pl.BlockSpec(out_block, out_map),
        out_shape=jax.ShapeDtypeStruct(out_shape, F32),
        scratch_shapes=[pltpu.VMEM((tm, d), BF16)],
        compiler_params=_cparams(("parallel", "arbitrary")),
        name="norm_matmul",
    )(x, g, w)


def _norm_mm_kv_body(x_ref, g_ref, w_ref, o_ref, *rest):
    oi_refs, h_ref = rest[:-1], rest[-1]
    j = pl.program_id(1)

    @pl.when(j == 0)
    def _():
        x = x_ref[...]
        ms = jnp.mean(x * x, axis=-1, keepdims=True)
        h_ref[...] = (x * lax.rsqrt(ms + EPS) * g_ref[...]).astype(BF16)

    r = _dot(h_ref[...], w_ref[...])
    o_ref[0] = r
    for jj, oi_ref in enumerate(oi_refs):
        @pl.when(j == jj)
        def _():
            for g in range(N_KV):
                oi_ref[pl.ds(g, r.shape[0], stride=N_KV), :] = r[:, g * HEAD_DIM:(g + 1) * HEAD_DIM]


def _norm_matmul_kv(x, x_map, g, w, *, m, tm):
    d, n = w.shape
    n_t = n // KV_W
    outs = pl.pallas_call(
        _norm_mm_kv_body,
        grid=(m // tm, n_t),
        in_specs=[
            pl.BlockSpec((tm, d), x_map),
            pl.BlockSpec((1, d), lambda i, j: (0, 0)),
            pl.BlockSpec((d, KV_W), lambda i, j: (0, j)),
        ],
        out_specs=[pl.BlockSpec((1, tm, KV_W), lambda i, j: (j, i, 0))]
        + [pl.BlockSpec((tm * N_KV, HEAD_DIM), lambda i, j: (i, 0)) for _ in range(n_t)],
        out_shape=[jax.ShapeDtypeStruct((n_t, m, KV_W), F32)]
        + [jax.ShapeDtypeStruct((m * N_KV, HEAD_DIM), F32) for _ in range(n_t)],
        scratch_shapes=[pltpu.VMEM((tm, d), BF16)],
        compiler_params=_cparams(("parallel", "arbitrary")),
        name="norm_matmul_kv",
    )(x, g, w)
    return outs[0], outs[1:]


def _out_proj_body(a_ref, w_ref, x_ref, g_ref, o_ref):
    y = _dot(a_ref[...].astype(BF16), w_ref[...])
    ms = jnp.mean(y * y, axis=-1, keepdims=True)
    o_ref[...] = x_ref[...] + y * lax.rsqrt(ms + EPS) * g_ref[...]


def _out_proj(a, w, x, x_map, g, *, m, tm, out_shape, out_map):
    e, d = w.shape
    return pl.pallas_call(
        _out_proj_body,
        grid=(m // tm,),
        in_specs=[
            pl.BlockSpec((tm, e), lambda i: (i, 0)),
            pl.BlockSpec((e, d), lambda i: (0, 0)),
            pl.BlockSpec((tm, d), x_map),
            pl.BlockSpec((1, d), lambda i: (0, 0)),
        ],
        out_specs=pl.BlockSpec((tm, d), out_map),
        out_shape=jax.ShapeDtypeStruct(out_shape, F32),
        compiler_params=_cparams(("parallel",)),
        name="out_proj",
    )(a, w, x, g)


POOL_HALO = 16
POOL_PAD = 8


def _pool_body(x_ref, g_ref, w_ref, buf_ref, wg_ref, bg_ref, sc_ref, a_ref, st_ref, ext, sa, sb,
               *, tt, pos0, gw):
    ti = pl.program_id(1)
    e = ext.shape[1]
    x = x_ref[...]
    ms = jnp.mean(x * x, axis=-1, keepdims=True)
    h = (x * lax.rsqrt(ms + EPS) * g_ref[...]).astype(BF16)
    lo = POOL_PAD
    top = POOL_PAD + POOL_HALO
    rows = top + tt

    @pl.when(ti == 0)
    def _():
        ext[0:lo, :] = jnp.zeros((lo, ext.shape[1]), F32)
        ext[lo:top, :] = buf_ref[0]

    @pl.when(ti > 0)
    def _():
        ext[lo:top, :] = ext[lo + tt:top + tt, :]

    u = _dot(h, w_ref[:, 0:e])
    ext[top:rows, :] = u
    sa[0:lo, :] = jnp.zeros((lo, gw), F32)
    sb[0:lo, :] = jnp.zeros((lo, gw), F32)
    pos = _iota((tt, 1), 0) + (pos0 + ti * tt)
    for g, w in enumerate(POOL_WINDOWS):
        cs = slice(g * gw, (g + 1) * gw)
        sa[lo:rows, :] = ext[lo:rows, cs] + ext[lo - 1:rows - 1, cs]
        cur, oth, sh = sa, sb, 2
        while sh < w:
            oth[lo:rows, :] = cur[lo:rows, :] + cur[lo - sh:rows - sh, :]
            cur, oth, sh = oth, cur, sh * 2
        cnt = jnp.minimum(pos + 1, w).astype(F32)
        mean = cur[top:rows, :] / cnt
        mixed = _dot((mean - u[:, cs]).astype(BF16), wg_ref[g]) + bg_ref[:, cs]
        z = _dot(h, w_ref[:, e + g * gw:e + (g + 1) * gw])
        a_ref[:, cs] = (mixed * sc_ref[:, cs] * _silu(z)).astype(a_ref.dtype)
    st_ref[0] = ext[lo + tt:top + tt, :]


def _pool_mixer(x, g_pre, w_in, buf16, wg, bg, sc, *, b, t, tt, pos0):
    d = x.shape[1]
    e = buf16.shape[-1]
    gw = e // len(POOL_WINDOWS)
    nt = t // tt
    rows = POOL_PAD + POOL_HALO + tt
    return pl.pallas_call(
        functools.partial(_pool_body, tt=tt, pos0=pos0, gw=gw),
        grid=(b, nt),
        in_specs=[
            pl.BlockSpec((tt, d), lambda i, j: (i * nt + j, 0)),
            pl.BlockSpec((1, d), lambda i, j: (0, 0)),
            pl.BlockSpec((d, 2 * e), lambda i, j: (0, 0)),
            pl.BlockSpec((1, POOL_HALO, e), lambda i, j: (i, 0, 0)),
            pl.BlockSpec((len(POOL_WINDOWS), gw, gw), lambda i, j: (0, 0, 0)),
            pl.BlockSpec((1, e), lambda i, j: (0, 0)),
            pl.BlockSpec((1, e), lambda i, j: (0, 0)),
        ],
        out_specs=[
            pl.BlockSpec((tt, e), lambda i, j: (i * nt + j, 0)),
            pl.BlockSpec((1, POOL_HALO, e), lambda i, j: (i, 0, 0)),
        ],
        out_shape=[
            jax.ShapeDtypeStruct((b * t, e), _act_dtype(tt)),
            jax.ShapeDtypeStruct((b, POOL_HALO, e), F32),
        ],
        scratch_shapes=[
            pltpu.VMEM((rows, e), F32),
            pltpu.VMEM((rows, gw), F32),
            pltpu.VMEM((rows, gw), F32),
        ],
        compiler_params=_cparams(("parallel", "arbitrary")),
        name="pool_mixer",
    )(x, g_pre, w_in, buf16, wg, bg, sc)


def _ret_body(lg_ref, q_ref, k_ref, v_ref, g_ref, cos_ref, sin_ref, s0_ref, a_ref, so_ref, s_scr,
              *, c_len, c_pad, dk, dv):
    c = pl.program_id(1)

    @pl.when(c == 0)
    def _():
        s_scr[...] = s0_ref[0]

    cosv = cos_ref[...]
    sinv = sin_ref[...]
    even = (_iota((c_len, dk), 1) & 1) == 0

    def rot(x):
        nxt = pltpu.roll(x, dk - 1, axis=1)
        prv = pltpu.roll(x, 1, axis=1)
        return x * cosv + jnp.where(even, nxt, prv) * sinv

    def pad(x):
        if c_pad == c_len:
            return x
        return jnp.concatenate([x, jnp.zeros((c_pad - c_len, x.shape[1]), x.dtype)], axis=0)

    i_col = _iota((c_pad, 1), 0).astype(F32)
    j_row = _iota((1, c_pad), 1).astype(F32)
    diff = i_col - j_row
    for h in range(R_HEADS):
        ks = slice(h * dk, (h + 1) * dk)
        vs = slice(h * dv, (h + 1) * dv)
        lg = lg_ref[h]
        q = pad(rot(q_ref[:, ks]))
        k = pad(rot(k_ref[:, ks]) * (dk ** -0.5))
        v = pad(v_ref[:, vs])
        intra = jnp.where(diff >= 0, jnp.exp(jnp.maximum(diff, 0.0) * lg), 0.0)
        q_dec = jnp.exp((i_col + 1.0) * lg)
        k_dec = jnp.exp((c_len - 1.0 - i_col) * lg)
        c_dec = jnp.exp(jnp.full((1, 1), float(c_len), F32) * lg)

        qb = q.astype(BF16)
        vb = v.astype(BF16)
        s = _dot_nt(qb, k.astype(BF16)) * intra
        s_old = s_scr[h]
        o = _dot(s.astype(BF16), vb) + _dot(qb, s_old.astype(BF16)) * q_dec
        s_scr[h] = s_old * c_dec + _dot_tn((k * k_dec).astype(BF16), vb)
        o = o[0:c_len]
        o = o * lax.rsqrt(jnp.mean(o * o, axis=-1, keepdims=True) + EPS)
        a_ref[:, vs] = (o * _silu(g_ref[:, vs])).astype(a_ref.dtype)

    @pl.when(c == pl.num_programs(1) - 1)
    def _():
        so_ref[0] = s_scr[...]


def _retention_mixer(proj, cos_t, sin_t, lg, s0, *, b, t, dk, dv):
    c_len = R_CHUNK if t % R_CHUNK == 0 else t
    c_pad = max(c_len, R_CHUNK)
    nc = t // c_len
    nh = R_HEADS
    qk_w, vg_w = nh * dk, nh * dv
    row = lambda i, c: i * nc + c
    return pl.pallas_call(
        functools.partial(_ret_body, c_len=c_len, c_pad=c_pad, dk=dk, dv=dv),
        grid=(b, nc),
        in_specs=[
            pl.BlockSpec(memory_space=pltpu.SMEM),
            pl.BlockSpec((c_len, qk_w), lambda i, c: (row(i, c), 0)),
            pl.BlockSpec((c_len, qk_w), lambda i, c: (row(i, c), 1)),
            pl.BlockSpec((c_len, vg_w), lambda i, c: (row(i, c), 2 * qk_w // vg_w)),
            pl.BlockSpec((c_len, vg_w), lambda i, c: (row(i, c), 2 * qk_w // vg_w + 1)),
            pl.BlockSpec((c_len, dk), lambda i, c: (c, 0)),
            pl.BlockSpec((c_len, dk), lambda i, c: (c, 0)),
            pl.BlockSpec((1, nh, dk, dv), lambda i, c: (i, 0, 0, 0)),
        ],
        out_specs=[
            pl.BlockSpec((c_len, vg_w), lambda i, c: (row(i, c), 0)),
            pl.BlockSpec((1, nh, dk, dv), lambda i, c: (i, 0, 0, 0)),
        ],
        out_shape=[
            jax.ShapeDtypeStruct((b * t, vg_w), _act_dtype(c_len)),
            jax.ShapeDtypeStruct((b, nh, dk, dv), F32),
        ],
        scratch_shapes=[pltpu.VMEM((nh, dk, dv), F32)],
        compiler_params=_cparams(("parallel", "arbitrary")),
        name="retention_mixer",
    )(lg, proj, proj, proj, proj, cos_t, sin_t, s0)


def _lru_body(u_ref, z_ref, buf_ref, cw_ref, cb_ref, wa_ref, ba_ref, wx_ref, bx_ref, lam_ref, h0_ref,
              a_ref, cso_ref, ho_ref, ext, a_s, b_s, h_s, hc, *, tt, nb):
    ti = pl.program_id(1)
    halo = (CONV_W - 1) * nb
    rows = tt * nb

    @pl.when(ti == 0)
    def _():
        ext[0:halo, :] = buf_ref[...]
        hc[...] = h0_ref[...]

    @pl.when(ti > 0)
    def _():
        ext[0:halo, :] = ext[rows:rows + halo, :]

    ext[halo:halo + rows, :] = u_ref[...]
    cw = cw_ref[...]
    c = cb_ref[...] + ext[0:rows, :] * cw[0:1, :]
    for kk in range(1, CONV_W):
        c = c + ext[kk * nb:kk * nb + rows, :] * cw[kk:kk + 1, :]
    cb16 = c.astype(BF16)
    r = jax.nn.sigmoid(_dot(cb16, wa_ref[0]) + ba_ref[...])
    ig = jax.nn.sigmoid(_dot(cb16, wx_ref[0]) + bx_ref[...])
    nl = -lam_ref[...]
    softplus = jnp.maximum(nl, 0.0) + jnp.log1p(jnp.exp(-jnp.abs(nl)))
    log_a = (-LRU_C) * r * softplus
    a = jnp.exp(log_a)
    a_s[...] = a
    b_s[...] = jnp.sqrt(1.0 - a * a) * (ig * c)

    def step(t, h):
        rs = pl.ds(pl.multiple_of(t * nb, SUBLANES), nb)
        h = a_s[rs, :] * h + b_s[rs, :]
        h_s[rs, :] = h
        return h

    hc[...] = lax.fori_loop(0, tt, step, hc[...], unroll=8)
    a_ref[...] = (h_s[...] * _silu(z_ref[...])).astype(BF16)
    cso_ref[...] = ext[rows:rows + halo, :]
    ho_ref[...] = hc[...]


def _lru_mixer(uz, buf, cw, cb, wa, ba, wx, bx, lam, h0, *, nb, t, tt):
    d_rnn = h0.shape[-1]
    nblk = d_rnn // LRU_BW
    nt = t // tt
    rows = tt * nb
    halo = (CONV_W - 1) * nb
    vec = lambda: pl.BlockSpec((1, LRU_BW), lambda j, i: (0, j))
    return pl.pallas_call(
        functools.partial(_lru_body, tt=tt, nb=nb),
        grid=(nblk, nt),
        in_specs=[
            pl.BlockSpec((rows, LRU_BW), lambda j, i: (i, j)),
            pl.BlockSpec((rows, LRU_BW), lambda j, i: (i, nblk + j)),
            pl.BlockSpec((halo, LRU_BW), lambda j, i: (0, j)),
            pl.BlockSpec((CONV_W, LRU_BW), lambda j, i: (0, j)),
            vec(),
            pl.BlockSpec((1, LRU_BW, LRU_BW), lambda j, i: (j, 0, 0)),
            vec(),
            pl.BlockSpec((1, LRU_BW, LRU_BW), lambda j, i: (j, 0, 0)),
            vec(),
            vec(),
            pl.BlockSpec((nb, LRU_BW), lambda j, i: (0, j)),
        ],
        out_specs=[
            pl.BlockSpec((rows, LRU_BW), lambda j, i: (i, j)),
            pl.BlockSpec((halo, LRU_BW), lambda j, i: (0, j)),
            pl.BlockSpec((nb, LRU_BW), lambda j, i: (0, j)),
        ],
        out_shape=[
            jax.ShapeDtypeStruct((t * nb, d_rnn), BF16),
            jax.ShapeDtypeStruct((halo, d_rnn), F32),
            jax.ShapeDtypeStruct((nb, d_rnn), F32),
        ],
        scratch_shapes=[
            pltpu.VMEM((halo + rows, LRU_BW), F32),
            pltpu.VMEM((rows, LRU_BW), F32),
            pltpu.VMEM((rows, LRU_BW), F32),
            pltpu.VMEM((rows, LRU_BW), F32),
            pltpu.VMEM((nb, LRU_BW), F32),
        ],
        compiler_params=_cparams(("parallel", "arbitrary")),
        name="lru_mixer",
    )(uz, uz, buf, cw, cb, wa, ba, wx, bx, lam, h0)


def _softmax_masked(s, mask):
    sm = jnp.where(mask, s, NEG)
    m = jnp.max(sm, axis=-1, keepdims=True)
    e = jnp.where(mask, jnp.exp(sm - m), 0.0)
    l = jnp.sum(e, axis=-1, keepdims=True)
    return e / jnp.where(l > 0.0, l, 1.0)


def _dot_f32_by_01(x, m01):
    hi = x.astype(BF16)
    r1 = x - hi.astype(F32)
    mid = r1.astype(BF16)
    lo = (r1 - mid.astype(F32)).astype(BF16)
    return _dot(hi, m01) + _dot(mid, m01) + _dot(lo, m01)


def _block_importance_matrix(n_rows, n_blk_lanes, n_blk, rows_per_cmp=1):
    r, s = L_SEL // S_CMP, L_CMP // S_CMP
    i = lax.shift_right_logical(_iota((n_rows, n_blk_lanes), 0), int(math.log2(rows_per_cmp)))
    j = _iota((n_rows, n_blk_lanes), 1)
    hit = (i >= r * j - (s - 1)) & (i <= r * j + (r - 1)) & (j < n_blk)
    return jnp.where(hit, 1.0, 0.0).astype(BF16)


def _select_blocks(imp, pos, n_blk):
    j = _iota(imp.shape, 1)
    cur = lax.shift_right_logical(pos, int(math.log2(L_SEL)))
    valid = j * L_SEL <= pos
    forced = (j == 0) | (j == cur) | (j == cur - 1)
    score = jnp.where(forced, FORCE, jnp.where(valid, imp, -1.0))
    cnt = jnp.zeros(imp.shape, F32)
    for i in range(n_blk):
        si = score[:, i:i + 1]
        ahead = (si > score) | ((si == score) & (j > i))
        cnt = cnt + jnp.where(ahead, 1.0, 0.0)
    keep = (cnt < float(min(N_SEL, n_blk))) & (score >= 0.0) & (j < n_blk)
    return jnp.where(keep, 1.0, 0.0)


def _select_blocks_t(imp, pos_row, n_blk):
    nb8 = -(-n_blk // SUBLANES) * SUBLANES
    imp_t = imp.T[0:nb8, :]
    j = _iota(imp_t.shape, 0)
    cur = lax.shift_right_logical(pos_row, int(math.log2(L_SEL)))
    valid = j * L_SEL <= pos_row
    forced = (j == 0) | (j == cur) | (j == cur - 1)
    score = jnp.where(forced, FORCE, jnp.where(valid, imp_t, -1.0))
    cnt = jnp.zeros(imp_t.shape, F32)
    for i in range(n_blk):
        si = score[i:i + 1, :]
        ahead = (si > score) | ((si == score) & (j > i))
        cnt = cnt + jnp.where(ahead, 1.0, 0.0)
    keep = (cnt < float(min(N_SEL, n_blk))) & (score >= 0.0) & (j < n_blk)
    sel_t = jnp.where(keep, 1.0, 0.0)
    if nb8 < LANES:
        sel_t = jnp.concatenate([sel_t, jnp.zeros((LANES - nb8, imp.shape[0]), F32)], axis=0)
    return sel_t.T


def _online_update(m_ref, l_ref, acc_ref, s, mask, pv):
    sm = jnp.where(mask, s, NEG)
    m_old = m_ref[...]
    m_new = jnp.maximum(m_old, jnp.max(sm, axis=-1, keepdims=True))
    p = jnp.where(mask, jnp.exp(sm - m_new), 0.0)
    alpha = jnp.exp(m_old - m_new)
    l_ref[...] = alpha * l_ref[...] + jnp.sum(p, axis=-1, keepdims=True)
    acc_ref[...] = alpha * acc_ref[...] + pv(p)
    m_ref[...] = m_new


def _online_finish(l_ref, acc_ref):
    l = l_ref[...]
    return acc_ref[...] / jnp.where(l > 0.0, l, 1.0)


def _cmp_accumulate(page, w2_ref, acc_a, acc_b, row0):
    pb = page.astype(BF16)
    for g in range(N_KV):
        cs = slice(g * HEAD_DIM, (g + 1) * HEAD_DIM)
        ab = _dot(w2_ref[g], pb[:, cs])
        acc_a[pl.ds(row0, SUB_PER_PAGE), cs] = ab[0:SUB_PER_PAGE]
        acc_b[pl.ds(row0, SUB_PER_PAGE), cs] = ab[SUB_PER_PAGE:2 * SUB_PER_PAGE]


def _cmp_finish(acc_a, acc_b, lin_ref, out_ref, ncp):
    pooled = acc_a[0:ncp, :] + acc_b[1:ncp + 1, :]
    for g in range(N_KV):
        cs = slice(g * HEAD_DIM, (g + 1) * HEAD_DIM)
        out_ref[0, :, cs] = _dot(pooled[:, cs].astype(BF16), lin_ref[g])


def _compress_rows_body(k_ref, v_ref, w2k_ref, w2v_ref, lk_ref, lv_ref, ok_ref, ov_ref,
                        aak, abk, aav, abv, *, pages, ncp):
    s = pl.program_id(1)

    @pl.when(s == 0)
    def _():
        tail = jnp.zeros((SUBLANES, KV_W), F32)
        abk[ncp:ncp + SUBLANES, :] = tail
        abv[ncp:ncp + SUBLANES, :] = tail

    for i in range(pages):
        row0 = pl.multiple_of((s * pages + i) * SUB_PER_PAGE, SUBLANES)
        rs = slice(i * PAGE_SIZE, (i + 1) * PAGE_SIZE)
        _cmp_accumulate(k_ref[0, rs, :], w2k_ref, aak, abk, row0)
        _cmp_accumulate(v_ref[0, rs, :], w2v_ref, aav, abv, row0)

    @pl.when(s == pl.num_programs(1) - 1)
    def _():
        _cmp_finish(aak, abk, lk_ref, ok_ref, ncp)
        _cmp_finish(aav, abv, lv_ref, ov_ref, ncp)


def _compress_rows(kv, w2k, w2v, lk, lv, *, b, t):
    pages = min(PAGES_PER_STEP, t // PAGE_SIZE)
    rows = pages * PAGE_SIZE
    ns = t // rows
    ncp = t // S_CMP
    wspec = lambda shp: pl.BlockSpec(shp, lambda i, s: (0, 0, 0))
    acc = lambda: pltpu.VMEM((ncp + SUBLANES, KV_W), F32)
    return pl.pallas_call(
        functools.partial(_compress_rows_body, pages=pages, ncp=ncp),
        grid=(b, ns),
        in_specs=[
            pl.BlockSpec((1, rows, KV_W), lambda i, s: (0, i * ns + s, 0)),
            pl.BlockSpec((1, rows, KV_W), lambda i, s: (1, i * ns + s, 0)),
            wspec(w2k.shape), wspec(w2v.shape), wspec(lk.shape), wspec(lv.shape),
        ],
        out_specs=[
            pl.BlockSpec((1, ncp, KV_W), lambda i, s: (i, 0, 0)),
            pl.BlockSpec((1, ncp, KV_W), lambda i, s: (i, 0, 0)),
        ],
        out_shape=[jax.ShapeDtypeStruct((b, ncp, KV_W), F32)] * 2,
        scratch_shapes=[acc(), acc(), acc(), acc()],
        compiler_params=_cparams(("parallel", "arbitrary")),
        name="compress_rows",
    )(kv, kv, w2k, w2v, lk, lv)


PAGE_ROWS = PAGE_SIZE * N_KV
SUB_ROWS = SUB_PER_PAGE * N_KV


def _cmp_accumulate_rg(page, w2_ref, acc_a, acc_b, row0):
    ab = _dot(w2_ref[...], page.astype(BF16))
    acc_a[pl.ds(row0, SUB_ROWS), :] = ab[0:SUB_ROWS]
    acc_b[pl.ds(row0, SUB_ROWS), :] = ab[SUB_ROWS:2 * SUB_ROWS]


def _cmp_finish_rg(acc_a, acc_b, lin_ref, out_ref, ncp):
    for g in range(N_KV):
        pooled = acc_a[pl.ds(g, ncp, stride=N_KV), :] + acc_b[pl.ds(N_KV + g, ncp, stride=N_KV), :]
        out_ref[0, :, g * HEAD_DIM:(g + 1) * HEAD_DIM] = _dot(pooled.astype(BF16), lin_ref[g])


def _compress_paged_body(pt_ref, *refs, pages, n_pool_steps, ncp):
    pk = refs[0:pages]
    pv = refs[pages:2 * pages]
    nk_ref, nv_ref, w2k_ref, w2v_ref, lk_ref, lv_ref, ok_ref, ov_ref, aak, abk, aav, abv = refs[2 * pages:]
    s = pl.program_id(1)

    for i in range(pages):
        row0 = pl.multiple_of((s * pages + i) * SUB_ROWS, SUB_ROWS)
        _cmp_accumulate_rg(pk[i][...], w2k_ref, aak, abk, row0)
        _cmp_accumulate_rg(pv[i][...], w2v_ref, aav, abv, row0)

    @pl.when(s == n_pool_steps - 1)
    def _():
        row0 = n_pool_steps * pages * SUB_ROWS
        _cmp_accumulate_rg(nk_ref[0, 0], w2k_ref, aak, abk, row0)
        _cmp_accumulate_rg(nv_ref[0, 0], w2v_ref, aav, abv, row0)
        _cmp_finish_rg(aak, abk, lk_ref, ok_ref, ncp)
        _cmp_finish_rg(aav, abv, lv_ref, ov_ref, ncp)


def _page_spec(i, pages):
    return pl.BlockSpec((PAGE_ROWS, HEAD_DIM), lambda bi, s, pt: (pt[bi, s * pages + i], 0))


def _compress_paged(page_table, pool_k, pool_v, new_pages, w2k, w2v, lk, lv):
    b, n_pages = page_table.shape
    pages = PAGES_PER_STEP
    n_pool_steps = n_pages // pages
    ncp = n_pages * SUB_PER_PAGE
    wspec = lambda shp: pl.BlockSpec(shp, lambda bi, s, pt: (0,) * len(shp))
    newspec = lambda idx: pl.BlockSpec((1, 1, PAGE_ROWS, HEAD_DIM), lambda bi, s, pt: (idx, bi, 0, 0))
    acc = lambda: pltpu.VMEM(((ncp + SUB_PER_PAGE) * N_KV, HEAD_DIM), F32)
    grid_spec = pltpu.PrefetchScalarGridSpec(
        num_scalar_prefetch=1,
        grid=(b, n_pool_steps),
        in_specs=(
            [_page_spec(i, pages) for i in range(pages)] * 2
            + [newspec(0), newspec(1), wspec(w2k.shape), wspec(w2v.shape), wspec(lk.shape), wspec(lv.shape)]
        ),
        out_specs=[
            pl.BlockSpec((1, ncp, KV_W), lambda bi, s, pt: (bi, 0, 0)),
            pl.BlockSpec((1, ncp, KV_W), lambda bi, s, pt: (bi, 0, 0)),
        ],
        scratch_shapes=[acc(), acc(), acc(), acc()],
    )
    return pl.pallas_call(
        functools.partial(_compress_paged_body, pages=pages, n_pool_steps=n_pool_steps, ncp=ncp),
        grid_spec=grid_spec,
        out_shape=[jax.ShapeDtypeStruct((b, ncp, KV_W), F32)] * 2,
        compiler_params=_cparams(("parallel", "arbitrary")),
        name="compress_paged",
    )(page_table, *([pool_k] * pages), *([pool_v] * pages), new_pages, new_pages, w2k, w2v, lk, lv)


LOG2E = 1.4426950408889634


def _lane_fold(x, op):
    out = x[:, 0:LANES]
    for u in range(1, x.shape[1] // LANES):
        out = op(out, x[:, u * LANES:(u + 1) * LANES])
    return out


def _nsa_prompt_body(q_ref, z_ref, gt_ref, ck_ref, cv_ref, ks_ref, vs_ref, kw_ref, vw_ref, a_ref,
                     ksb, vsb, kwb, vwb, s_buf, mx_s, acc_s, *, tq, t, kc, n_cmp):
    ones_col = jnp.where(_iota((t, LANES), 1) == 0, 1.0, 0.0).astype(BF16)
    ksb[...] = ks_ref[0].astype(BF16)
    vsb[...] = jnp.concatenate([vs_ref[0].astype(BF16), ones_col], axis=1)
    kwb[...] = kw_ref[0].astype(BF16)
    vwb[...] = jnp.concatenate([vw_ref[0].astype(BF16), ones_col], axis=1)

    n_blk = -(-t // L_SEL)
    ncl = ck_ref.shape[1]
    scale = HEAD_DIM ** -0.5
    c_exp = scale * LOG2E
    hs = [slice(h * HEAD_DIM, (h + 1) * HEAD_DIM) for h in range(HPG)]
    hr = [slice(h * tq, (h + 1) * tq) for h in range(HPG)]
    ckb = ck_ref[0].astype(BF16)
    cvb = cv_ref[0].astype(BF16)
    msel = _block_importance_matrix(ncl, LANES, n_blk)
    ws = min(WINDOW + tq, t)
    blk_shift = int(math.log2(L_SEL))

    def q_tile(qi, carry):
        q0 = pl.multiple_of(qi * tq, tq)
        rows = pl.ds(q0, tq)
        pos = q0 + _iota((tq, 1), 0)
        qb = jnp.concatenate([q_ref[rows, hs[h]] for h in range(HPG)], axis=0).astype(BF16)

        ci = _iota((1, ncl), 1)
        cmask = (ci * S_CMP + (L_CMP - 1) <= pos) & (ci < n_cmp)
        s_c = _dot_nt(qb, ckb) * scale
        p_c = [_softmax_masked(s_c[hr[h]], cmask) for h in range(HPG)]
        imp_c = p_c[0] + p_c[1] + p_c[2] + p_c[3]
        o_c = _dot(jnp.concatenate(p_c, axis=0).astype(BF16), cvb)

        imp = _dot_f32_by_01(imp_c, msel)
        selb = _select_blocks_t(imp, q0 + _iota((1, tq), 1), n_blk).astype(BF16)

        w0 = pl.multiple_of(jnp.minimum(jnp.maximum(q0 - WINDOW, 0), t - ws), tq)
        wpos = w0 + _iota((1, ws), 1)
        wbias = jnp.where((wpos <= pos) & (wpos > pos - WINDOW), 0.0, NEG)
        s_w = _dot_nt(qb, kwb[pl.ds(w0, ws), :])
        e_w = []
        for h in range(HPG):
            sh = s_w[hr[h]] + wbias
            e_w.append(jnp.exp2((sh - jnp.max(sh, axis=-1, keepdims=True)) * c_exp))
        pv_w = _dot(jnp.concatenate(e_w, axis=0).astype(BF16), vwb[pl.ds(w0, ws), :])
        o_w = pv_w[:, 0:HEAD_DIM] / pv_w[:, HEAD_DIM:HEAD_DIM + 1]

        mx_s[...] = jnp.full(mx_s.shape, NEG, F32)

        def pass1(ki, c):
            k0 = pl.multiple_of(ki * kc, kc)
            kpos = k0 + _iota((1, kc), 1)
            blk = lax.shift_right_logical(k0 + _iota((LANES, kc), 1), blk_shift)
            expand = jnp.where(blk == _iota((LANES, kc), 0), 1.0, 0.0).astype(BF16)
            bias = (_dot(selb, expand) - 1.0) * (-NEG)
            bias = jnp.where(kpos <= pos, bias, NEG)
            s = _dot_nt(qb, ksb[pl.ds(k0, kc), :])
            for h in range(HPG):
                sh = s[hr[h]] + bias
                s_buf[ki, hr[h], :] = sh
                mx_s[hr[h], :] = jnp.maximum(mx_s[hr[h], :], _lane_fold(sh, jnp.maximum))
            return c

        n_chunks = (q0 + tq + kc - 1) // kc
        lax.fori_loop(0, n_chunks, pass1, 0)
        m_b = jnp.broadcast_to(jnp.max(mx_s[...], axis=-1, keepdims=True), mx_s.shape)
        mx_s[...] = m_b
        acc_s[...] = jnp.zeros(acc_s.shape, F32)

        def pass2(ki, c):
            k0 = pl.multiple_of(ki * kc, kc)
            mb = mx_s[...]
            e = [jnp.exp2((s_buf[ki, :, u * LANES:(u + 1) * LANES] - mb) * c_exp) for u in range(kc // LANES)]
            acc_s[...] = acc_s[...] + _dot(jnp.concatenate(e, axis=1).astype(BF16), vsb[pl.ds(k0, kc), :])
            return c

        lax.fori_loop(0, n_chunks, pass2, 0)
        o_s = acc_s[:, 0:HEAD_DIM] / acc_s[:, HEAD_DIM:HEAD_DIM + 1]

        gates = jax.nn.sigmoid(gt_ref[rows, :])
        for h in range(HPG):
            o = (gates[:, h:h + 1] * o_c[hr[h]] + gates[:, HPG + h:HPG + h + 1] * o_s[hr[h]]
                 + gates[:, 2 * HPG + h:2 * HPG + h + 1] * o_w[hr[h]])
            a_ref[rows, hs[h]] = (o * _silu(z_ref[rows, hs[h]])).astype(BF16)
        return carry

    lax.fori_loop(0, t // tq, q_tile, 0)


def _nsa_prompt_attend(qz, gates, ck, cv, kv, *, b, t, tq, kc):
    gq = HPG * HEAD_DIM
    n_cmp = t // S_CMP - (L_CMP // S_CMP) + 1
    ncl = ck.shape[1]
    kvspec = lambda idx: pl.BlockSpec((1, t, HEAD_DIM), lambda i, g: (idx, i, g))
    kb = lambda: pltpu.VMEM((t, HEAD_DIM), BF16)
    vb = lambda: pltpu.VMEM((t, HEAD_DIM + LANES), BF16)
    return pl.pallas_call(
        functools.partial(_nsa_prompt_body, tq=tq, t=t, kc=kc, n_cmp=n_cmp),
        grid=(b, N_KV),
        in_specs=[
            pl.BlockSpec((t, gq), lambda i, g: (i, g)),
            pl.BlockSpec((t, gq), lambda i, g: (i, N_KV + g)),
            pl.BlockSpec((t, LANES), lambda i, g: (i, g)),
            pl.BlockSpec((1, ncl, HEAD_DIM), lambda i, g: (i, 0, g)),
            pl.BlockSpec((1, ncl, HEAD_DIM), lambda i, g: (i, 0, g)),
            kvspec(2), kvspec(3), kvspec(4), kvspec(5),
        ],
        out_specs=pl.BlockSpec((t, gq), lambda i, g: (i, g)),
        out_shape=jax.ShapeDtypeStruct((b * t, N_KV * gq), BF16),
        scratch_shapes=[
            kb(), vb(), kb(), vb(),
            pltpu.VMEM((t // kc, HPG * tq, kc), F32),
            pltpu.VMEM((HPG * tq, LANES), F32),
            pltpu.VMEM((HPG * tq, HEAD_DIM + LANES), F32),
        ],
        compiler_params=_cparams(("parallel", "parallel")),
        name="nsa_prompt_attend",
    )(qz, qz, gates, ck, cv, kv, kv, kv, kv)


def _rows_by_group(ref, n):
    return jnp.concatenate([ref[pl.ds(g, n, stride=N_KV), :] for g in range(N_KV)], axis=1)


def _diag_blocks(x, rows_per_group):
    return jnp.concatenate(
        [x[g * rows_per_group:(g + 1) * rows_per_group, g * HEAD_DIM:(g + 1) * HEAD_DIM]
         for g in range(N_KV)], axis=0)


def _pad_rows(x, n):
    return jnp.concatenate([x, jnp.zeros((n - x.shape[0], x.shape[1]), x.dtype)], axis=0)


def _nsa_sample_body(pt_ref, *refs, pages, n_pool_steps, past, n_new, wb):
    sk = refs[0:pages]
    sv = refs[pages:2 * pages]
    (nks_ref, nvs_ref, nkw_ref, nvw_ref, nkw_rg, nvw_rg, ck_ref, cv_ref, wk_ref, wv_ref, q_ref, z_ref, gt_ref,
     a_ref, wko_ref, wvo_ref, q2, selr, m_s, l_s, acc, oc, ow) = refs[2 * pages:]
    s = pl.program_id(1)
    rows = N_HEADS * n_new
    rpg = HPG * n_new
    scale = HEAD_DIM ** -0.5
    n_blk = past // L_SEL + -(-n_new // L_SEL)
    n_cmp = ck_ref.shape[1]
    sel_lanes = selr.shape[1]
    pos = past + (_iota((rows, 1), 0) % n_new)
    new_page = lambda ref: _pad_rows(ref[0], PAGE_SIZE).astype(BF16)
    new_mask = (past + _iota((1, PAGE_SIZE), 1) <= pos) & (_iota((1, PAGE_SIZE), 1) < n_new)

    @pl.when(s == 0)
    def _():
        q2[...] = jnp.zeros(q2.shape, F32)
        for hh in range(N_HEADS):
            g = hh // HPG
            q2[hh * n_new:(hh + 1) * n_new, g * HEAD_DIM:(g + 1) * HEAD_DIM] = (
                q_ref[:, hh * HEAD_DIM:(hh + 1) * HEAD_DIM])
        qb = q2[...].astype(BF16)

        ci = _iota((1, n_cmp), 1)
        cmask = ci * S_CMP + (L_CMP - 1) <= pos
        p_c = _softmax_masked(_dot_nt(qb, ck_ref[0].astype(BF16)) * scale, cmask)
        oc[...] = _diag_blocks(_dot(p_c.astype(BF16), cv_ref[0].astype(BF16)), rpg)

        imp_c = jnp.concatenate(
            [sum(p_c[(g * HPG + h) * n_new:(g * HPG + h + 1) * n_new, :] for h in range(HPG))
             for g in range(N_KV)], axis=0)
        imp = _dot_f32_by_01(imp_c, _block_importance_matrix(n_cmp, sel_lanes, n_blk))
        pos_gt = past + (_iota((N_KV * n_new, 1), 0) % n_new)
        sel = _select_blocks(imp, pos_gt, n_blk)
        selr[...] = jnp.concatenate(
            [sel[(hh // HPG) * n_new:(hh // HPG + 1) * n_new, :] for hh in range(N_HEADS)], axis=0)

        wpos = (past - wb) + _iota((1, wb), 1)
        mask = jnp.concatenate([(wpos <= pos) & (wpos > pos - WINDOW), new_mask], axis=1)
        wk = _rows_by_group(wk_ref.at[0], wb).astype(BF16)
        wv = _rows_by_group(wv_ref.at[0], wb).astype(BF16)
        sc = jnp.concatenate([_dot_nt(qb, wk), _dot_nt(qb, new_page(nkw_ref))], axis=1) * scale
        p_w = _softmax_masked(sc, mask)
        ow[...] = _diag_blocks(
            _dot(p_w[:, 0:wb].astype(BF16), wv) + _dot(p_w[:, wb:].astype(BF16), new_page(nvw_ref)), rpg)

        keep = (wb - n_new) * N_KV
        wko_ref[0, 0:keep, :] = wk_ref[0, n_new * N_KV:wb * N_KV, :]
        wko_ref[0, keep:wb * N_KV, :] = nkw_rg[...]
        wvo_ref[0, 0:keep, :] = wv_ref[0, n_new * N_KV:wb * N_KV, :]
        wvo_ref[0, keep:wb * N_KV, :] = nvw_rg[...]

        m_s[...] = jnp.full(m_s.shape, NEG, F32)
        l_s[...] = jnp.zeros(l_s.shape, F32)
        acc[...] = jnp.zeros(acc.shape, F32)

    def page_step():
        qb = q2[...].astype(BF16)
        nk = pages * PAGE_SIZE
        sc = jnp.concatenate(
            [_dot_nt(qb, _rows_by_group(sk[i], PAGE_SIZE).astype(BF16)) for i in range(pages)], axis=1) * scale
        blk = s * (nk // L_SEL) + lax.shift_right_logical(_iota((sel_lanes, nk), 1), int(math.log2(L_SEL)))
        expand = jnp.where(blk == _iota((sel_lanes, nk), 0), 1.0, 0.0).astype(BF16)
        mask = _dot(selr[...].astype(BF16), expand) > 0.5

        def pv(p):
            return sum(_dot(p[:, i * PAGE_SIZE:(i + 1) * PAGE_SIZE].astype(BF16),
                            _rows_by_group(sv[i], PAGE_SIZE).astype(BF16)) for i in range(pages))

        _online_update(m_s, l_s, acc, sc, mask, pv)

    page_step()

    @pl.when(s == n_pool_steps - 1)
    def _():
        qb = q2[...].astype(BF16)
        new_blk = past // L_SEL
        mask = (selr[:, new_blk:new_blk + 1] > 0.5) & new_mask
        _online_update(m_s, l_s, acc, _dot_nt(qb, new_page(nks_ref)) * scale, mask,
                       lambda p: _dot(p.astype(BF16), new_page(nvs_ref)))
        o_s = _diag_blocks(_online_finish(l_s, acc), rpg)

        gates = jax.nn.sigmoid(gt_ref[...])

        def gate_col(br):
            return jnp.concatenate(
                [gates[:, (hh // HPG) * LANES + br * HPG + hh % HPG:(hh // HPG) * LANES + br * HPG + hh % HPG + 1]
                 for hh in range(N_HEADS)], axis=0)

        zr = jnp.concatenate([z_ref[:, hh * HEAD_DIM:(hh + 1) * HEAD_DIM] for hh in range(N_HEADS)], axis=0)
        o = gate_col(0) * oc[...] + gate_col(1) * o_s + gate_col(2) * ow[...]
        res = o * _silu(zr)
        for hh in range(N_HEADS):
            a_ref[:, hh * HEAD_DIM:(hh + 1) * HEAD_DIM] = res[hh * n_new:(hh + 1) * n_new, :]


def _nsa_sample_attend(page_table, pool_k, pool_v, kv, kw_rg, vw_rg, ck, cv, win_k, win_v, qz, gates, *, n_new):
    b, n_pages = page_table.shape
    pages = PAGES_PER_STEP
    n_pool_steps = n_pages // pages
    past = n_pages * PAGE_SIZE
    wb = win_k.shape[1] // N_KV
    rows = N_HEADS * n_new
    qw = N_HEADS * HEAD_DIM
    n_blk = past // L_SEL + -(-n_new // L_SEL)
    sel_lanes = -(-n_blk // LANES) * LANES
    newspec = lambda idx: pl.BlockSpec((1, n_new, KV_W), lambda bi, s, pt: (idx, bi, 0))
    newspec_rg = lambda: pl.BlockSpec((n_new * N_KV, HEAD_DIM), lambda bi, s, pt: (bi, 0))
    full = lambda arr: pl.BlockSpec((1,) + arr.shape[1:], lambda bi, s, pt: (bi, 0, 0))
    grid_spec = pltpu.PrefetchScalarGridSpec(
        num_scalar_prefetch=1,
        grid=(b, n_pool_steps),
        in_specs=(
            [_page_spec(i, pages) for i in range(pages)] * 2
            + [newspec(2), newspec(3), newspec(4), newspec(5), newspec_rg(), newspec_rg(),
               full(ck), full(cv), full(win_k), full(win_v),
               pl.BlockSpec((n_new, qw), lambda bi, s, pt: (bi, 0)),
               pl.BlockSpec((n_new, qw), lambda bi, s, pt: (bi, 1)),
               pl.BlockSpec((n_new, N_KV * LANES), lambda bi, s, pt: (bi, 0))]
        ),
        out_specs=[
            pl.BlockSpec((n_new, qw), lambda bi, s, pt: (bi, 0)),
            full(win_k), full(win_v),
        ],
        scratch_shapes=[
            pltpu.VMEM((rows, KV_W), F32),
            pltpu.VMEM((rows, sel_lanes), F32),
            pltpu.VMEM((rows, 1), F32),
            pltpu.VMEM((rows, 1), F32),
            pltpu.VMEM((rows, KV_W), F32),
            pltpu.VMEM((rows, HEAD_DIM), F32),
            pltpu.VMEM((rows, HEAD_DIM), F32),
        ],
    )
    body = functools.partial(_nsa_sample_body, pages=pages, n_pool_steps=n_pool_steps, past=past,
                             n_new=n_new, wb=wb)

    return pl.pallas_call(
        body,
        grid_spec=grid_spec,
        out_shape=[
            jax.ShapeDtypeStruct((b * n_new, qw), F32),
            jax.ShapeDtypeStruct(win_k.shape, F32),
            jax.ShapeDtypeStruct(win_v.shape, F32),
        ],
        compiler_params=_cparams(("parallel", "arbitrary")),
        name="nsa_sample_attend",
    )(page_table, *([pool_k] * pages), *([pool_v] * pages), kv, kv, kv, kv, kw_rg, vw_rg,
      ck, cv, win_k, win_v, qz, qz, gates)


def _rope_tables(pos, dk):
    half = dk // 2
    inv = 1.0 / (ROPE_BASE ** jnp.linspace(0.0, 1.0, half))
    ang = pos.astype(F32)[:, None] * inv[None, :]
    cos = jnp.repeat(jnp.cos(ang), 2, axis=1)
    sin = jnp.stack([-jnp.sin(ang), jnp.sin(ang)], axis=-1).reshape(pos.shape[0], dk)
    return cos, sin


def _pooling_matrix(w_pos):
    eye = jnp.eye(SUB_PER_PAGE, dtype=F32)
    first = jnp.einsum("mn,lg->gmnl", eye, w_pos[:S_CMP]).reshape(N_KV, SUB_PER_PAGE, PAGE_SIZE)
    second = jnp.einsum("mn,lg->gmnl", eye, w_pos[S_CMP:]).reshape(N_KV, SUB_PER_PAGE, PAGE_SIZE)
    return jnp.concatenate([first, second], axis=1).astype(BF16)


def _pooling_matrix_rg(w_pos):
    band = _pooling_matrix(w_pos).astype(F32).reshape(N_KV, 2, SUB_PER_PAGE, PAGE_SIZE)
    full = jnp.einsum("ghmr,gk->hmgrk", band, jnp.eye(N_KV, dtype=F32))
    return full.reshape(2 * SUB_ROWS, PAGE_ROWS).astype(BF16)


def _split_nsa_w_in(w):
    qw = N_HEADS * HEAD_DIM
    ng = 3 * N_HEADS
    d = w.shape[0]
    w_qz = w[:, :2 * qw]
    w_g = w[:, 2 * qw:2 * qw + ng].reshape(d, 3, N_KV, HPG).transpose(0, 2, 1, 3).reshape(d, N_KV, 3 * HPG)
    w_g = jnp.pad(w_g, ((0, 0), (0, 0), (0, LANES - 3 * HPG))).reshape(d, N_KV * LANES)
    w_kv = w[:, 2 * qw + ng:]
    return w_qz.astype(BF16), w_g.astype(BF16), w_kv.astype(BF16)


def _tile_rows(m, pref):
    return pref if m % pref == 0 else m


def _run_group(x, *, pos0, pool_buf, ret_s0, conv_buf, lru_h0, nsa_cache, params):
    nb, t, d = x.shape
    m = nb * t
    tm = _tile_rows(m, OUT_ROWS)
    tm_in = _tile_rows(m, IN_ROWS)
    tm_kv = _tile_rows(m, KV_ROWS)
    fused_layout = all(t % rows == 0 for rows in (tm, tm_in, tm_kv))
    nt = t // tm if fused_layout else 1
    bm_in = lambda rows: (lambda i, j: (i, 0))
    bm_io = lambda i: (i, 0)
    tb_in = lambda rows: (lambda i, j: (i % (t // rows), i // (t // rows)))
    tb_io = lambda i: (i % nt, i // nt)
    pos = pos0 + jnp.arange(t, dtype=jnp.int32)
    vec = lambda v: v.reshape(1, -1)

    depth = params["norm_pre"].shape[0]
    cur = x.reshape(m, d)
    cur_is_tm = False
    states = {k: [] for k in ("pool", "ret", "conv", "lru", "nsa")}

    def project(xv, x_in, g_pre, w):
        n = w.shape[1]
        tn = IN_COLS if n % IN_COLS == 0 else n
        return _norm_matmul(xv, x_in(tm_in), g_pre, w, m=m, tm=tm_in, tn=tn, out_shape=(m, n),
                            out_block=(tm_in, tn), out_map=lambda a, b_: (a, b_))

    def to_bm_view(a):
        return (a.reshape(t, nb * d), tb_in, tb_io) if fused_layout else (a, bm_in, bm_io)

    for i in range(depth):
        kind, j = i % 4, i // 4
        g_pre = vec(params["norm_pre"][i])
        g_post = vec(params["norm_post"][i])
        if kind == 0:
            w_in = params["pool_w_in"][j].astype(BF16)
            buf16 = jnp.pad(pool_buf[j], ((0, 0), (POOL_HALO - pool_buf[j].shape[1], 0), (0, 0)))
            a, st = _pool_mixer(cur, g_pre, w_in, buf16, params["pool_w_grp"][j].astype(BF16),
                                vec(params["pool_b_grp"][j]), vec(params["pool_scale"][j]),
                                b=nb, t=t, tt=_tile_rows(t, POOL_ROWS), pos0=pos0)
            states["pool"].append(st[:, 1:, :])
            cur = _out_proj(a, params["pool_w_out"][j].astype(BF16), cur, bm_io, g_post, m=m, tm=tm,
                            out_shape=(m, d), out_map=bm_io)
        elif kind == 1:
            w_in = params["ret_w_in"][j].astype(BF16)
            dk = d // R_HEADS
            dv = 2 * dk
            proj = project(cur, bm_in, g_pre, w_in)
            cos_t, sin_t = _rope_tables(pos, dk)
            lg = jnp.log(1.0 - 2.0 ** (-5.0 - jnp.arange(R_HEADS, dtype=F32)))
            a, s_new = _retention_mixer(proj, cos_t, sin_t, lg, ret_s0[j], b=nb, t=t, dk=dk, dv=dv)
            states["ret"].append(s_new)
            w_out = params["ret_w_out"][j].astype(BF16)
            if fused_layout:
                cur = _out_proj(a, w_out, cur, bm_io, g_post, m=m, tm=tm,
                                out_shape=(t, nb * d), out_map=tb_io).reshape(m, d)
            else:
                cur = _out_proj(a, w_out, cur, bm_io, g_post, m=m, tm=tm, out_shape=(m, d), out_map=bm_io)
                cur = cur.reshape(nb, t, d).swapaxes(0, 1).reshape(m, d)
            cur_is_tm = True
        elif kind == 2:
            assert cur_is_tm
            w_in = params["lru_w_in"][j].astype(BF16)
            d_rnn = w_in.shape[1] // 2
            uz = project(cur, bm_in, g_pre, w_in)
            buf = conv_buf[j].swapaxes(0, 1).reshape((CONV_W - 1) * nb, d_rnn)
            tt = max(1, min(t, LRU_ROWS // nb))
            a, cs_new, h_new = _lru_mixer(
                uz, buf, params["lru_conv_w"][j], vec(params["lru_conv_b"][j]),
                params["lru_w_a"][j].astype(BF16), vec(params["lru_b_a"][j]),
                params["lru_w_x"][j].astype(BF16), vec(params["lru_b_x"][j]),
                vec(params["lru_lam"][j]), lru_h0[j], nb=nb, t=t, tt=tt)
            states["conv"].append(cs_new.reshape(CONV_W - 1, nb, d_rnn).swapaxes(0, 1))
            states["lru"].append(h_new)
            cur = _out_proj(a, params["lru_w_out"][j].astype(BF16), cur, bm_io, g_post, m=m, tm=tm,
                            out_shape=(m, d), out_map=bm_io)
        else:
            if cur_is_tm and not fused_layout:
                cur = cur.reshape(t, nb, d).swapaxes(0, 1).reshape(m, d)
                cur_is_tm = False
            xv, x_in, x_io = to_bm_view(cur) if cur_is_tm else (cur, bm_in, bm_io)
            w_qz, w_g, w_kv = _split_nsa_w_in(params["nsa_w_in"][j])
            qz = project(xv, x_in, g_pre, w_qz)
            gates = project(xv, x_in, g_pre, w_g)
            n_kv_t = w_kv.shape[1] // KV_W
            kv, kv_rg = _norm_matmul_kv(xv, x_in(tm_kv), g_pre, w_kv, m=m, tm=tm_kv)
            pos_k, pos_v = params["nsa_cmp_pos_k"][j], params["nsa_cmp_pos_v"][j]
            lin_k, lin_v = params["nsa_cmp_lin_k"][j], params["nsa_cmp_lin_v"][j]
            kv5 = [a_.reshape(nb, t, N_KV, HEAD_DIM) for a_ in kv_rg]
            if nsa_cache is None:
                ck, cv = _compress_rows(kv, _pooling_matrix(pos_k), _pooling_matrix(pos_v),
                                        lin_k.astype(BF16), lin_v.astype(BF16), b=nb, t=t)
                a = _nsa_prompt_attend(qz, gates, ck, cv, kv, b=nb, t=t, tq=NSA_TQ, kc=NSA_KC)
                wb = min(WINDOW, t)
                new_state = (kv5[0], kv5[1], kv5[2], kv5[3], kv5[4][:, t - wb:], kv5[5][:, t - wb:])
            else:
                page_table, cmp_k, cmp_v, sel_k, sel_v, win_k, win_v = nsa_cache
                pool2d = lambda c: c[j].reshape(c.shape[1] * PAGE_ROWS, HEAD_DIM)
                win2d = lambda c: c[j].reshape(c.shape[1], c.shape[2] * N_KV, HEAD_DIM)
                new_pages = jnp.pad(jnp.stack(kv_rg[0:2]).reshape(2, nb, t * N_KV, HEAD_DIM),
                                    ((0, 0), (0, 0), (0, PAGE_ROWS - t * N_KV), (0, 0)))
                ck, cv = _compress_paged(page_table, pool2d(cmp_k), pool2d(cmp_v), new_pages,
                                         _pooling_matrix_rg(pos_k), _pooling_matrix_rg(pos_v),
                                         lin_k.astype(BF16), lin_v.astype(BF16))
                a, wk_new, wv_new = _nsa_sample_attend(page_table, pool2d(sel_k), pool2d(sel_v), kv, kv_rg[4], kv_rg[5], ck, cv,
                                                       win2d(win_k), win2d(win_v), qz, gates, n_new=t)
                wshape = (nb, win_k.shape[2], N_KV, HEAD_DIM)
                new_state = (kv5[0], kv5[1], kv5[2], kv5[3], wk_new.reshape(wshape), wv_new.reshape(wshape))
            states["nsa"].append(new_state)
            cur = _out_proj(a, params["nsa_w_out"][j].astype(BF16), xv, x_io, g_post, m=m, tm=tm,
                            out_shape=(m, d), out_map=bm_io)
            cur_is_tm = False
    if cur_is_tm:
        cur = cur.reshape(t, nb, d).swapaxes(0, 1).reshape(m, d)
    return cur.reshape(nb, t, d), states


def kernel(x_prompt, x_sample, state_pool, state_ret, state_conv, state_lru, cache_cmp_k, cache_cmp_v, cache_sel_k, cache_sel_v, cache_win_k, cache_win_v, page_table, norm_pre, norm_post, pool_w_in, pool_w_grp, pool_b_grp, pool_scale, pool_w_out, ret_w_in, ret_w_out, lru_w_in, lru_conv_w, lru_conv_b, lru_w_a, lru_b_a, lru_w_x, lru_b_x, lru_lam, lru_w_out, nsa_w_in, nsa_cmp_pos_k, nsa_cmp_lin_k, nsa_cmp_pos_v, nsa_cmp_lin_v, nsa_w_out):
    params = dict(
        norm_pre=norm_pre, norm_post=norm_post, pool_w_in=pool_w_in, pool_w_grp=pool_w_grp, pool_b_grp=pool_b_grp,
        pool_scale=pool_scale, pool_w_out=pool_w_out, ret_w_in=ret_w_in, ret_w_out=ret_w_out, lru_w_in=lru_w_in,
        lru_conv_w=lru_conv_w, lru_conv_b=lru_conv_b, lru_w_a=lru_w_a, lru_b_a=lru_b_a, lru_w_x=lru_w_x,
        lru_b_x=lru_b_x, lru_lam=lru_lam, lru_w_out=lru_w_out, nsa_w_in=nsa_w_in, nsa_cmp_pos_k=nsa_cmp_pos_k,
        nsa_cmp_lin_k=nsa_cmp_lin_k, nsa_cmp_pos_v=nsa_cmp_pos_v, nsa_cmp_lin_v=nsa_cmp_lin_v, nsa_w_out=nsa_w_out)
    b = x_prompt.shape[0]
    past = page_table.shape[1] * PAGE_SIZE
    zeros_like_state = lambda s: jnp.zeros((s.shape[0], b) + s.shape[2:], F32)
    yp, sp = _run_group(
        x_prompt, pos0=0, pool_buf=zeros_like_state(state_pool), ret_s0=zeros_like_state(state_ret),
        conv_buf=zeros_like_state(state_conv), lru_h0=zeros_like_state(state_lru), nsa_cache=None, params=params)
    ys, ss = _run_group(
        x_sample, pos0=past, pool_buf=state_pool, ret_s0=state_ret, conv_buf=state_conv, lru_h0=state_lru,
        nsa_cache=(page_table, cache_cmp_k, cache_cmp_v, cache_sel_k, cache_sel_v, cache_win_k, cache_win_v),
        params=params)
    out = [yp, ys]
    for key in ("pool", "ret", "conv", "lru"):
        out += [jnp.stack(sp[key]), jnp.stack(ss[key])]
    for idx in range(6):
        out += [jnp.stack([e[idx] for e in sp["nsa"]]), jnp.stack([e[idx] for e in ss["nsa"]])]
    return tuple(out)
```
